```python
import jax, jax.numpy as jnp
from jax import lax
import numpy as np

D_MODEL = 1024
BATCH = 32
SEQ = 256
DEPTH = 2
DEC_BATCH = 4
DEC_SEQ = 4096
PAST_LEN = 256

GRID_W = 64
D_MIX = D_MODEL
ATTN_WIDTH = D_MIX // 2
LRU_WIDTH = D_MIX // 4
CONV_WIDTH = D_MIX - ATTN_WIDTH - LRU_WIDTH
HEAD_DIM = 64
N_HEADS = ATTN_WIDTH // HEAD_DIM
N_KV_HEADS = 2
KV_GROUP = N_HEADS // N_KV_HEADS
KV_WIDTH = N_KV_HEADS * HEAD_DIM
ROPE_BASE = 10000.0
Q_BLOCK = 128
LRU_HEADS = 4
LRU_HEAD_DIM = LRU_WIDTH // LRU_HEADS
LRU_CONV = 4
RG_C = 8.0
CM_KERNEL = 31
N_EXPERTS = 256
TOP_K = 8
EXPERT_FF = D_MODEL // 4
SHARED_FF = EXPERT_FF
ROUTED_SCALE = 2.5
MOE_BLOCK = 128
EPS = 1e-6
IN_WIDTH = ATTN_WIDTH + 2 * KV_WIDTH + 2 * LRU_WIDTH + 2 * CONV_WIDTH
IN_SPLITS = (ATTN_WIDTH,
             ATTN_WIDTH + KV_WIDTH,
             ATTN_WIDTH + 2 * KV_WIDTH,
             ATTN_WIDTH + 2 * KV_WIDTH + LRU_WIDTH,
             ATTN_WIDTH + 2 * KV_WIDTH + 2 * LRU_WIDTH)

kernel_name = "hybrid_diffusion_prefix_trunk_step"


def rms_norm(x, g):
    xf = x.astype(jnp.float32)
    y = xf * lax.rsqrt(jnp.mean(xf * xf, axis=-1, keepdims=True) + EPS)
    return (y * g.astype(jnp.float32)).astype(x.dtype)


def layer_norm(x, g, b):
    xf = x.astype(jnp.float32)
    mu = jnp.mean(xf, axis=-1, keepdims=True)
    var = jnp.mean(jnp.square(xf - mu), axis=-1, keepdims=True)
    y = (xf - mu) * lax.rsqrt(var + EPS)
    return (y * g.astype(jnp.float32) + b.astype(jnp.float32)).astype(x.dtype)


def axial_rope_tables(n_tok):
    rows = n_tok // GRID_W
    row = jnp.repeat(jnp.arange(rows, dtype=jnp.float32), GRID_W)
    col = jnp.tile(jnp.arange(GRID_W, dtype=jnp.float32), rows)
    n_freq = HEAD_DIM // 4
    inv = jnp.power(ROPE_BASE, -jnp.arange(n_freq, dtype=jnp.float32) / n_freq)
    ang = jnp.concatenate([row[:, None] * inv, col[:, None] * inv], axis=-1)
    return jnp.cos(ang), jnp.sin(ang)


def apply_rope(x, cos, sin):
    xf = x.astype(jnp.float32).reshape(*x.shape[:-1], HEAD_DIM // 2, 2)
    x0, x1 = xf[..., 0], xf[..., 1]
    c = cos[None, :, None, :]
    s = sin[None, :, None, :]
    out = jnp.stack([x0 * c - x1 * s, x0 * s + x1 * c], axis=-1)
    return out.reshape(x.shape).astype(x.dtype)


def block_attention(q, k, v):
    bsz, s_len = q.shape[0], q.shape[1]
    nb = s_len // Q_BLOCK
    qb = (q.astype(jnp.float32) * HEAD_DIM ** -0.5).reshape(
        bsz, nb, Q_BLOCK, N_KV_HEADS, KV_GROUP, HEAD_DIM).transpose(1, 0, 2, 3, 4, 5)
    kf = k.astype(jnp.float32)

    def one_block(qblk):
        s = jnp.einsum('bqkgd,btkd->bkgqt', qblk, kf)
        p = jax.nn.softmax(s, axis=-1).astype(v.dtype)
        return jnp.einsum('bkgqt,btkd->bqkgd', p, v)

    o = lax.map(one_block, qb)
    return o.transpose(1, 0, 2, 3, 4, 5).reshape(bsz, s_len, ATTN_WIDTH)


def depthwise_conv(x, w, b, pad):
    y = lax.conv_general_dilated(x, w[:, None, :].astype(x.dtype), (1,), [pad],
                                 dimension_numbers=('NWC', 'WIO', 'NWC'),
                                 feature_group_count=x.shape[-1])
    return y + b.astype(x.dtype)


def linear_scan(log_a, xin, h0, reverse):
    a = jnp.exp(log_a)
    b = jnp.sqrt(-jnp.expm1(2.0 * log_a)) * xin

    def step(h, ab):
        h = ab[0] * h + ab[1]
        return h, h

    h_last, hs = lax.scan(step, h0, (jnp.swapaxes(a, 0, 1), jnp.swapaxes(b, 0, 1)), reverse=reverse)
    return jnp.swapaxes(hs, 0, 1), h_last


def rglru_mixer(u, gate, conv_w, conv_b, w_a, b_a, w_x, b_x, lam, h0):
    bsz, t_len, _ = u.shape
    xc = depthwise_conv(u, conv_w, conv_b, ((LRU_CONV - 1) // 2, LRU_CONV // 2))
    xf = xc.astype(jnp.float32)
    xh = xf.reshape(bsz, t_len, LRU_HEADS, LRU_HEAD_DIM)
    outs, finals = [], []
    for d, rev in ((0, False), (1, True)):
        r = jax.nn.sigmoid(jnp.einsum('btnc,ncd->btnd', xh, w_a[d].astype(jnp.float32)).reshape(
            bsz, t_len, LRU_WIDTH) + b_a[d].astype(jnp.float32))
        i = jax.nn.sigmoid(jnp.einsum('btnc,ncd->btnd', xh, w_x[d].astype(jnp.float32)).reshape(
            bsz, t_len, LRU_WIDTH) + b_x[d].astype(jnp.float32))
        log_a = -RG_C * r * jax.nn.softplus(-lam[d].astype(jnp.float32))
        hs, h_last = linear_scan(log_a, i * xf, h0[:, d].astype(jnp.float32), rev)
        outs.append(hs)
        finals.append(h_last)
    y = ((outs[0] + outs[1]) * jax.nn.gelu(gate.astype(jnp.float32))).astype(u.dtype)
    return y, jnp.stack(finals, axis=1)


def conv_module(p, dw_w, dw_b, ln_g, ln_b):
    a, g = jnp.split(p, 2, axis=-1)
    h = a * jax.nn.sigmoid(g)
    h = depthwise_conv(h, dw_w, dw_b, (CM_KERNEL // 2, CM_KERNEL // 2))
    h = layer_norm(h, ln_g, ln_b)
    return jax.nn.silu(h)


def swiglu(x, wg, wu, wd):
    return (jax.nn.silu(x @ wg) * (x @ wu)) @ wd


def moe_ffn(h, w_router, b_router, w_e_gate, w_e_up, w_e_down, w_s_gate, w_s_up, w_s_down):
    shape = h.shape
    d = shape[-1]
    t = h.reshape(-1, d)
    n_tok = t.shape[0]
    n_assign = n_tok * TOP_K
    scores = jax.nn.sigmoid(t.astype(jnp.float32) @ w_router.astype(jnp.float32))
    _, idx = lax.top_k(scores + b_router.astype(jnp.float32), TOP_K)
    gw = jnp.take_along_axis(scores, idx, axis=-1)
    gw = ROUTED_SCALE * gw / jnp.sum(gw, axis=-1, keepdims=True)
    flat_e = idx.reshape(-1)
    order = jnp.argsort(flat_e)
    sorted_e = flat_e[order]
    counts = jnp.bincount(flat_e, length=N_EXPERTS)
    padded = (counts + MOE_BLOCK - 1) // MOE_BLOCK * MOE_BLOCK
    pad_end = jnp.cumsum(padded)
    pad_start = pad_end - padded
    cnt_start = jnp.cumsum(counts) - counts
    dest = pad_start[sorted_e] + jnp.arange(n_assign, dtype=pad_end.dtype) - cnt_start[sorted_e]
    n_blocks = -(-n_assign // MOE_BLOCK) + N_EXPERTS
    n_slots = n_blocks * MOE_BLOCK
    slot_tok = jnp.full((n_slots,), n_tok, jnp.int32).at[dest].set((order // TOP_K).astype(jnp.int32))
    slot_w = jnp.zeros((n_slots,), jnp.float32).at[dest].set(gw.reshape(-1)[order])
    blk_e = jnp.minimum(jnp.searchsorted(pad_end, jnp.arange(n_blocks, dtype=pad_end.dtype) * MOE_BLOCK,
                                         side='right'), N_EXPERTS - 1)
    t_pad = jnp.concatenate([t, jnp.zeros((1, d), t.dtype)], axis=0)
    xs = t_pad[slot_tok].reshape(n_blocks, MOE_BLOCK, d)

    def expert_block(args):
        xb, e = args
        return swiglu(xb, w_e_gate[e], w_e_up[e], w_e_down[e])

    ys = lax.map(expert_block, (xs, blk_e)).reshape(n_slots, d)
    ys = ys * slot_w[:, None].astype(ys.dtype)
    routed = jax.ops.segment_sum(ys, slot_tok, num_segments=n_tok + 1)[:n_tok]
    return (routed + swiglu(t, w_s_gate, w_s_up, w_s_down)).reshape(shape)


def mixing(h, P, l, rope, ctx_k, ctx_v, lru_h0):
    bsz, t_len, _ = h.shape
    q, k, v, u, gate, cv = jnp.split(h @ P['w_in'][l], IN_SPLITS, axis=-1)
    q = rms_norm(q.reshape(bsz, t_len, N_HEADS, HEAD_DIM), P['q_norm_g'][l])
    k = rms_norm(k.reshape(bsz, t_len, N_KV_HEADS, HEAD_DIM), P['k_norm_g'][l])
    v = v.reshape(bsz, t_len, N_KV_HEADS, HEAD_DIM)
    if rope is not None:
        q = apply_rope(q, *rope)
        k_lat = apply_rope(k, *rope)
        k_all = jnp.concatenate([k_lat, ctx_k.astype(k.dtype)], axis=1)
        v_all = jnp.concatenate([v, ctx_v.astype(v.dtype)], axis=1)
    else:
        k_all, v_all = k, v
    attn = block_attention(q, k_all, v_all)
    lru, lru_final = rglru_mixer(u, gate, P['lru_conv_w'][l], P['lru_conv_b'][l], P['lru_w_a'][l],
                                 P['lru_b_a'][l], P['lru_w_x'][l], P['lru_b_x'][l], P['lru_lambda'][l], lru_h0)
    conv = conv_module(cv, P['cm_dw_w'][l], P['cm_dw_b'][l], P['cm_ln_g'][l], P['cm_ln_b'][l])
    g = P['out_norm_g'][l]
    y = jnp.concatenate([rms_norm(attn, g[:ATTN_WIDTH]),
                         rms_norm(lru, g[ATTN_WIDTH:ATTN_WIDTH + LRU_WIDTH]),
                         rms_norm(conv, g[ATTN_WIDTH + LRU_WIDTH:])], axis=-1)
    return y @ P['w_out'][l], k, v, lru_final


def trunk_layer(x, cond, P, l, rope, ctx_k, ctx_v, lru_h0):
    mod = (jax.nn.silu(cond) @ P['w_mod'][l] + P['b_mod'][l])[:, None, :]
    sh1, sc1, g1, sh2, sc2, g2 = jnp.split(mod, 6, axis=-1)
    h = rms_norm(x, P['norm1_g'][l]) * (1 + sc1) + sh1
    y, k, v, lru_final = mixing(h, P, l, rope, ctx_k, ctx_v, lru_h0)
    x = x + g1 * y
    h = rms_norm(x, P['norm2_g'][l]) * (1 + sc2) + sh2
    x = x + g2 * moe_ffn(h, P['w_router'][l], P['b_router'][l], P['w_e_gate'][l], P['w_e_up'][l],
                         P['w_e_down'][l], P['w_s_gate'][l], P['w_s_up'][l], P['w_s_down'][l])
    return x, k, v, lru_final


def setup_inputs(seed: int = 0) -> dict:
    key = jax.random.key(seed)
    ks = iter(jax.random.split(key, 48))
    f32 = jnp.float32

    def nrm(shape, scale):
        return jax.random.normal(next(ks), shape, f32) * scale

    def gain(shape):
        return 1.0 + 0.05 * jax.random.normal(next(ks), shape, f32)

    a0 = jax.random.uniform(next(ks), (DEPTH, 2, LRU_WIDTH), f32, minval=0.9, maxval=0.999)
    s0 = a0 ** (1.0 / RG_C)
    lru_lambda = jnp.log(s0) - jnp.log1p(-s0)
    return {
        'x_prompt': nrm((BATCH, SEQ, D_MODEL), 1.0),
        'x_sample': nrm((DEC_BATCH, DEC_SEQ, D_MODEL), 1.0),
        'c': nrm((DEC_BATCH, D_MODEL), 1.0),
        'cache_k': nrm((DEC_BATCH, DEPTH, PAST_LEN, N_KV_HEADS, HEAD_DIM), 1.0),
        'cache_v': nrm((DEC_BATCH, DEPTH, PAST_LEN, N_KV_HEADS, HEAD_DIM), 1.0),
        'state_lru': nrm((DEC_BATCH, DEPTH, 2, LRU_WIDTH), 0.5),
        'c_ctx': nrm((D_MODEL,), 1.0),
        'w_mod': nrm((DEPTH, D_MODEL, 6 * D_MODEL), 0.5 * D_MODEL ** -0.5),
        'b_mod': nrm((DEPTH, 6 * D_MODEL), 0.02),
        'norm1_g': gain((DEPTH, D_MODEL)),
        'w_in': nrm((DEPTH, D_MODEL, IN_WIDTH), D_MODEL ** -0.5),
        'q_norm_g': gain((DEPTH, HEAD_DIM)),
        'k_norm_g': gain((DEPTH, HEAD_DIM)),
        'lru_conv_w': nrm((DEPTH, LRU_CONV, LRU_WIDTH), LRU_CONV ** -0.5),
        'lru_conv_b': nrm((DEPTH, LRU_WIDTH), 0.02),
        'lru_w_a': nrm((DEPTH, 2, LRU_HEADS, LRU_HEAD_DIM, LRU_HEAD_DIM), LRU_HEAD_DIM ** -0.5),
        'lru_b_a': nrm((DEPTH, 2, LRU_WIDTH), 0.02),
        'lru_w_x': nrm((DEPTH, 2, LRU_HEADS, LRU_HEAD_DIM, LRU_HEAD_DIM), LRU_HEAD_DIM ** -0.5),
        'lru_b_x': nrm((DEPTH, 2, LRU_WIDTH), 0.02),
        'lru_lambda': lru_lambda,
        'cm_dw_w': nrm((DEPTH, CM_KERNEL, CONV_WIDTH), CM_KERNEL ** -0.5),
        'cm_dw_b': nrm((DEPTH, CONV_WIDTH), 0.02),
        'cm_ln_g': gain((DEPTH, CONV_WIDTH)),
        'cm_ln_b': nrm((DEPTH, CONV_WIDTH), 0.02),
        'out_norm_g': gain((DEPTH, D_MIX)),
        'w_out': nrm((DEPTH, D_MIX, D_MODEL), D_MIX ** -0.5),
        'norm2_g': gain((DEPTH, D_MODEL)),
        'w_router': nrm((DEPTH, D_MODEL, N_EXPERTS), D_MODEL ** -0.5),
        'b_router': nrm((DEPTH, N_EXPERTS), 0.01),
        'w_e_gate': nrm((DEPTH, N_EXPERTS, D_MODEL, EXPERT_FF), D_MODEL ** -0.5),
        'w_e_up': nrm((DEPTH, N_EXPERTS, D_MODEL, EXPERT_FF), D_MODEL ** -0.5),
        'w_e_down': nrm((DEPTH, N_EXPERTS, EXPERT_FF, D_MODEL), EXPERT_FF ** -0.5),
        'w_s_gate': nrm((DEPTH, D_MODEL, SHARED_FF), D_MODEL ** -0.5),
        'w_s_up': nrm((DEPTH, D_MODEL, SHARED_FF), D_MODEL ** -0.5),
        'w_s_down': nrm((DEPTH, SHARED_FF, D_MODEL), SHARED_FF ** -0.5),
    }


def reference(x_prompt, x_sample, c, cache_k, cache_v, state_lru, c_ctx, w_mod, b_mod, norm1_g, w_in,
              q_norm_g, k_norm_g, lru_conv_w, lru_conv_b, lru_w_a, lru_b_a, lru_w_x, lru_b_x, lru_lambda,
              cm_dw_w, cm_dw_b, cm_ln_g, cm_ln_b, out_norm_g, w_out, norm2_g, w_router, b_router,
              w_e_gate, w_e_up, w_e_down, w_s_gate, w_s_up, w_s_down):
    P = {'w_mod': w_mod, 'b_mod': b_mod, 'norm1_g': norm1_g, 'w_in': w_in, 'q_norm_g': q_norm_g,
         'k_norm_g': k_norm_g, 'lru_conv_w': lru_conv_w, 'lru_conv_b': lru_conv_b, 'lru_w_a': lru_w_a,
         'lru_b_a': lru_b_a, 'lru_w_x': lru_w_x, 'lru_b_x': lru_b_x, 'lru_lambda': lru_lambda,
         'cm_dw_w': cm_dw_w, 'cm_dw_b': cm_dw_b, 'cm_ln_g': cm_ln_g, 'cm_ln_b': cm_ln_b,
         'out_norm_g': out_norm_g, 'w_out': w_out, 'norm2_g': norm2_g, 'w_router': w_router,
         'b_router': b_router, 'w_e_gate': w_e_gate, 'w_e_up': w_e_up, 'w_e_down': w_e_down,
         'w_s_gate': w_s_gate, 'w_s_up': w_s_up, 'w_s_down': w_s_down}
    xp = x_prompt
    h0 = jnp.zeros((x_prompt.shape[0], 2, LRU_WIDTH), jnp.float32)
    ks_, vs_, ss_ = [], [], []
    for l in range(DEPTH):
        xp, k, v, s = trunk_layer(xp, c_ctx[None, :], P, l, None, None, None, h0)
        ks_.append(k)
        vs_.append(v)
        ss_.append(s)
    new_cache_k = jnp.stack(ks_, axis=1)
    new_cache_v = jnp.stack(vs_, axis=1)
    new_state_lru = jnp.stack(ss_, axis=1)
    rope = axial_rope_tables(x_sample.shape[1])
    xs = x_sample
    for l in range(DEPTH):
        xs, _, _, _ = trunk_layer(xs, c, P, l, rope, cache_k[:, l], cache_v[:, l], state_lru[:, l])
    return (xp, xs, new_cache_k, new_cache_v, new_state_lru)
```

```python
import functools
import math

import jax
import jax.numpy as jnp
from jax import lax
from jax.experimental import pallas as pl
from jax.experimental.pallas import tpu as pltpu

F32 = jnp.float32
BF16 = jnp.bfloat16
I32 = jnp.int32

D_MODEL = 1024
DEPTH = 2
GRID_W = 64
ATTN_WIDTH = 512
LRU_WIDTH = 256
CONV_WIDTH = 256
HEAD_DIM = 64
N_HEADS = 8
N_KV_HEADS = 2
KV_GROUP = N_HEADS // N_KV_HEADS
KV_WIDTH = N_KV_HEADS * HEAD_DIM
ROPE_BASE = 10000.0
LRU_HEADS = 4
LRU_HEAD_DIM = LRU_WIDTH // LRU_HEADS
LRU_CONV = 4
RG_C = 8.0
CM_KERNEL = 31
N_EXPERTS = 256
TOP_K = 8
EXPERT_FF = 256
ROUTED_SCALE = 2.5
EPS = 1e-6
IN_WIDTH = ATTN_WIDTH + 2 * KV_WIDTH + 2 * LRU_WIDTH + 2 * CONV_WIDTH

SUBLANES = 8
LANES = 128
VMEM_LIMIT = 56 * 1024 * 1024

TOKEN_TILE = 256
SCAN_CHUNK = 256
CONV_CHUNK = 128
CONV_HALO = 16
LRU_HALO = 8
EXPERT_BLOCK = 128
COMBINE_TILE = 128
DMA_RING_TOKENS = 8


def _params(sem):
    return pltpu.CompilerParams(dimension_semantics=sem, vmem_limit_bytes=VMEM_LIMIT)


def _sigmoid(x):
    return 1.0 / (1.0 + jnp.exp(-x))


def _silu(x):
    return x * _sigmoid(x)


def _bdot(a, b):
    return jnp.dot(a.astype(BF16), b.astype(BF16), preferred_element_type=F32)


def _split_dot(a, b_bf16):
    hi = a.astype(BF16)
    lo = (a - hi.astype(F32)).astype(BF16)
    return (jnp.dot(hi, b_bf16, preferred_element_type=F32)
            + jnp.dot(lo, b_bf16, preferred_element_type=F32))


def _rms(x, g):
    return x * lax.rsqrt(jnp.mean(x * x, axis=-1, keepdims=True) + EPS) * g


MOD_COLS = 1536


def _mod_kernel(c_ref, w_ref, b_ref, o_ref):
    a = _silu(c_ref[...])
    o_ref[...] = jnp.dot(a, w_ref[...], preferred_element_type=F32,
                         precision=lax.Precision.HIGHEST) + b_ref[...]


def _mod_call(cond, w_mod, b_mod):
    n_cond = cond.shape[0]
    width = 6 * D_MODEL
    return pl.pallas_call(
        _mod_kernel,
        out_shape=jax.ShapeDtypeStruct((DEPTH, n_cond, width), F32),
        grid=(DEPTH, width // MOD_COLS),
        in_specs=[
            pl.BlockSpec((n_cond, D_MODEL), lambda l, j: (0, 0)),
            pl.BlockSpec((None, D_MODEL, MOD_COLS), lambda l, j: (l, 0, j)),
            pl.BlockSpec((None, 1, MOD_COLS), lambda l, j: (l, 0, j)),
        ],
        out_specs=pl.BlockSpec((None, n_cond, MOD_COLS), lambda l, j: (l, 0, j)),
        compiler_params=_params(("arbitrary", "arbitrary")),
        name="mod",
    )(cond, w_mod, b_mod.reshape(DEPTH, 1, width))


def _swap_pairs(x):
    n = x.shape[-1]
    lane = lax.broadcasted_iota(I32, x.shape, 1)
    nxt = pltpu.roll(x, n - 1, 1)
    prv = pltpu.roll(x, 1, 1)
    return jnp.where(lane % 2 == 0, nxt, prv)


def _premix_kernel(x_ref, mod_ref, n1g_ref, w_ref, qg_ref, kg_ref, bd_ref, cos_ref, sin_ref,
                   q_ref, kr_ref, vb_ref, kc_ref, vc_ref, u_ref, gate_ref, cv_ref):
    x = x_ref[...]
    sh1 = mod_ref[:, 0:D_MODEL]
    sc1 = mod_ref[:, D_MODEL:2 * D_MODEL]
    h = _rms(x, n1g_ref[...]) * (1.0 + sc1) + sh1
    z = jnp.dot(h.astype(BF16), w_ref[...], preferred_element_type=F32)
    o = 0
    q = z[:, o:o + ATTN_WIDTH]; o += ATTN_WIDTH
    k = z[:, o:o + KV_WIDTH]; o += KV_WIDTH
    v = z[:, o:o + KV_WIDTH]; o += KV_WIDTH
    u_ref[...] = z[:, o:o + LRU_WIDTH]; o += LRU_WIDTH
    gate_ref[...] = z[:, o:o + LRU_WIDTH]; o += LRU_WIDTH
    cv_ref[...] = z[:, o:o + 2 * CONV_WIDTH]

    bd = bd_ref[...]
    inv_hd = 1.0 / HEAD_DIM
    q_ms = _split_dot(q * q, bd) * inv_hd
    k_ms = _split_dot(k * k, bd[:KV_WIDTH, :KV_WIDTH]) * inv_hd
    qn = q * lax.rsqrt(q_ms + EPS) * qg_ref[...]
    kn = k * lax.rsqrt(k_ms + EPS) * kg_ref[...]
    kc_ref[...] = kn
    vc_ref[...] = v
    vb_ref[...] = v.astype(BF16)

    cos = cos_ref[...]
    sin = sin_ref[...]
    reps = ATTN_WIDTH // cos.shape[-1]
    cos_q = jnp.concatenate([cos] * reps, axis=1)
    sin_q = jnp.concatenate([sin] * reps, axis=1)
    qr = qn * cos_q + _swap_pairs(qn) * sin_q
    kr = kn * cos + _swap_pairs(kn) * sin
    q_ref[...] = (qr * (HEAD_DIM ** -0.5)).astype(BF16)
    kr_ref[...] = kr.astype(BF16)


def _premix_call(x, mod3, l, P, rope_cos, rope_sin, n_ctx, lat_t):
    n = x.shape[0]
    tm = TOKEN_TILE
    n_ctx_tiles = n_ctx // tm
    tiles_per_seq = lat_t // tm

    def mod_idx(i):
        return (jnp.where(i < n_ctx_tiles, 0, 1 + (i - n_ctx_tiles) // tiles_per_seq), 0, 0)

    def rope_idx(i):
        return (jnp.where(i < n_ctx_tiles, tiles_per_seq, (i - n_ctx_tiles) % tiles_per_seq), 0)

    const = lambda i: (0, 0)
    row = lambda i: (i, 0)
    outs = pl.pallas_call(
        _premix_kernel,
        out_shape=(
            jax.ShapeDtypeStruct((n, ATTN_WIDTH), BF16),
            jax.ShapeDtypeStruct((n, KV_WIDTH), BF16),
            jax.ShapeDtypeStruct((n, KV_WIDTH), BF16),
            jax.ShapeDtypeStruct((n, KV_WIDTH), F32),
            jax.ShapeDtypeStruct((n, KV_WIDTH), F32),
            jax.ShapeDtypeStruct((n, LRU_WIDTH), F32),
            jax.ShapeDtypeStruct((n, LRU_WIDTH), F32),
            jax.ShapeDtypeStruct((n, 2 * CONV_WIDTH), F32),
        ),
        grid=(n // tm,),
        in_specs=[
            pl.BlockSpec((tm, D_MODEL), row),
            pl.BlockSpec((None, 1, 6 * D_MODEL), mod_idx),
            pl.BlockSpec((1, D_MODEL), const),
            pl.BlockSpec((D_MODEL, IN_WIDTH), const),
            pl.BlockSpec((1, ATTN_WIDTH), const),
            pl.BlockSpec((1, KV_WIDTH), const),
            pl.BlockSpec((ATTN_WIDTH, ATTN_WIDTH), const),
            pl.BlockSpec((tm, LANES), rope_idx),
            pl.BlockSpec((tm, LANES), rope_idx),
        ],
        out_specs=(
            pl.BlockSpec((tm, ATTN_WIDTH), row),
            pl.BlockSpec((tm, KV_WIDTH), row),
            pl.BlockSpec((tm, KV_WIDTH), row),
            pl.BlockSpec((tm, KV_WIDTH), row),
            pl.BlockSpec((tm, KV_WIDTH), row),
            pl.BlockSpec((tm, LRU_WIDTH), row),
            pl.BlockSpec((tm, LRU_WIDTH), row),
            pl.BlockSpec((tm, 2 * CONV_WIDTH), row),
        ),
        compiler_params=_params(("parallel",)),
        name="premix",
    )(x, mod3, P["norm1_g"][l][None, :], P["w_in_bf16"][l], P["q_norm_g_t"][l][None, :],
      P["k_norm_g_t"][l][None, :], P["head_blockdiag"], rope_cos, rope_sin)
    return outs


def _attn_kernel(q_ref, k_ref, v_ref, o_ref):
    outs = []
    for kh in range(N_KV_HEADS):
        k = k_ref[:, kh * HEAD_DIM:(kh + 1) * HEAD_DIM]
        v = v_ref[:, kh * HEAD_DIM:(kh + 1) * HEAD_DIM]
        for g in range(KV_GROUP):
            hd = kh * KV_GROUP + g
            q = q_ref[:, hd * HEAD_DIM:(hd + 1) * HEAD_DIM]
            s = lax.dot_general(q, k, (((1,), (1,)), ((), ())), preferred_element_type=F32)
            m = jnp.max(s, axis=-1, keepdims=True)
            p = jnp.exp(s - m)
            denom = jnp.sum(p, axis=-1, keepdims=True)
            o = jnp.dot(p.astype(BF16), v, preferred_element_type=F32)
            outs.append(o / denom)
    o_ref[...] = jnp.concatenate(outs, axis=1)


def _attn_call(q3, k3, v3, tq):
    b, s, _ = q3.shape
    t = k3.shape[1]
    return pl.pallas_call(
        _attn_kernel,
        out_shape=jax.ShapeDtypeStruct((b, s, ATTN_WIDTH), F32),
        grid=(b, s // tq),
        in_specs=[
            pl.BlockSpec((None, tq, ATTN_WIDTH), lambda i, j: (i, j, 0)),
            pl.BlockSpec((None, t, KV_WIDTH), lambda i, j: (i, 0, 0)),
            pl.BlockSpec((None, t, KV_WIDTH), lambda i, j: (i, 0, 0)),
        ],
        out_specs=pl.BlockSpec((None, tq, ATTN_WIDTH), lambda i, j: (i, j, 0)),
        compiler_params=_params(("parallel", "parallel")),
        name="attention",
    )(q3, k3, v3)


def _chunk_scan(a, b, reverse):
    n = a.shape[0]
    row = lax.broadcasted_iota(I32, a.shape, 0)
    d = 1
    while d < n:
        if reverse:
            a_s = pltpu.roll(a, n - d, 0)
            b_s = pltpu.roll(b, n - d, 0)
            ok = row < n - d
        else:
            a_s = pltpu.roll(a, d, 0)
            b_s = pltpu.roll(b, d, 0)
            ok = row >= d
        b = jnp.where(ok, a * b_s + b, b)
        a = jnp.where(ok, a * a_s, a)
        d *= 2
    return a, b


def _gelu_tanh(x):
    return 0.5 * x * (1.0 + jnp.tanh(math.sqrt(2.0 / math.pi) * (x + 0.044715 * (x * x * x))))


def _lru_kernel(u_ref, gate_ref, cw_ref, cb_ref, wg_ref, bg_ref, lam_ref, h0_ref,
                y_ref, fin_ref, upad, xc_scr, fwd_scr):
    t_len = u_ref.shape[0]
    ch = SCAN_CHUNK
    n_chunks = t_len // ch
    w = LRU_WIDTH
    zeros_halo = jnp.zeros((LRU_HALO, w), F32)
    upad[0:LRU_HALO, :] = zeros_halo
    upad[t_len + LRU_HALO:t_len + 2 * LRU_HALO, :] = zeros_halo

    def fill(c, carry):
        r0 = pl.multiple_of(c * ch, ch)
        upad[pl.ds(r0 + LRU_HALO, ch), :] = u_ref[pl.ds(r0, ch), :]
        return carry

    lax.fori_loop(0, n_chunks, fill, 0)

    lam = lam_ref[...]
    nlam = -lam
    softplus = jnp.maximum(nlam, 0.0) + jnp.log(1.0 + jnp.exp(-jnp.abs(nlam)))
    decay = -RG_C * softplus
    cw = cw_ref[...]
    cb = cb_ref[...]

    def gates(xc, d):
        z = jnp.dot(xc.astype(BF16), wg_ref[:, 2 * d * w:2 * (d + 1) * w],
                    preferred_element_type=F32) + bg_ref[:, 2 * d * w:2 * (d + 1) * w]
        r = _sigmoid(z[:, :w])
        i = _sigmoid(z[:, w:])
        log_a = decay[d:d + 1, :] * r
        a = jnp.exp(log_a)
        b = jnp.sqrt(-jnp.tanh(log_a) * (a * a + 1.0)) * (i * xc)
        return a, b

    def fwd(c, h):
        r0 = pl.multiple_of(c * ch, ch)
        win = upad[pl.ds(r0, ch + 2 * LRU_HALO), :]
        xc = cb
        for j in range(LRU_CONV):
            s0 = LRU_HALO - (LRU_CONV - 1) // 2 + j
            xc = xc + cw[j:j + 1, :] * win[s0:s0 + ch, :]
        xc_scr[pl.ds(r0, ch), :] = xc
        a, b = gates(xc, 0)
        a_cum, b_cum = _chunk_scan(a, b, reverse=False)
        hs = a_cum * h + b_cum
        fwd_scr[pl.ds(r0, ch), :] = hs
        return hs[ch - 1:ch, :]

    h_f = lax.fori_loop(0, n_chunks, fwd, h0_ref[0:1, :])

    def bwd(ci, h):
        c = n_chunks - 1 - ci
        r0 = pl.multiple_of(c * ch, ch)
        xc = xc_scr[pl.ds(r0, ch), :]
        a, b = gates(xc, 1)
        a_cum, b_cum = _chunk_scan(a, b, reverse=True)
        hs = a_cum * h + b_cum
        y_ref[pl.ds(r0, ch), :] = (fwd_scr[pl.ds(r0, ch), :] + hs) * _gelu_tanh(gate_ref[pl.ds(r0, ch), :])
        return hs[0:1, :]

    h_b = lax.fori_loop(0, n_chunks, bwd, h0_ref[1:2, :])
    fin_ref[0:1, :] = h_f
    fin_ref[1:2, :] = h_b


def _lru_call(u3, gate3, h0, l, P):
    b, t, w = u3.shape
    const = lambda i: (0, 0)
    seq = lambda i: (i, 0, 0)
    return pl.pallas_call(
        _lru_kernel,
        out_shape=(jax.ShapeDtypeStruct((b, t, w), F32),
                   jax.ShapeDtypeStruct((b, 2, w), F32)),
        grid=(b,),
        in_specs=[
            pl.BlockSpec((None, t, w), seq),
            pl.BlockSpec((None, t, w), seq),
            pl.BlockSpec((LRU_CONV, w), const),
            pl.BlockSpec((1, w), const),
            pl.BlockSpec((w, 4 * w), const),
            pl.BlockSpec((1, 4 * w), const),
            pl.BlockSpec((2, w), const),
            pl.BlockSpec((None, 2, w), seq),
        ],
        out_specs=(pl.BlockSpec((None, t, w), seq),
                   pl.BlockSpec((None, 2, w), seq)),
        scratch_shapes=[pltpu.VMEM((t + 2 * LRU_HALO, w), F32),
                        pltpu.VMEM((t, w), F32),
                        pltpu.VMEM((t, w), F32)],
        compiler_params=_params(("parallel",)),
        name="rglru",
    )(u3, gate3, P["lru_conv_w"][l], P["lru_conv_b"][l][None, :], P["lru_gate_w"][l],
      P["lru_gate_b"][l][None, :], P["lru_lambda"][l], h0)


def _convmod_kernel(cv_ref, w_ref, b_ref, g_ref, beta_ref, o_ref, hpad):
    t_len = cv_ref.shape[0]
    ch = CONV_CHUNK
    n_chunks = t_len // ch
    w = CONV_WIDTH
    zeros_halo = jnp.zeros((CONV_HALO, w), F32)
    hpad[0:CONV_HALO, :] = zeros_halo
    hpad[t_len + CONV_HALO:t_len + 2 * CONV_HALO, :] = zeros_halo

    def glu(c, carry):
        r0 = pl.multiple_of(c * ch, ch)
        blk = cv_ref[pl.ds(r0, ch), :]
        hpad[pl.ds(r0 + CONV_HALO, ch), :] = blk[:, :w] * _sigmoid(blk[:, w:])
        return carry

    lax.fori_loop(0, n_chunks, glu, 0)

    taps = w_ref[...]
    bias = b_ref[...]
    gamma = g_ref[...]
    beta = beta_ref[...]

    def conv(c, carry):
        r0 = pl.multiple_of(c * ch, ch)
        win = hpad[pl.ds(r0, ch + 2 * CONV_HALO), :]
        acc = bias
        for j in range(CM_KERNEL):
            s0 = CONV_HALO - CM_KERNEL // 2 + j
            acc = acc + taps[j:j + 1, :] * win[s0:s0 + ch, :]
        mu = jnp.mean(acc, axis=-1, keepdims=True)
        cen = acc - mu
        var = jnp.mean(cen * cen, axis=-1, keepdims=True)
        y = cen * lax.rsqrt(var + EPS) * gamma + beta
        o_ref[pl.ds(r0, ch), :] = _silu(y)
        return carry

    lax.fori_loop(0, n_chunks, conv, 0)


def _convmod_call(cv3, l, P):
    b, t, _ = cv3.shape
    w = CONV_WIDTH
    const = lambda i: (0, 0)
    seq = lambda i: (i, 0, 0)
    return pl.pallas_call(
        _convmod_kernel,
        out_shape=jax.ShapeDtypeStruct((b, t, w), F32),
        grid=(b,),
        in_specs=[
            pl.BlockSpec((None, t, 2 * w), seq),
            pl.BlockSpec((CM_KERNEL, w), const),
            pl.BlockSpec((1, w), const),
            pl.BlockSpec((1, w), const),
            pl.BlockSpec((1, w), const),
        ],
        out_specs=pl.BlockSpec((None, t, w), seq),
        scratch_shapes=[pltpu.VMEM((t + 2 * CONV_HALO, w), F32)],
        compiler_params=_params(("parallel",)),
        name="convmod",
    )(cv3, P["cm_dw_w"][l], P["cm_dw_b"][l][None, :], P["cm_ln_g"][l][None, :], P["cm_ln_b"][l][None, :])


def _postmix_kernel(attn_ref, lru_ref, conv_ref, x_ref, mod_ref, og_ref, wo_ref, n2g_ref, wr_hi_ref, wr_lo_ref,
                    x1_ref, h2_ref, lg_ref):
    og = og_ref[...]
    a0, a1, a2 = ATTN_WIDTH, ATTN_WIDTH + LRU_WIDTH, D_MODEL
    y = jnp.concatenate([_rms(attn_ref[...], og[:, :a0]),
                         _rms(lru_ref[...], og[:, a0:a1]),
                         _rms(conv_ref[...], og[:, a1:a2])], axis=1)
    y = jnp.dot(y.astype(BF16), wo_ref[...], preferred_element_type=F32)
    g1 = mod_ref[:, 2 * D_MODEL:3 * D_MODEL]
    sh2 = mod_ref[:, 3 * D_MODEL:4 * D_MODEL]
    sc2 = mod_ref[:, 4 * D_MODEL:5 * D_MODEL]
    x1 = x_ref[...] + g1 * y
    x1_ref[...] = x1
    h2 = _rms(x1, n2g_ref[...]) * (1.0 + sc2) + sh2
    h2_ref[...] = h2
    hi = h2.astype(BF16)
    lo = (h2 - hi.astype(F32)).astype(BF16)
    w_hi = wr_hi_ref[...]
    lg_ref[...] = (jnp.dot(hi, w_hi, preferred_element_type=F32)
                   + jnp.dot(lo, w_hi, preferred_element_type=F32)
                   + jnp.dot(hi, wr_lo_ref[...], preferred_element_type=F32))


def _postmix_call(attn, lru, conv, x, mod3, l, P, n_ctx, lat_t):
    n = x.shape[0]
    tm = TOKEN_TILE
    n_ctx_tiles = n_ctx // tm
    tiles_per_seq = lat_t // tm

    def mod_idx(i):
        return (jnp.where(i < n_ctx_tiles, 0, 1 + (i - n_ctx_tiles) // tiles_per_seq), 0, 0)

    const = lambda i: (0, 0)
    row = lambda i: (i, 0)
    return pl.pallas_call(
        _postmix_kernel,
        out_shape=(jax.ShapeDtypeStruct((n, D_MODEL), F32),
                   jax.ShapeDtypeStruct((n, D_MODEL), F32),
                   jax.ShapeDtypeStruct((n, N_EXPERTS), F32)),
        grid=(n // tm,),
        in_specs=[
            pl.BlockSpec((tm, ATTN_WIDTH), row),
            pl.BlockSpec((tm, LRU_WIDTH), row),
            pl.BlockSpec((tm, CONV_WIDTH), row),
            pl.BlockSpec((tm, D_MODEL), row),
            pl.BlockSpec((None, 1, 6 * D_MODEL), mod_idx),
            pl.BlockSpec((1, D_MODEL), const),
            pl.BlockSpec((D_MODEL, D_MODEL), const),
            pl.BlockSpec((1, D_MODEL), const),
            pl.BlockSpec((D_MODEL, N_EXPERTS), const),
            pl.BlockSpec((D_MODEL, N_EXPERTS), const),
        ],
        out_specs=(pl.BlockSpec((tm, D_MODEL), row),
                   pl.BlockSpec((tm, D_MODEL), row),
                   pl.BlockSpec((tm, N_EXPERTS), row)),
        compiler_params=_params(("parallel",)),
        name="postmix",
    )(attn, lru, conv, x, mod3, P["out_norm_g"][l][None, :], P["w_out_bf16"][l], P["norm2_g"][l][None, :],
      P["w_router_hi"][l], P["w_router_lo"][l])


def _route_kernel(lg_ref, br_ref, tri_ref, ids_ref, gw_ref, rank_ref, cnt_ref, carry):
    i = pl.program_id(0)

    @pl.when(i == 0)
    def _():
        carry[...] = jnp.zeros_like(carry)

    scores = _sigmoid(lg_ref[...])
    sel = scores + br_ref[...]
    tm = scores.shape[0]
    lane = lax.broadcasted_iota(I32, (tm, N_EXPERTS), 1).astype(F32)
    slot_lane = lax.broadcasted_iota(I32, (tm, LANES), 1)
    ids_acc = jnp.zeros((tm, LANES), F32)
    gw_acc = jnp.zeros((tm, LANES), F32)
    hot = jnp.zeros((tm, N_EXPERTS), F32)
    picked = []
    for k in range(TOP_K):
        m = jnp.max(sel, axis=-1, keepdims=True)
        idx = jnp.min(jnp.where(sel == m, lane, float(N_EXPERTS)), axis=-1, keepdims=True)
        one = lane == idx
        g = jnp.sum(jnp.where(one, scores, 0.0), axis=-1, keepdims=True)
        sel = jnp.where(one, -jnp.inf, sel)
        hot = jnp.where(one, 1.0, hot)
        ids_acc = jnp.where(slot_lane == k, idx, ids_acc)
        gw_acc = jnp.where(slot_lane == k, g, gw_acc)
        picked.append(idx)
    denom = jnp.sum(gw_acc, axis=-1, keepdims=True)
    gw_acc = ROUTED_SCALE * gw_acc / denom

    before = jnp.dot(tri_ref[...], hot.astype(BF16), preferred_element_type=F32) + carry[0:1, :]
    rank_acc = jnp.zeros((tm, LANES), F32)
    for k in range(TOP_K):
        r = jnp.sum(jnp.where(lane == picked[k], before, 0.0), axis=-1, keepdims=True)
        rank_acc = jnp.where(slot_lane == k, r, rank_acc)
    carry[0:1, :] = carry[0:1, :] + jnp.sum(hot, axis=0, keepdims=True)
    cnt_ref[...] = carry[...]
    ids_ref[...] = ids_acc[:, :TOP_K].astype(I32)
    gw_ref[...] = gw_acc[:, :TOP_K]
    rank_ref[...] = rank_acc[:, :TOP_K].astype(I32)


def _route_call(logits, b_router_l, tri):
    n = logits.shape[0]
    tm = TOKEN_TILE
    const = lambda i: (0, 0)
    row = lambda i: (i, 0)
    return pl.pallas_call(
        _route_kernel,
        out_shape=(jax.ShapeDtypeStruct((n, TOP_K), I32),
                   jax.ShapeDtypeStruct((n, TOP_K), F32),
                   jax.ShapeDtypeStruct((n, TOP_K), I32),
                   jax.ShapeDtypeStruct((SUBLANES, N_EXPERTS), F32)),
        grid=(n // tm,),
        in_specs=[
            pl.BlockSpec((tm, N_EXPERTS), row),
            pl.BlockSpec((1, N_EXPERTS), const),
            pl.BlockSpec((tm, tm), const),
        ],
        out_specs=(pl.BlockSpec((tm, TOP_K), row),
                   pl.BlockSpec((tm, TOP_K), row),
                   pl.BlockSpec((tm, TOP_K), row),
                   pl.BlockSpec((SUBLANES, N_EXPERTS), const)),
        scratch_shapes=[pltpu.VMEM((SUBLANES, N_EXPERTS), F32)],
        compiler_params=_params(("arbitrary",)),
        name="route",
    )(logits, b_router_l[None, :], tri)


def _row_copy_ring(n_tok, make_copy):
    depth = DMA_RING_TOKENS

    def start_only(t, c):
        for k in range(TOP_K):
            make_copy(t, k).start()
        return c

    def wait_then_start(t, c):
        for k in range(TOP_K):
            make_copy(t - depth, k).wait()
            make_copy(t, k).start()
        return c

    def wait_only(t, c):
        for k in range(TOP_K):
            make_copy(t, k).wait()
        return c

    lax.fori_loop(0, depth, start_only, 0)
    lax.fori_loop(depth, n_tok, wait_then_start, 0)
    lax.fori_loop(n_tok - depth, n_tok, wait_only, 0)


def _dispatch_kernel(slots_ref, h_ref, xs_in_ref, xs_ref, sems):
    del xs_in_ref
    n_tok = h_ref.shape[0]

    def make_copy(t, k):
        slot = slots_ref[t * TOP_K + k]
        return pltpu.make_async_copy(h_ref.at[pl.ds(t, 1), :], xs_ref.at[pl.ds(slot, 1), :],
                                     sems.at[(t % DMA_RING_TOKENS) * TOP_K + k])

    _row_copy_ring(n_tok, make_copy)


def _dispatch_call(slots_flat, h2, xs_zero):
    n = h2.shape[0]
    tm = TOKEN_TILE
    return pl.pallas_call(
        _dispatch_kernel,
        out_shape=jax.ShapeDtypeStruct(xs_zero.shape, xs_zero.dtype),
        grid=(n // tm,),
        in_specs=[
            pl.BlockSpec((tm * TOP_K,), lambda i: (i,), memory_space=pltpu.SMEM),
            pl.BlockSpec((tm, D_MODEL), lambda i: (i, 0)),
            pl.BlockSpec(memory_space=pl.ANY),
        ],
        out_specs=pl.BlockSpec(memory_space=pl.ANY),
        scratch_shapes=[pltpu.SemaphoreType.DMA((DMA_RING_TOKENS * TOP_K,))],
        input_output_aliases={2: 0},
        compiler_params=_params(("arbitrary",)),
        name="dispatch",
    )(slots_flat, h2, xs_zero)


def _expert_kernel(blk_e_ref, nused_ref, xs_ref, wg_ref, wu_ref, wd_ref, ys_ref):
    del blk_e_ref
    b = pl.program_id(0)

    @pl.when(b < nused_ref[0])
    def _():
        x = xs_ref[...].astype(BF16)
        g = jnp.dot(x, wg_ref[...].astype(BF16), preferred_element_type=F32)
        u = jnp.dot(x, wu_ref[...].astype(BF16), preferred_element_type=F32)
        h = (_silu(g) * u).astype(BF16)
        ys_ref[...] = jnp.dot(h, wd_ref[...].astype(BF16), preferred_element_type=F32)

    @pl.when(b >= nused_ref[0])
    def _():
        ys_ref[...] = jnp.zeros_like(ys_ref)


def _expert_call(blk_e, nused, xs, w_e_gate, w_e_up, w_e_down, l):
    n_slots = xs.shape[0]
    bm = EXPERT_BLOCK
    n_blocks = n_slots // bm

    def x_idx(b, blk_e_ref, nused_ref):
        return (jnp.minimum(b, nused_ref[0] - 1), 0)

    def w_idx(b, blk_e_ref, nused_ref):
        return (l, blk_e_ref[b], 0, 0)

    grid_spec = pltpu.PrefetchScalarGridSpec(
        num_scalar_prefetch=2,
        grid=(n_blocks,),
        in_specs=[
            pl.BlockSpec((bm, D_MODEL), x_idx),
            pl.BlockSpec((None, None, D_MODEL, EXPERT_FF), w_idx),
            pl.BlockSpec((None, None, D_MODEL, EXPERT_FF), w_idx),
            pl.BlockSpec((None, None, EXPERT_FF, D_MODEL), w_idx),
        ],
        out_specs=pl.BlockSpec((bm, D_MODEL), lambda b, blk_e_ref, nused_ref: (b, 0)),
    )
    return pl.pallas_call(
        _expert_kernel,
        out_shape=jax.ShapeDtypeStruct((n_slots, D_MODEL), F32),
        grid_spec=grid_spec,
        compiler_params=_params(("arbitrary",)),
        name="experts",
    )(blk_e, nused, xs, w_e_gate, w_e_up, w_e_down)


def _combine_kernel(slots_ref, gw_ref, x1_ref, h2_ref, mod_ref, wsg_ref, wsu_ref, wsd_ref, ys_ref,
                    o_ref, buf, sems):
    n_tok = x1_ref.shape[0]

    def make_copy(t, k):
        slot = slots_ref[t * TOP_K + k]
        return pltpu.make_async_copy(ys_ref.at[pl.ds(slot, 1), :], buf.at[k, pl.ds(t, 1), :],
                                     sems.at[(t % DMA_RING_TOKENS) * TOP_K + k])

    _row_copy_ring(n_tok, make_copy)

    h = h2_ref[...].astype(BF16)
    hid = _silu(jnp.dot(h, wsg_ref[...], preferred_element_type=F32)) * jnp.dot(
        h, wsu_ref[...], preferred_element_type=F32)
    acc = jnp.dot(hid.astype(BF16), wsd_ref[...], preferred_element_type=F32)
    gw = gw_ref[...]
    for k in range(TOP_K):
        acc = acc + gw[:, k:k + 1] * buf[k]
    g2 = mod_ref[:, 5 * D_MODEL:6 * D_MODEL]
    o_ref[...] = x1_ref[...] + g2 * acc


def _combine_call(slots_flat, gw, x1, h2, mod3, ys, l, P, n_ctx, lat_t):
    n = x1.shape[0]
    tm = COMBINE_TILE
    n_ctx_tiles = n_ctx // tm
    tiles_per_seq = lat_t // tm

    def mod_idx(i):
        return (jnp.where(i < n_ctx_tiles, 0, 1 + (i - n_ctx_tiles) // tiles_per_seq), 0, 0)

    const = lambda i: (0, 0)
    row = lambda i: (i, 0)
    return pl.pallas_call(
        _combine_kernel,
        out_shape=jax.ShapeDtypeStruct((n, D_MODEL), F32),
        grid=(n // tm,),
        in_specs=[
            pl.BlockSpec((tm * TOP_K,), lambda i: (i,), memory_space=pltpu.SMEM),
            pl.BlockSpec((tm, TOP_K), row),
            pl.BlockSpec((tm, D_MODEL), row),
            pl.BlockSpec((tm, D_MODEL), row),
            pl.BlockSpec((None, 1, 6 * D_MODEL), mod_idx),
            pl.BlockSpec((D_MODEL, EXPERT_FF), const),
            pl.BlockSpec((D_MODEL, EXPERT_FF), const),
            pl.BlockSpec((EXPERT_FF, D_MODEL), const),
            pl.BlockSpec(memory_space=pl.ANY),
        ],
        out_specs=pl.BlockSpec((tm, D_MODEL), row),
        scratch_shapes=[pltpu.VMEM((TOP_K, tm, D_MODEL), F32),
                        pltpu.SemaphoreType.DMA((DMA_RING_TOKENS * TOP_K,))],
        compiler_params=_params(("arbitrary",)),
        name="combine",
    )(slots_flat, gw, x1, h2, mod3, P["w_s_gate_bf16"][l], P["w_s_up_bf16"][l], P["w_s_down_bf16"][l], ys)


def _rope_tables(lat_t):
    rows = lat_t // GRID_W
    row = jnp.repeat(jnp.arange(rows, dtype=F32), GRID_W)
    col = jnp.tile(jnp.arange(GRID_W, dtype=F32), rows)
    n_freq = HEAD_DIM // 4
    inv = jnp.power(ROPE_BASE, -jnp.arange(n_freq, dtype=F32) / n_freq)
    ang = jnp.concatenate([row[:, None] * inv, col[:, None] * inv], axis=-1)
    cos = jnp.repeat(jnp.cos(ang), 2, axis=-1)
    sign = jnp.tile(jnp.array([-1.0, 1.0], F32), HEAD_DIM // 2)
    sin = jnp.repeat(jnp.sin(ang), 2, axis=-1) * sign
    cos = jnp.concatenate([cos, jnp.ones((TOKEN_TILE, HEAD_DIM), F32)], axis=0)
    sin = jnp.concatenate([sin, jnp.zeros((TOKEN_TILE, HEAD_DIM), F32)], axis=0)
    reps = LANES // HEAD_DIM
    return jnp.tile(cos, (1, reps)), jnp.tile(sin, (1, reps))


def _block_diag_gates(w_a, w_x):
    def dense(w):
        eye = jnp.eye(LRU_HEADS, dtype=w.dtype)
        return jnp.einsum("ncd,nm->ncmd", w, eye).reshape(LRU_WIDTH, LRU_WIDTH)
    return jnp.concatenate([dense(w_a[0]), dense(w_x[0]), dense(w_a[1]), dense(w_x[1])], axis=1)


def _prepare(P):
    Q = dict(P)
    Q["w_in_bf16"] = P["w_in"].astype(BF16)
    Q["w_out_bf16"] = P["w_out"].astype(BF16)
    Q["q_norm_g_t"] = jnp.tile(P["q_norm_g"], (1, N_HEADS))
    Q["k_norm_g_t"] = jnp.tile(P["k_norm_g"], (1, N_KV_HEADS))
    head = jnp.arange(ATTN_WIDTH) // HEAD_DIM
    Q["head_blockdiag"] = (head[:, None] == head[None, :]).astype(BF16)
    Q["lru_gate_w"] = jnp.stack([_block_diag_gates(P["lru_w_a"][l], P["lru_w_x"][l])
                                 for l in range(DEPTH)]).astype(BF16)
    Q["lru_gate_b"] = jnp.concatenate([P["lru_b_a"][:, 0], P["lru_b_x"][:, 0],
                                       P["lru_b_a"][:, 1], P["lru_b_x"][:, 1]], axis=-1)
    w_hi = P["w_router"].astype(BF16)
    Q["w_router_hi"] = w_hi
    Q["w_router_lo"] = (P["w_router"] - w_hi.astype(F32)).astype(BF16)
    Q["w_s_gate_bf16"] = P["w_s_gate"].astype(BF16)
    Q["w_s_up_bf16"] = P["w_s_up"].astype(BF16)
    Q["w_s_down_bf16"] = P["w_s_down"].astype(BF16)
    return Q


def _slot_tables(ids, rank, counts, n_blocks):
    bm = EXPERT_BLOCK
    counts = counts.astype(I32)
    padded = (counts + bm - 1) // bm * bm
    pad_end = jnp.cumsum(padded)
    pad_start = pad_end - padded
    slots = (pad_start[ids] + rank).reshape(-1)
    nused = pad_end[-1] // bm
    blk = jnp.arange(n_blocks, dtype=I32)
    blk_e = jnp.searchsorted(pad_end, jnp.minimum(blk, nused - 1) * bm, side="right").astype(I32)
    blk_e = jnp.minimum(blk_e, N_EXPERTS - 1)
    return slots.astype(I32), blk_e, nused.reshape(1).astype(I32)


def _trunk(x_prompt, x_sample, c, cache_k, cache_v, state_lru, c_ctx, P):
    n_seq_c, ctx_t, _ = x_prompt.shape
    n_seq_l, lat_t, _ = x_sample.shape
    past = cache_k.shape[2]
    n_ctx = n_seq_c * ctx_t
    n_lat = n_seq_l * lat_t
    n = n_ctx + n_lat
    P = _prepare(P)

    n_cond = -(-(1 + n_seq_l) // SUBLANES) * SUBLANES
    cond = jnp.concatenate([c_ctx[None, :], c, jnp.zeros((n_cond - 1 - n_seq_l, D_MODEL), F32)], axis=0)
    mods = _mod_call(cond, P["w_mod"], P["b_mod"])
    rope_cos, rope_sin = _rope_tables(lat_t)
    tri = (jnp.arange(TOKEN_TILE)[:, None] > jnp.arange(TOKEN_TILE)[None, :]).astype(BF16)
    n_blocks = n * TOP_K // EXPERT_BLOCK + N_EXPERTS
    n_slots = n_blocks * EXPERT_BLOCK

    x = jnp.concatenate([x_prompt.reshape(n_ctx, D_MODEL), x_sample.reshape(n_lat, D_MODEL)], axis=0)
    ks, vs, ss = [], [], []
    for l in range(DEPTH):
        mod3 = mods[l].reshape(n_cond, 1, 6 * D_MODEL)
        q, kr, vb, kc, vc, u, gate, cv = _premix_call(x, mod3, l, P, rope_cos, rope_sin, n_ctx, lat_t)
        ks.append(kc[:n_ctx].reshape(n_seq_c, ctx_t, N_KV_HEADS, HEAD_DIM))
        vs.append(vc[:n_ctx].reshape(n_seq_c, ctx_t, N_KV_HEADS, HEAD_DIM))

        attn_c = _attn_call(q[:n_ctx].reshape(n_seq_c, ctx_t, ATTN_WIDTH),
                            kr[:n_ctx].reshape(n_seq_c, ctx_t, KV_WIDTH),
                            vb[:n_ctx].reshape(n_seq_c, ctx_t, KV_WIDTH), tq=min(ctx_t, 256))
        k_all = jnp.concatenate([kr[n_ctx:].reshape(n_seq_l, lat_t, KV_WIDTH),
                                 cache_k[:, l].reshape(n_seq_l, past, KV_WIDTH).astype(BF16)], axis=1)
        v_all = jnp.concatenate([vb[n_ctx:].reshape(n_seq_l, lat_t, KV_WIDTH),
                                 cache_v[:, l].reshape(n_seq_l, past, KV_WIDTH).astype(BF16)], axis=1)
        attn_l = _attn_call(q[n_ctx:].reshape(n_seq_l, lat_t, ATTN_WIDTH), k_all, v_all, tq=128)

        lru_c, fin_c = _lru_call(u[:n_ctx].reshape(n_seq_c, ctx_t, LRU_WIDTH),
                                 gate[:n_ctx].reshape(n_seq_c, ctx_t, LRU_WIDTH),
                                 jnp.zeros((n_seq_c, 2, LRU_WIDTH), F32), l, P)
        lru_l, _ = _lru_call(u[n_ctx:].reshape(n_seq_l, lat_t, LRU_WIDTH),
                             gate[n_ctx:].reshape(n_seq_l, lat_t, LRU_WIDTH),
                             state_lru[:, l], l, P)
        ss.append(fin_c)
        conv_c = _convmod_call(cv[:n_ctx].reshape(n_seq_c, ctx_t, 2 * CONV_WIDTH), l, P)
        conv_l = _convmod_call(cv[n_ctx:].reshape(n_seq_l, lat_t, 2 * CONV_WIDTH), l, P)

        attn = jnp.concatenate([attn_c.reshape(n_ctx, ATTN_WIDTH), attn_l.reshape(n_lat, ATTN_WIDTH)], axis=0)
        lru = jnp.concatenate([lru_c.reshape(n_ctx, LRU_WIDTH), lru_l.reshape(n_lat, LRU_WIDTH)], axis=0)
        conv = jnp.concatenate([conv_c.reshape(n_ctx, CONV_WIDTH), conv_l.reshape(n_lat, CONV_WIDTH)], axis=0)
        x1, h2, logits = _postmix_call(attn, lru, conv, x, mod3, l, P, n_ctx, lat_t)

        ids, gw, rank, counts = _route_call(logits, P["b_router"][l], tri)
        slots, blk_e, nused = _slot_tables(ids, rank, counts[0], n_blocks)
        xs = _dispatch_call(slots, h2, jnp.zeros((n_slots, D_MODEL), F32))
        ys = _expert_call(blk_e, nused, xs, P["w_e_gate"], P["w_e_up"], P["w_e_down"], l)
        x = _combine_call(slots, gw, x1, h2, mod3, ys, l, P, n_ctx, lat_t)

    y_prompt = x[:n_ctx].reshape(n_seq_c, ctx_t, D_MODEL)
    y_sample = x[n_ctx:].reshape(n_seq_l, lat_t, D_MODEL)
    return (y_prompt, y_sample, jnp.stack(ks, axis=1), jnp.stack(vs, axis=1), jnp.stack(ss, axis=1))


def kernel(x_prompt, x_sample, c, cache_k, cache_v, state_lru, c_ctx, w_mod, b_mod, norm1_g, w_in, q_norm_g, k_norm_g, lru_conv_w, lru_conv_b, lru_w_a, lru_b_a, lru_w_x, lru_b_x, lru_lambda, cm_dw_w, cm_dw_b, cm_ln_g, cm_ln_b, out_norm_g, w_out, norm2_g, w_router, b_router, w_e_gate, w_e_up, w_e_down, w_s_gate, w_s_up, w_s_down):
    P = {"w_mod": w_mod, "b_mod": b_mod, "norm1_g": norm1_g, "w_in": w_in, "q_norm_g": q_norm_g,
         "k_norm_g": k_norm_g, "lru_conv_w": lru_conv_w, "lru_conv_b": lru_conv_b, "lru_w_a": lru_w_a,
         "lru_b_a": lru_b_a, "lru_w_x": lru_w_x, "lru_b_x": lru_b_x, "lru_lambda": lru_lambda,
         "cm_dw_w": cm_dw_w, "cm_dw_b": cm_dw_b, "cm_ln_g": cm_ln_g, "cm_ln_b": cm_ln_b,
         "out_norm_g": out_norm_g, "w_out": w_out, "norm2_g": norm2_g, "w_router": w_router,
         "b_router": b_router, "w_e_gate": w_e_gate, "w_e_up": w_e_up, "w_e_down": w_e_down,
         "w_s_gate": w_s_gate, "w_s_up": w_s_up, "w_s_down": w_s_down}
    return _trunk(x_prompt, x_sample, c, cache_k, cache_v, state_lru, c_ctx, P)
```

```python
import functools
import math

import jax
import jax.numpy as jnp
from jax import lax
from jax.experimental import pallas as pl
from jax.experimental.pallas import tpu as pltpu

F32 = jnp.float32
BF16 = jnp.bfloat16
I32 = jnp.int32

D_MODEL = 1024
DEPTH = 2
GRID_W = 64
ATTN_WIDTH = 512
LRU_WIDTH = 256
CONV_WIDTH = 256
HEAD_DIM = 64
N_HEADS = 8
N_KV_HEADS = 2
KV_GROUP = N_HEADS // N_KV_HEADS
KV_WIDTH = N_KV_HEADS * HEAD_DIM
ROPE_BASE = 10000.0
LRU_HEADS = 4
LRU_HEAD_DIM = LRU_WIDTH // LRU_HEADS
LRU_CONV = 4
RG_C = 8.0
CM_KERNEL = 31
N_EXPERTS = 256
TOP_K = 8
EXPERT_FF = 256
ROUTED_SCALE = 2.5
EPS = 1e-6
IN_WIDTH = ATTN_WIDTH + 2 * KV_WIDTH + 2 * LRU_WIDTH + 2 * CONV_WIDTH

SUBLANES = 8
LANES = 128
VMEM_LIMIT = 56 * 1024 * 1024

TOKEN_TILE = 256
SCAN_CHUNK = 256
CONV_CHUNK = 128
CONV_HALO = 16
LRU_HALO = 8
EXPERT_BLOCK = 128
COMBINE_TILE = 128
DISPATCH_TILE = 512


def _params(sem):
    return pltpu.CompilerParams(dimension_semantics=sem, vmem_limit_bytes=VMEM_LIMIT)


def _sigmoid(x):
    return 1.0 / (1.0 + jnp.exp(-x))


def _silu(x):
    return x * _sigmoid(x)


def _bdot(a, b):
    return jnp.dot(a.astype(BF16), b.astype(BF16), preferred_element_type=F32)


def _split_dot(a, b_bf16):
    hi = a.astype(BF16)
    lo = (a - hi.astype(F32)).astype(BF16)
    return (jnp.dot(hi, b_bf16, preferred_element_type=F32)
            + jnp.dot(lo, b_bf16, preferred_element_type=F32))


def _rms(x, g):
    return x * lax.rsqrt(jnp.mean(x * x, axis=-1, keepdims=True) + EPS) * g


MOD_COLS = 1536


def _mod_kernel(c_ref, w_ref, b_ref, o_ref):
    a = _silu(c_ref[...])
    o_ref[...] = jnp.dot(a, w_ref[...], preferred_element_type=F32,
                         precision=lax.Precision.HIGHEST) + b_ref[...]


def _mod_call(cond, w_mod, b_mod):
    n_cond = cond.shape[0]
    width = 6 * D_MODEL
    return pl.pallas_call(
        _mod_kernel,
        out_shape=jax.ShapeDtypeStruct((DEPTH, n_cond, width), F32),
        grid=(DEPTH, width // MOD_COLS),
        in_specs=[
            pl.BlockSpec((n_cond, D_MODEL), lambda l, j: (0, 0)),
            pl.BlockSpec((None, D_MODEL, MOD_COLS), lambda l, j: (l, 0, j)),
            pl.BlockSpec((None, 1, MOD_COLS), lambda l, j: (l, 0, j)),
        ],
        out_specs=pl.BlockSpec((None, n_cond, MOD_COLS), lambda l, j: (l, 0, j)),
        compiler_params=_params(("arbitrary", "arbitrary")),
        name="mod",
    )(cond, w_mod, b_mod.reshape(DEPTH, 1, width))


def _swap_pairs(x):
    n = x.shape[-1]
    lane = lax.broadcasted_iota(I32, x.shape, 1)
    nxt = pltpu.roll(x, n - 1, 1)
    prv = pltpu.roll(x, 1, 1)
    return jnp.where(lane % 2 == 0, nxt, prv)


def _premix_kernel(x_ref, mod_ref, n1g_ref, w_ref, qg_ref, kg_ref, bd_ref, cos_ref, sin_ref,
                   q_ref, kr_ref, vb_ref, kc_ref, vc_ref, u_ref, gate_ref, cv_ref):
    x = x_ref[...]
    sh1 = mod_ref[:, 0:D_MODEL]
    sc1 = mod_ref[:, D_MODEL:2 * D_MODEL]
    h = _rms(x, n1g_ref[...]) * (1.0 + sc1) + sh1
    z = jnp.dot(h.astype(BF16), w_ref[...], preferred_element_type=F32)
    o = 0
    q = z[:, o:o + ATTN_WIDTH]; o += ATTN_WIDTH
    k = z[:, o:o + KV_WIDTH]; o += KV_WIDTH
    v = z[:, o:o + KV_WIDTH]; o += KV_WIDTH
    u_ref[...] = z[:, o:o + LRU_WIDTH]; o += LRU_WIDTH
    gate_ref[...] = z[:, o:o + LRU_WIDTH]; o += LRU_WIDTH
    cv_ref[...] = z[:, o:o + 2 * CONV_WIDTH]

    bd = bd_ref[...]
    inv_hd = 1.0 / HEAD_DIM
    q_ms = _split_dot(q * q, bd) * inv_hd
    k_ms = _split_dot(k * k, bd[:KV_WIDTH, :KV_WIDTH]) * inv_hd
    qn = q * lax.rsqrt(q_ms + EPS) * qg_ref[...]
    kn = k * lax.rsqrt(k_ms + EPS) * kg_ref[...]
    kc_ref[...] = kn
    vc_ref[...] = v
    ones = jnp.ones((v.shape[0], HEAD_DIM), F32)
    vb_ref[...] = jnp.concatenate([v[:, :HEAD_DIM], ones, v[:, HEAD_DIM:], ones], axis=1).astype(BF16)

    cos = cos_ref[...]
    sin = sin_ref[...]
    reps = ATTN_WIDTH // cos.shape[-1]
    cos_q = jnp.concatenate([cos] * reps, axis=1)
    sin_q = jnp.concatenate([sin] * reps, axis=1)
    qr = qn * cos_q + _swap_pairs(qn) * sin_q
    kr = kn * cos + _swap_pairs(kn) * sin
    q_ref[...] = (qr * (HEAD_DIM ** -0.5 * math.log2(math.e))).astype(BF16)
    kr_ref[...] = kr.astype(BF16)


def _premix_call(x, mod3, l, P, rope_cos, rope_sin, n_ctx, lat_t):
    n = x.shape[0]
    tm = TOKEN_TILE
    n_ctx_tiles = n_ctx // tm
    tiles_per_seq = lat_t // tm

    def mod_idx(i):
        return (jnp.where(i < n_ctx_tiles, 0, 1 + (i - n_ctx_tiles) // tiles_per_seq), 0, 0)

    def rope_idx(i):
        return (jnp.where(i < n_ctx_tiles, tiles_per_seq, (i - n_ctx_tiles) % tiles_per_seq), 0)

    const = lambda i: (0, 0)
    row = lambda i: (i, 0)
    outs = pl.pallas_call(
        _premix_kernel,
        out_shape=(
            jax.ShapeDtypeStruct((n, ATTN_WIDTH), BF16),
            jax.ShapeDtypeStruct((n, KV_WIDTH), BF16),
            jax.ShapeDtypeStruct((n, 2 * KV_WIDTH), BF16),
            jax.ShapeDtypeStruct((n, KV_WIDTH), F32),
            jax.ShapeDtypeStruct((n, KV_WIDTH), F32),
            jax.ShapeDtypeStruct((n, LRU_WIDTH), F32),
            jax.ShapeDtypeStruct((n, LRU_WIDTH), F32),
            jax.ShapeDtypeStruct((n, 2 * CONV_WIDTH), F32),
        ),
        grid=(n // tm,),
        in_specs=[
            pl.BlockSpec((tm, D_MODEL), row),
            pl.BlockSpec((None, 1, 6 * D_MODEL), mod_idx),
            pl.BlockSpec((1, D_MODEL), const),
            pl.BlockSpec((D_MODEL, IN_WIDTH), const),
            pl.BlockSpec((1, ATTN_WIDTH), const),
            pl.BlockSpec((1, KV_WIDTH), const),
            pl.BlockSpec((ATTN_WIDTH, ATTN_WIDTH), const),
            pl.BlockSpec((tm, LANES), rope_idx),
            pl.BlockSpec((tm, LANES), rope_idx),
        ],
        out_specs=(
            pl.BlockSpec((tm, ATTN_WIDTH), row),
            pl.BlockSpec((tm, KV_WIDTH), row),
            pl.BlockSpec((tm, 2 * KV_WIDTH), row),
            pl.BlockSpec((tm, KV_WIDTH), row),
            pl.BlockSpec((tm, KV_WIDTH), row),
            pl.BlockSpec((tm, LRU_WIDTH), row),
            pl.BlockSpec((tm, LRU_WIDTH), row),
            pl.BlockSpec((tm, 2 * CONV_WIDTH), row),
        ),
        compiler_params=_params(("parallel",)),
        name="premix",
    )(x, mod3, P["norm1_g"][l][None, :], P["w_in_bf16"][l], P["q_norm_g_t"][l][None, :],
      P["k_norm_g_t"][l][None, :], P["head_blockdiag"], rope_cos, rope_sin)
    return outs


def _attn_kernel(q_ref, k_ref, v_ref, o_ref):
    outs = []
    for kh in range(N_KV_HEADS):
        k = k_ref[:, kh * HEAD_DIM:(kh + 1) * HEAD_DIM]
        v = v_ref[:, kh * 2 * HEAD_DIM:(kh + 1) * 2 * HEAD_DIM]
        for g in range(KV_GROUP):
            hd = kh * KV_GROUP + g
            q = q_ref[:, hd * HEAD_DIM:(hd + 1) * HEAD_DIM]
            s = lax.dot_general(q, k, (((1,), (1,)), ((), ())), preferred_element_type=F32)
            m = jnp.max(s, axis=-1, keepdims=True)
            p = jnp.exp2(s - m).astype(BF16)
            o = jnp.dot(p, v, preferred_element_type=F32)
            outs.append(o[:, :HEAD_DIM] / o[:, HEAD_DIM:HEAD_DIM + 1])
    o_ref[...] = jnp.concatenate(outs, axis=1)


def _attn_call(q3, k3, v3, tq):
    b, s, _ = q3.shape
    t = k3.shape[1]
    return pl.pallas_call(
        _attn_kernel,
        out_shape=jax.ShapeDtypeStruct((b, s, ATTN_WIDTH), F32),
        grid=(b, s // tq),
        in_specs=[
            pl.BlockSpec((None, tq, ATTN_WIDTH), lambda i, j: (i, j, 0)),
            pl.BlockSpec((None, t, KV_WIDTH), lambda i, j: (i, 0, 0)),
            pl.BlockSpec((None, t, 2 * KV_WIDTH), lambda i, j: (i, 0, 0)),
        ],
        out_specs=pl.BlockSpec((None, tq, ATTN_WIDTH), lambda i, j: (i, j, 0)),
        compiler_params=_params(("parallel", "parallel")),
        name="attention",
    )(q3, k3, v3)


def _chunk_scan(a, b, reverse):
    n = a.shape[0]
    row = lax.broadcasted_iota(I32, a.shape, 0)
    d = 1
    while d < n:
        if reverse:
            a_s = pltpu.roll(a, n - d, 0)
            b_s = pltpu.roll(b, n - d, 0)
            ok = row < n - d
        else:
            a_s = pltpu.roll(a, d, 0)
            b_s = pltpu.roll(b, d, 0)
            ok = row >= d
        b = jnp.where(ok, a * b_s + b, b)
        a = jnp.where(ok, a * a_s, a)
        d *= 2
    return a, b


def _gelu_tanh(x):
    return 0.5 * x * (1.0 + jnp.tanh(math.sqrt(2.0 / math.pi) * (x + 0.044715 * (x * x * x))))


def _lru_kernel(u_ref, gate_ref, cw_ref, cb_ref, wg_ref, bg_ref, lam_ref, h0_ref,
                y_ref, fin_ref, upad, xc_scr, fwd_scr):
    t_len = u_ref.shape[0]
    ch = SCAN_CHUNK
    n_chunks = t_len // ch
    w = LRU_WIDTH
    zeros_halo = jnp.zeros((LRU_HALO, w), F32)
    upad[0:LRU_HALO, :] = zeros_halo
    upad[t_len + LRU_HALO:t_len + 2 * LRU_HALO, :] = zeros_halo

    def fill(c, carry):
        r0 = pl.multiple_of(c * ch, ch)
        upad[pl.ds(r0 + LRU_HALO, ch), :] = u_ref[pl.ds(r0, ch), :]
        return carry

    lax.fori_loop(0, n_chunks, fill, 0)

    lam = lam_ref[...]
    nlam = -lam
    softplus = jnp.maximum(nlam, 0.0) + jnp.log(1.0 + jnp.exp(-jnp.abs(nlam)))
    decay = -RG_C * softplus
    cw = cw_ref[...]
    cb = cb_ref[...]

    def gates(xc, d):
        z = jnp.dot(xc.astype(BF16), wg_ref[:, 2 * d * w:2 * (d + 1) * w],
                    preferred_element_type=F32) + bg_ref[:, 2 * d * w:2 * (d + 1) * w]
        r = _sigmoid(z[:, :w])
        i = _sigmoid(z[:, w:])
        log_a = decay[d:d + 1, :] * r
        a = jnp.exp(log_a)
        b = jnp.sqrt(-jnp.tanh(log_a) * (a * a + 1.0)) * (i * xc)
        return a, b

    def fwd(c, h):
        r0 = pl.multiple_of(c * ch, ch)
        win = upad[pl.ds(r0, ch + 2 * LRU_HALO), :]
        xc = cb
        for j in range(LRU_CONV):
            s0 = LRU_HALO - (LRU_CONV - 1) // 2 + j
            xc = xc + cw[j:j + 1, :] * win[s0:s0 + ch, :]
        xc_scr[pl.ds(r0, ch), :] = xc
        a, b = gates(xc, 0)
        a_cum, b_cum = _chunk_scan(a, b, reverse=False)
        hs = a_cum * h + b_cum
        fwd_scr[pl.ds(r0, ch), :] = hs
        return hs[ch - 1:ch, :]

    h_f = lax.fori_loop(0, n_chunks, fwd, h0_ref[0:1, :])

    def bwd(ci, h):
        c = n_chunks - 1 - ci
        r0 = pl.multiple_of(c * ch, ch)
        xc = xc_scr[pl.ds(r0, ch), :]
        a, b = gates(xc, 1)
        a_cum, b_cum = _chunk_scan(a, b, reverse=True)
        hs = a_cum * h + b_cum
        y_ref[pl.ds(r0, ch), :] = (fwd_scr[pl.ds(r0, ch), :] + hs) * _gelu_tanh(gate_ref[pl.ds(r0, ch), :])
        return hs[0:1, :]

    h_b = lax.fori_loop(0, n_chunks, bwd, h0_ref[1:2, :])
    fin_ref[0:1, :] = h_f
    fin_ref[1:2, :] = h_b


def _lru_call(u3, gate3, h0, l, P):
    b, t, w = u3.shape
    const = lambda i: (0, 0)
    seq = lambda i: (i, 0, 0)
    return pl.pallas_call(
        _lru_kernel,
        out_shape=(jax.ShapeDtypeStruct((b, t, w), F32),
                   jax.ShapeDtypeStruct((b, 2, w), F32)),
        grid=(b,),
        in_specs=[
            pl.BlockSpec((None, t, w), seq),
            pl.BlockSpec((None, t, w), seq),
            pl.BlockSpec((LRU_CONV, w), const),
            pl.BlockSpec((1, w), const),
            pl.BlockSpec((w, 4 * w), const),
            pl.BlockSpec((1, 4 * w), const),
            pl.BlockSpec((2, w), const),
            pl.BlockSpec((None, 2, w), seq),
        ],
        out_specs=(pl.BlockSpec((None, t, w), seq),
                   pl.BlockSpec((None, 2, w), seq)),
        scratch_shapes=[pltpu.VMEM((t + 2 * LRU_HALO, w), F32),
                        pltpu.VMEM((t, w), F32),
                        pltpu.VMEM((t, w), F32)],
        compiler_params=_params(("parallel",)),
        name="rglru",
    )(u3, gate3, P["lru_conv_w"][l], P["lru_conv_b"][l][None, :], P["lru_gate_w"][l],
      P["lru_gate_b"][l][None, :], P["lru_lambda"][l], h0)


def _convmod_kernel(cv_ref, w_ref, b_ref, g_ref, beta_ref, o_ref, hpad):
    t_len = cv_ref.shape[0]
    ch = CONV_CHUNK
    n_chunks = t_len // ch
    w = CONV_WIDTH
    zeros_halo = jnp.zeros((CONV_HALO, w), F32)
    hpad[0:CONV_HALO, :] = zeros_halo
    hpad[t_len + CONV_HALO:t_len + 2 * CONV_HALO, :] = zeros_halo

    def glu(c, carry):
        r0 = pl.multiple_of(c * ch, ch)
        blk = cv_ref[pl.ds(r0, ch), :]
        hpad[pl.ds(r0 + CONV_HALO, ch), :] = blk[:, :w] * _sigmoid(blk[:, w:])
        return carry

    lax.fori_loop(0, n_chunks, glu, 0)

    taps = w_ref[...]
    bias = b_ref[...]
    gamma = g_ref[...]
    beta = beta_ref[...]

    def conv(c, carry):
        r0 = pl.multiple_of(c * ch, ch)
        win = hpad[pl.ds(r0, ch + 2 * CONV_HALO), :]
        acc = bias
        for j in range(CM_KERNEL):
            s0 = CONV_HALO - CM_KERNEL // 2 + j
            acc = acc + taps[j:j + 1, :] * win[s0:s0 + ch, :]
        mu = jnp.mean(acc, axis=-1, keepdims=True)
        cen = acc - mu
        var = jnp.mean(cen * cen, axis=-1, keepdims=True)
        y = cen * lax.rsqrt(var + EPS) * gamma + beta
        o_ref[pl.ds(r0, ch), :] = _silu(y)
        return carry

    lax.fori_loop(0, n_chunks, conv, 0)


def _convmod_call(cv3, l, P):
    b, t, _ = cv3.shape
    w = CONV_WIDTH
    const = lambda i: (0, 0)
    seq = lambda i: (i, 0, 0)
    return pl.pallas_call(
        _convmod_kernel,
        out_shape=jax.ShapeDtypeStruct((b, t, w), F32),
        grid=(b,),
        in_specs=[
            pl.BlockSpec((None, t, 2 * w), seq),
            pl.BlockSpec((CM_KERNEL, w), const),
            pl.BlockSpec((1, w), const),
            pl.BlockSpec((1, w), const),
            pl.BlockSpec((1, w), const),
        ],
        out_specs=pl.BlockSpec((None, t, w), seq),
        scratch_shapes=[pltpu.VMEM((t + 2 * CONV_HALO, w), F32)],
        compiler_params=_params(("parallel",)),
        name="convmod",
    )(cv3, P["cm_dw_w"][l], P["cm_dw_b"][l][None, :], P["cm_ln_g"][l][None, :], P["cm_ln_b"][l][None, :])


def _postmix_kernel(attn_ref, lru_ref, conv_ref, x_ref, mod_ref, og_ref, wo_ref, n2g_ref, wr_hi_ref, wr_lo_ref,
                    x1_ref, h2_ref, lg_ref):
    og = og_ref[...]
    a0, a1, a2 = ATTN_WIDTH, ATTN_WIDTH + LRU_WIDTH, D_MODEL
    y = jnp.concatenate([_rms(attn_ref[...], og[:, :a0]),
                         _rms(lru_ref[...], og[:, a0:a1]),
                         _rms(conv_ref[...], og[:, a1:a2])], axis=1)
    y = jnp.dot(y.astype(BF16), wo_ref[...], preferred_element_type=F32)
    g1 = mod_ref[:, 2 * D_MODEL:3 * D_MODEL]
    sh2 = mod_ref[:, 3 * D_MODEL:4 * D_MODEL]
    sc2 = mod_ref[:, 4 * D_MODEL:5 * D_MODEL]
    x1 = x_ref[...] + g1 * y
    x1_ref[...] = x1
    h2 = _rms(x1, n2g_ref[...]) * (1.0 + sc2) + sh2
    h2_ref[...] = h2
    hi = h2.astype(BF16)
    lo = (h2 - hi.astype(F32)).astype(BF16)
    w_hi = wr_hi_ref[...]
    lg_ref[...] = (jnp.dot(hi, w_hi, preferred_element_type=F32)
                   + jnp.dot(lo, w_hi, preferred_element_type=F32)
                   + jnp.dot(hi, wr_lo_ref[...], preferred_element_type=F32))


def _postmix_call(attn, lru, conv, x, mod3, l, P, n_ctx, lat_t):
    n = x.shape[0]
    tm = TOKEN_TILE
    n_ctx_tiles = n_ctx // tm
    tiles_per_seq = lat_t // tm

    def mod_idx(i):
        return (jnp.where(i < n_ctx_tiles, 0, 1 + (i - n_ctx_tiles) // tiles_per_seq), 0, 0)

    const = lambda i: (0, 0)
    row = lambda i: (i, 0)
    return pl.pallas_call(
        _postmix_kernel,
        out_shape=(jax.ShapeDtypeStruct((n, D_MODEL), F32),
                   jax.ShapeDtypeStruct((n, D_MODEL), F32),
                   jax.ShapeDtypeStruct((n, N_EXPERTS), F32)),
        grid=(n // tm,),
        in_specs=[
            pl.BlockSpec((tm, ATTN_WIDTH), row),
            pl.BlockSpec((tm, LRU_WIDTH), row),
            pl.BlockSpec((tm, CONV_WIDTH), row),
            pl.BlockSpec((tm, D_MODEL), row),
            pl.BlockSpec((None, 1, 6 * D_MODEL), mod_idx),
            pl.BlockSpec((1, D_MODEL), const),
            pl.BlockSpec((D_MODEL, D_MODEL), const),
            pl.BlockSpec((1, D_MODEL), const),
            pl.BlockSpec((D_MODEL, N_EXPERTS), const),
            pl.BlockSpec((D_MODEL, N_EXPERTS), const),
        ],
        out_specs=(pl.BlockSpec((tm, D_MODEL), row),
                   pl.BlockSpec((tm, D_MODEL), row),
                   pl.BlockSpec((tm, N_EXPERTS), row)),
        compiler_params=_params(("parallel",)),
        name="postmix",
    )(attn, lru, conv, x, mod3, P["out_norm_g"][l][None, :], P["w_out_bf16"][l], P["norm2_g"][l][None, :],
      P["w_router_hi"][l], P["w_router_lo"][l])


def _route_kernel(lg_ref, br_ref, tri_ref, ids_ref, gw_ref, rank_ref, cnt_ref, carry):
    i = pl.program_id(0)

    @pl.when(i == 0)
    def _():
        carry[...] = jnp.zeros_like(carry)

    scores = _sigmoid(lg_ref[...])
    sel = scores + br_ref[...]
    tm = scores.shape[0]
    lane = lax.broadcasted_iota(I32, (tm, N_EXPERTS), 1).astype(F32)
    slot_lane = lax.broadcasted_iota(I32, (tm, LANES), 1)
    ids_acc = jnp.zeros((tm, LANES), F32)
    gw_acc = jnp.zeros((tm, LANES), F32)
    hot = jnp.zeros((tm, N_EXPERTS), F32)
    picked = []
    for k in range(TOP_K):
        m = jnp.max(sel, axis=-1, keepdims=True)
        idx = jnp.min(jnp.where(sel == m, lane, float(N_EXPERTS)), axis=-1, keepdims=True)
        one = lane == idx
        g = jnp.sum(jnp.where(one, scores, 0.0), axis=-1, keepdims=True)
        sel = jnp.where(one, -jnp.inf, sel)
        hot = jnp.where(one, 1.0, hot)
        ids_acc = jnp.where(slot_lane == k, idx, ids_acc)
        gw_acc = jnp.where(slot_lane == k, g, gw_acc)
        picked.append(idx)
    denom = jnp.sum(gw_acc, axis=-1, keepdims=True)
    gw_acc = ROUTED_SCALE * gw_acc / denom

    before = jnp.dot(tri_ref[...], hot.astype(BF16), preferred_element_type=F32) + carry[0:1, :]
    rank_acc = jnp.zeros((tm, LANES), F32)
    for k in range(TOP_K):
        r = jnp.sum(jnp.where(lane == picked[k], before, 0.0), axis=-1, keepdims=True)
        rank_acc = jnp.where(slot_lane == k, r, rank_acc)
    carry[0:1, :] = carry[0:1, :] + jnp.sum(hot, axis=0, keepdims=True)
    cnt_ref[...] = carry[...]
    ids_ref[...] = ids_acc[:, :TOP_K].astype(I32)
    gw_ref[...] = gw_acc[:, :TOP_K]
    rank_ref[...] = rank_acc[:, :TOP_K].astype(I32)


def _route_call(logits, b_router_l, tri):
    n = logits.shape[0]
    tm = TOKEN_TILE
    const = lambda i: (0, 0)
    row = lambda i: (i, 0)
    return pl.pallas_call(
        _route_kernel,
        out_shape=(jax.ShapeDtypeStruct((n, TOP_K), I32),
                   jax.ShapeDtypeStruct((n, TOP_K), F32),
                   jax.ShapeDtypeStruct((n, TOP_K), I32),
                   jax.ShapeDtypeStruct((SUBLANES, N_EXPERTS), F32)),
        grid=(n // tm,),
        in_specs=[
            pl.BlockSpec((tm, N_EXPERTS), row),
            pl.BlockSpec((1, N_EXPERTS), const),
            pl.BlockSpec((tm, tm), const),
        ],
        out_specs=(pl.BlockSpec((tm, TOP_K), row),
                   pl.BlockSpec((tm, TOP_K), row),
                   pl.BlockSpec((tm, TOP_K), row),
                   pl.BlockSpec((SUBLANES, N_EXPERTS), const)),
        scratch_shapes=[pltpu.VMEM((SUBLANES, N_EXPERTS), F32)],
        compiler_params=_params(("arbitrary",)),
        name="route",
    )(logits, b_router_l[None, :], tri)


def _slots_kernel(ids_ref, rank_ref, start_ref, slots_ref):
    ids = ids_ref[...].astype(F32)
    tm = ids.shape[0]
    lane = lax.broadcasted_iota(I32, (tm, N_EXPERTS), 1).astype(F32)
    slot_lane = lax.broadcasted_iota(I32, (tm, TOP_K), 1)
    start = start_ref[...]
    acc = jnp.zeros((tm, TOP_K), F32)
    for k in range(TOP_K):
        s = jnp.sum(jnp.where(lane == ids[:, k:k + 1], start, 0.0), axis=-1, keepdims=True)
        acc = jnp.where(slot_lane == k, s, acc)
    slots_ref[...] = acc.astype(I32) + rank_ref[...]


def _slots_call(ids, rank, group_start):
    n = ids.shape[0]
    tm = TOKEN_TILE
    row = lambda i: (i, 0)
    return pl.pallas_call(
        _slots_kernel,
        out_shape=jax.ShapeDtypeStruct((n, TOP_K), I32),
        grid=(n // tm,),
        in_specs=[pl.BlockSpec((tm, TOP_K), row), pl.BlockSpec((tm, TOP_K), row),
                  pl.BlockSpec((1, N_EXPERTS), lambda i: (0, 0))],
        out_specs=pl.BlockSpec((tm, TOP_K), row),
        compiler_params=_params(("parallel",)),
        name="slots",
    )(ids, rank, group_start)


def _wait_rows(src_ref, dst_ref, sem, n_waits):
    for _ in range(n_waits):
        pltpu.make_async_copy(src_ref, dst_ref, sem).wait()


def _dispatch_kernel(first_ref, slots_ref, h_ref, xs_ref, zeros, sem_rows, sem_zero):
    n_tok = h_ref.shape[0]
    bm = EXPERT_BLOCK
    n_blocks = xs_ref.shape[0] // bm
    nused = first_ref[N_EXPERTS]

    def zero_copy(b):
        return pltpu.make_async_copy(zeros, xs_ref.at[pl.ds(pl.multiple_of(b * bm, bm), bm), :], sem_zero)

    @pl.when(pl.program_id(0) == 0)
    def _():
        zeros[...] = jnp.zeros_like(zeros)

        def group(start):
            def body(e, c):
                @pl.when(first_ref[e + 1] > first_ref[e])
                def _():
                    cp = zero_copy(first_ref[e + 1] - 1)
                    cp.start() if start else cp.wait()
                return c
            lax.fori_loop(0, N_EXPERTS, body, 0)

        def tail(start):
            def body(b, c):
                cp = zero_copy(b)
                cp.start() if start else cp.wait()
                return c
            lax.fori_loop(nused, n_blocks, body, 0)

        group(True)
        tail(True)
        group(False)
        tail(False)

    def issue(t, c):
        for k in range(TOP_K):
            slot = slots_ref[t * TOP_K + k]
            pltpu.make_async_copy(h_ref.at[pl.ds(t, 1), :], xs_ref.at[pl.ds(slot, 1), :],
                                  sem_rows).start(priority=k % 2)
        return c

    lax.fori_loop(0, n_tok, issue, 0)
    _wait_rows(h_ref, xs_ref.at[pl.ds(0, n_tok), :], sem_rows, TOP_K)


def _dispatch_call(first_blk, slots_flat, h2, n_slots):
    n = h2.shape[0]
    tm = DISPATCH_TILE
    grid_spec = pltpu.PrefetchScalarGridSpec(
        num_scalar_prefetch=1,
        grid=(n // tm,),
        in_specs=[
            pl.BlockSpec((tm * TOP_K,), lambda i, first: (i,), memory_space=pltpu.SMEM),
            pl.BlockSpec((tm, D_MODEL), lambda i, first: (i, 0)),
        ],
        out_specs=pl.BlockSpec(memory_space=pl.ANY),
        scratch_shapes=[pltpu.VMEM((EXPERT_BLOCK, D_MODEL), F32),
                        pltpu.SemaphoreType.DMA,
                        pltpu.SemaphoreType.DMA],
    )
    return pl.pallas_call(
        _dispatch_kernel,
        out_shape=jax.ShapeDtypeStruct((n_slots, D_MODEL), F32),
        grid_spec=grid_spec,
        compiler_params=_params(("arbitrary",)),
        name="dispatch",
    )(first_blk, slots_flat, h2)


def _expert_kernel(first_ref, xs_ref, wg_ref, wu_ref, wd_ref, ys_ref,
                   xbuf, ybuf, wg_bf, wu_bf, wd_bf, sem_in, sem_out):
    e = pl.program_id(0)
    bm = EXPERT_BLOCK
    n_blocks = ys_ref.shape[0] // bm
    lo = first_ref[e]
    hi = first_ref[e + 1]
    nused = first_ref[N_EXPERTS]

    def rows(b):
        return pl.ds(pl.multiple_of(b * bm, bm), bm)

    def load(b, s):
        return pltpu.make_async_copy(xs_ref.at[rows(b), :], xbuf.at[s], sem_in.at[s])

    def store(b, s):
        return pltpu.make_async_copy(ybuf.at[s], ys_ref.at[rows(b), :], sem_out.at[s])

    @pl.when(jnp.logical_and(e == 0, nused > 0))
    def _():
        load(0, 0).start()

    @pl.when(hi > lo)
    def _():
        wg_bf[...] = wg_ref[...].astype(BF16)
        wu_bf[...] = wu_ref[...].astype(BF16)
        wd_bf[...] = wd_ref[...].astype(BF16)

    def block(b, c):
        s = b % 2
        load(b, s).wait()

        @pl.when(b + 1 < nused)
        def _():
            load(b + 1, 1 - s).start()

        @pl.when(b >= 2)
        def _():
            store(b - 2, s).wait()

        x = xbuf[s].astype(BF16)
        g = jnp.dot(x, wg_bf[...], preferred_element_type=F32)
        u = jnp.dot(x, wu_bf[...], preferred_element_type=F32)
        h = (_silu(g) * u).astype(BF16)
        ybuf[s] = jnp.dot(h, wd_bf[...], preferred_element_type=F32)
        store(b, s).start()
        return c

    lax.fori_loop(lo, hi, block, 0)

    @pl.when(e == pl.num_programs(0) - 1)
    def _():
        @pl.when(nused >= 2)
        def _():
            store(nused - 2, nused % 2).wait()

        @pl.when(nused >= 1)
        def _():
            store(nused - 1, (nused - 1) % 2).wait()

        ybuf[0] = jnp.zeros((bm, D_MODEL), F32)

        def fill(b, c):
            store(b, 0).start()
            return c

        def drain(b, c):
            store(b, 0).wait()
            return c

        lax.fori_loop(nused, n_blocks, fill, 0)
        lax.fori_loop(nused, n_blocks, drain, 0)


def _expert_call(first_blk, xs, w_e_gate, w_e_up, w_e_down, l):
    n_slots = xs.shape[0]
    bm = EXPERT_BLOCK

    def w_idx(e, first_ref):
        return (l, e, 0, 0)

    grid_spec = pltpu.PrefetchScalarGridSpec(
        num_scalar_prefetch=1,
        grid=(N_EXPERTS,),
        in_specs=[
            pl.BlockSpec(memory_space=pl.ANY),
            pl.BlockSpec((None, None, D_MODEL, EXPERT_FF), w_idx),
            pl.BlockSpec((None, None, D_MODEL, EXPERT_FF), w_idx),
            pl.BlockSpec((None, None, EXPERT_FF, D_MODEL), w_idx),
        ],
        out_specs=pl.BlockSpec(memory_space=pl.ANY),
        scratch_shapes=[pltpu.VMEM((2, bm, D_MODEL), F32),
                        pltpu.VMEM((2, bm, D_MODEL), F32),
                        pltpu.VMEM((D_MODEL, EXPERT_FF), BF16),
                        pltpu.VMEM((D_MODEL, EXPERT_FF), BF16),
                        pltpu.VMEM((EXPERT_FF, D_MODEL), BF16),
                        pltpu.SemaphoreType.DMA((2,)),
                        pltpu.SemaphoreType.DMA((2,))],
    )
    return pl.pallas_call(
        _expert_kernel,
        out_shape=jax.ShapeDtypeStruct((n_slots, D_MODEL), F32),
        grid_spec=grid_spec,
        compiler_params=_params(("arbitrary",)),
        name="experts",
    )(first_blk, xs, w_e_gate, w_e_up, w_e_down)


def _combine_kernel(slots_ref, slots_next_ref, gw_ref, x1_ref, h2_ref, mod_ref, wsg_ref, wsu_ref, wsd_ref, ys_ref,
                    o_ref, buf, sems):
    i = pl.program_id(0)
    n_tok = x1_ref.shape[0]
    half = i % 2

    def issue(idx_ref, dst_half):
        def body(t, c):
            for k in range(TOP_K):
                slot = idx_ref[t * TOP_K + k]
                pltpu.make_async_copy(ys_ref.at[pl.ds(slot, 1), :], buf.at[dst_half, k, pl.ds(t, 1), :],
                                      sems.at[dst_half]).start(priority=k % 2)
            return c
        lax.fori_loop(0, n_tok, body, 0)

    @pl.when(i == 0)
    def _():
        issue(slots_ref, 0)

    @pl.when(i + 1 < pl.num_programs(0))
    def _():
        issue(slots_next_ref, 1 - half)

    h = h2_ref[...].astype(BF16)
    hid = _silu(jnp.dot(h, wsg_ref[...], preferred_element_type=F32)) * jnp.dot(
        h, wsu_ref[...], preferred_element_type=F32)
    acc = jnp.dot(hid.astype(BF16), wsd_ref[...], preferred_element_type=F32)

    for k in range(TOP_K):
        pltpu.make_async_copy(ys_ref.at[pl.ds(0, n_tok), :], buf.at[half, k], sems.at[half]).wait()
    gw = gw_ref[...]
    for k in range(TOP_K):
        acc = acc + gw[:, k:k + 1] * buf[half, k]
    g2 = mod_ref[:, 5 * D_MODEL:6 * D_MODEL]
    o_ref[...] = x1_ref[...] + g2 * acc


def _combine_call(slots_flat, gw, x1, h2, mod3, ys, l, P, n_ctx, lat_t):
    n = x1.shape[0]
    tm = COMBINE_TILE
    n_ctx_tiles = n_ctx // tm
    tiles_per_seq = lat_t // tm

    def mod_idx(i):
        return (jnp.where(i < n_ctx_tiles, 0, 1 + (i - n_ctx_tiles) // tiles_per_seq), 0, 0)

    const = lambda i: (0, 0)
    row = lambda i: (i, 0)
    n_tiles = n // tm
    return pl.pallas_call(
        _combine_kernel,
        out_shape=jax.ShapeDtypeStruct((n, D_MODEL), F32),
        grid=(n_tiles,),
        in_specs=[
            pl.BlockSpec((tm * TOP_K,), lambda i: (i,), memory_space=pltpu.SMEM),
            pl.BlockSpec((tm * TOP_K,), lambda i: (jnp.minimum(i + 1, n_tiles - 1),), memory_space=pltpu.SMEM),
            pl.BlockSpec((tm, TOP_K), row),
            pl.BlockSpec((tm, D_MODEL), row),
            pl.BlockSpec((tm, D_MODEL), row),
            pl.BlockSpec((None, 1, 6 * D_MODEL), mod_idx),
            pl.BlockSpec((D_MODEL, EXPERT_FF), const),
            pl.BlockSpec((D_MODEL, EXPERT_FF), const),
            pl.BlockSpec((EXPERT_FF, D_MODEL), const),
            pl.BlockSpec(memory_space=pl.ANY),
        ],
        out_specs=pl.BlockSpec((tm, D_MODEL), row),
        scratch_shapes=[pltpu.VMEM((2, TOP_K, tm, D_MODEL), F32),
                        pltpu.SemaphoreType.DMA((2,))],
        compiler_params=_params(("arbitrary",)),
        name="combine",
    )(slots_flat, slots_flat, gw, x1, h2, mod3, P["w_s_gate_bf16"][l], P["w_s_up_bf16"][l],
      P["w_s_down_bf16"][l], ys)


def _rope_tables(lat_t):
    rows = lat_t // GRID_W
    row = jnp.repeat(jnp.arange(rows, dtype=F32), GRID_W)
    col = jnp.tile(jnp.arange(GRID_W, dtype=F32), rows)
    n_freq = HEAD_DIM // 4
    inv = jnp.power(ROPE_BASE, -jnp.arange(n_freq, dtype=F32) / n_freq)
    ang = jnp.concatenate([row[:, None] * inv, col[:, None] * inv], axis=-1)
    cos = jnp.repeat(jnp.cos(ang), 2, axis=-1)
    sign = jnp.tile(jnp.array([-1.0, 1.0], F32), HEAD_DIM // 2)
    sin = jnp.repeat(jnp.sin(ang), 2, axis=-1) * sign
    cos = jnp.concatenate([cos, jnp.ones((TOKEN_TILE, HEAD_DIM), F32)], axis=0)
    sin = jnp.concatenate([sin, jnp.zeros((TOKEN_TILE, HEAD_DIM), F32)], axis=0)
    reps = LANES // HEAD_DIM
    return jnp.tile(cos, (1, reps)), jnp.tile(sin, (1, reps))


def _block_diag_gates(w_a, w_x):
    def dense(w):
        eye = jnp.eye(LRU_HEADS, dtype=w.dtype)
        return jnp.einsum("ncd,nm->ncmd", w, eye).reshape(LRU_WIDTH, LRU_WIDTH)
    return jnp.concatenate([dense(w_a[0]), dense(w_x[0]), dense(w_a[1]), dense(w_x[1])], axis=1)


def _prepare(P):
    Q = dict(P)
    Q["w_in_bf16"] = P["w_in"].astype(BF16)
    Q["w_out_bf16"] = P["w_out"].astype(BF16)
    Q["q_norm_g_t"] = jnp.tile(P["q_norm_g"], (1, N_HEADS))
    Q["k_norm_g_t"] = jnp.tile(P["k_norm_g"], (1, N_KV_HEADS))
    head = jnp.arange(ATTN_WIDTH) // HEAD_DIM
    Q["head_blockdiag"] = (head[:, None] == head[None, :]).astype(BF16)
    Q["lru_gate_w"] = jnp.stack([_block_diag_gates(P["lru_w_a"][l], P["lru_w_x"][l])
                                 for l in range(DEPTH)]).astype(BF16)
    Q["lru_gate_b"] = jnp.concatenate([P["lru_b_a"][:, 0], P["lru_b_x"][:, 0],
                                       P["lru_b_a"][:, 1], P["lru_b_x"][:, 1]], axis=-1)
    w_hi = P["w_router"].astype(BF16)
    Q["w_router_hi"] = w_hi
    Q["w_router_lo"] = (P["w_router"] - w_hi.astype(F32)).astype(BF16)
    Q["w_s_gate_bf16"] = P["w_s_gate"].astype(BF16)
    Q["w_s_up_bf16"] = P["w_s_up"].astype(BF16)
    Q["w_s_down_bf16"] = P["w_s_down"].astype(BF16)
    return Q


def _group_tables(counts):
    bm = EXPERT_BLOCK
    blocks = (counts.astype(I32) + bm - 1) // bm
    first_blk = jnp.concatenate([jnp.zeros((1,), I32), jnp.cumsum(blocks).astype(I32)])
    group_start = (first_blk[:N_EXPERTS] * bm).astype(F32)[None, :]
    return group_start, first_blk


def _trunk(x_prompt, x_sample, c, cache_k, cache_v, state_lru, c_ctx, P):
    n_seq_c, ctx_t, _ = x_prompt.shape
    n_seq_l, lat_t, _ = x_sample.shape
    past = cache_k.shape[2]
    n_ctx = n_seq_c * ctx_t
    n_lat = n_seq_l * lat_t
    n = n_ctx + n_lat
    P = _prepare(P)

    n_cond = -(-(1 + n_seq_l) // SUBLANES) * SUBLANES
    cond = jnp.concatenate([c_ctx[None, :], c, jnp.zeros((n_cond - 1 - n_seq_l, D_MODEL), F32)], axis=0)
    mods = _mod_call(cond, P["w_mod"], P["b_mod"])
    rope_cos, rope_sin = _rope_tables(lat_t)
    tri = (jnp.arange(TOKEN_TILE)[:, None] > jnp.arange(TOKEN_TILE)[None, :]).astype(BF16)
    n_blocks = n * TOP_K // EXPERT_BLOCK + N_EXPERTS
    n_slots = n_blocks * EXPERT_BLOCK

    x = jnp.concatenate([x_prompt.reshape(n_ctx, D_MODEL), x_sample.reshape(n_lat, D_MODEL)], axis=0)
    ks, vs, ss = [], [], []
    for l in range(DEPTH):
        mod3 = mods[l].reshape(n_cond, 1, 6 * D_MODEL)
        q, kr, vb, kc, vc, u, gate, cv = _premix_call(x, mod3, l, P, rope_cos, rope_sin, n_ctx, lat_t)
        ks.append(kc[:n_ctx].reshape(n_seq_c, ctx_t, N_KV_HEADS, HEAD_DIM))
        vs.append(vc[:n_ctx].reshape(n_seq_c, ctx_t, N_KV_HEADS, HEAD_DIM))

        attn_c = _attn_call(q[:n_ctx].reshape(n_seq_c, ctx_t, ATTN_WIDTH),
                            kr[:n_ctx].reshape(n_seq_c, ctx_t, KV_WIDTH),
                            vb[:n_ctx].reshape(n_seq_c, ctx_t, 2 * KV_WIDTH), tq=min(ctx_t, 256))
        k_all = jnp.concatenate([kr[n_ctx:].reshape(n_seq_l, lat_t, KV_WIDTH),
                                 cache_k[:, l].reshape(n_seq_l, past, KV_WIDTH).astype(BF16)], axis=1)
        cv_ones = jnp.concatenate([cache_v[:, l], jnp.ones_like(cache_v[:, l])], axis=-1)
        v_all = jnp.concatenate([vb[n_ctx:].reshape(n_seq_l, lat_t, 2 * KV_WIDTH),
                                 cv_ones.reshape(n_seq_l, past, 2 * KV_WIDTH).astype(BF16)], axis=1)
        attn_l = _attn_call(q[n_ctx:].reshape(n_seq_l, lat_t, ATTN_WIDTH), k_all, v_all, tq=256)

        lru_c, fin_c = _lru_call(u[:n_ctx].reshape(n_seq_c, ctx_t, LRU_WIDTH),
                                 gate[:n_ctx].reshape(n_seq_c, ctx_t, LRU_WIDTH),
                                 jnp.zeros((n_seq_c, 2, LRU_WIDTH), F32), l, P)
        lru_l, _ = _lru_call(u[n_ctx:].reshape(n_seq_l, lat_t, LRU_WIDTH),
                             gate[n_ctx:].reshape(n_seq_l, lat_t, LRU_WIDTH),
                             state_lru[:, l], l, P)
        ss.append(fin_c)
        conv_c = _convmod_call(cv[:n_ctx].reshape(n_seq_c, ctx_t, 2 * CONV_WIDTH), l, P)
        conv_l = _convmod_call(cv[n_ctx:].reshape(n_seq_l, lat_t, 2 * CONV_WIDTH), l, P)

        attn = jnp.concatenate([attn_c.reshape(n_ctx, ATTN_WIDTH), attn_l.reshape(n_lat, ATTN_WIDTH)], axis=0)
        lru = jnp.concatenate([lru_c.reshape(n_ctx, LRU_WIDTH), lru_l.reshape(n_lat, LRU_WIDTH)], axis=0)
        conv = jnp.concatenate([conv_c.reshape(n_ctx, CONV_WIDTH), conv_l.reshape(n_lat, CONV_WIDTH)], axis=0)
        x1, h2, logits = _postmix_call(attn, lru, conv, x, mod3, l, P, n_ctx, lat_t)

        ids, gw, rank, counts = _route_call(logits, P["b_router"][l], tri)
        group_start, first_blk = _group_tables(counts[0])
        slots = _slots_call(ids, rank, group_start).reshape(-1)
        xs = _dispatch_call(first_blk, slots, h2, n_slots)
        ys = _expert_call(first_blk, xs, P["w_e_gate"], P["w_e_up"], P["w_e_down"], l)
        x = _combine_call(slots, gw, x1, h2, mod3, ys, l, P, n_ctx, lat_t)

    y_prompt = x[:n_ctx].reshape(n_seq_c, ctx_t, D_MODEL)
    y_sample = x[n_ctx:].reshape(n_seq_l, lat_t, D_MODEL)
    return (y_prompt, y_sample, jnp.stack(ks, axis=1), jnp.stack(vs, axis=1), jnp.stack(ss, axis=1))


def kernel(x_prompt, x_sample, c, cache_k, cache_v, state_lru, c_ctx, w_mod, b_mod, norm1_g, w_in, q_norm_g, k_norm_g, lru_conv_w, lru_conv_b, lru_w_a, lru_b_a, lru_w_x, lru_b_x, lru_lambda, cm_dw_w, cm_dw_b, cm_ln_g, cm_ln_b, out_norm_g, w_out, norm2_g, w_router, b_router, w_e_gate, w_e_up, w_e_down, w_s_gate, w_s_up, w_s_down):
    P = {"w_mod": w_mod, "b_mod": b_mod, "norm1_g": norm1_g, "w_in": w_in, "q_norm_g": q_norm_g,
         "k_norm_g": k_norm_g, "lru_conv_w": lru_conv_w, "lru_conv_b": lru_conv_b, "lru_w_a": lru_w_a,
         "lru_b_a": lru_b_a, "lru_w_x": lru_w_x, "lru_b_x": lru_b_x, "lru_lambda": lru_lambda,
         "cm_dw_w": cm_dw_w, "cm_dw_b": cm_dw_b, "cm_ln_g": cm_ln_g, "cm_ln_b": cm_ln_b,
         "out_norm_g": out_norm_g, "w_out": w_out, "norm2_g": norm2_g, "w_router": w_router,
         "b_router": b_router, "w_e_gate": w_e_gate, "w_e_up": w_e_up, "w_e_down": w_e_down,
         "w_s_gate": w_s_gate, "w_s_up": w_s_up, "w_s_down": w_s_down}
    return _trunk(x_prompt, x_sample, c, cache_k, cache_v, state_lru, c_ctx, P)
```

```python
import functools
import math

import jax
import jax.numpy as jnp
from jax import lax
from jax.experimental import pallas as pl
from jax.experimental.pallas import tpu as pltpu

F32 = jnp.float32
BF16 = jnp.bfloat16
I32 = jnp.int32

D_MODEL = 1024
DEPTH = 2
GRID_W = 64
ATTN_WIDTH = 512
LRU_WIDTH = 256
CONV_WIDTH = 256
HEAD_DIM = 64
N_HEADS = 8
N_KV_HEADS = 2
KV_GROUP = N_HEADS // N_KV_HEADS
KV_WIDTH = N_KV_HEADS * HEAD_DIM
ROPE_BASE = 10000.0
LRU_HEADS = 4
LRU_HEAD_DIM = LRU_WIDTH // LRU_HEADS
LRU_CONV = 4
RG_C = 8.0
CM_KERNEL = 31
N_EXPERTS = 256
TOP_K = 8
EXPERT_FF = 256
ROUTED_SCALE = 2.5
EPS = 1e-6
IN_WIDTH = ATTN_WIDTH + 2 * KV_WIDTH + 2 * LRU_WIDTH + 2 * CONV_WIDTH

SUBLANES = 8
LANES = 128
VMEM_LIMIT = 56 * 1024 * 1024

TOKEN_TILE = 256
SCAN_CHUNK = 256
CONV_CHUNK = 128
CONV_HALO = 16
LRU_HALO = 8
EXPERT_BLOCK = 256
EXPERT_IN_SLOTS = 4
COMBINE_TILE = 128
DISPATCH_TILE = 512


def _params(sem):
    return pltpu.CompilerParams(dimension_semantics=sem, vmem_limit_bytes=VMEM_LIMIT)


def _sigmoid(x):
    return 1.0 / (1.0 + jnp.exp(-x))


def _silu(x):
    return x * _sigmoid(x)


def _bdot(a, b):
    return jnp.dot(a.astype(BF16), b.astype(BF16), preferred_element_type=F32)


def _split_dot(a, b_bf16):
    hi = a.astype(BF16)
    lo = (a - hi.astype(F32)).astype(BF16)
    return (jnp.dot(hi, b_bf16, preferred_element_type=F32)
            + jnp.dot(lo, b_bf16, preferred_element_type=F32))


def _rms(x, g):
    return x * lax.rsqrt(jnp.mean(x * x, axis=-1, keepdims=True) + EPS) * g


MOD_COLS = 1536


def _mod_kernel(c_ref, w_ref, b_ref, o_ref):
    a = _silu(c_ref[...])
    o_ref[...] = jnp.dot(a, w_ref[...], preferred_element_type=F32,
                         precision=lax.Precision.HIGHEST) + b_ref[...]


def _mod_call(cond, w_mod, b_mod):
    n_cond = cond.shape[0]
    width = 6 * D_MODEL
    return pl.pallas_call(
        _mod_kernel,
        out_shape=jax.ShapeDtypeStruct((DEPTH, n_cond, width), F32),
        grid=(DEPTH, width // MOD_COLS),
        in_specs=[
            pl.BlockSpec((n_cond, D_MODEL), lambda l, j: (0, 0)),
            pl.BlockSpec((None, D_MODEL, MOD_COLS), lambda l, j: (l, 0, j)),
            pl.BlockSpec((None, 1, MOD_COLS), lambda l, j: (l, 0, j)),
        ],
        out_specs=pl.BlockSpec((None, n_cond, MOD_COLS), lambda l, j: (l, 0, j)),
        compiler_params=_params(("arbitrary", "arbitrary")),
        name="mod",
    )(cond, w_mod, b_mod.reshape(DEPTH, 1, width))


def _swap_pairs(x):
    n = x.shape[-1]
    lane = lax.broadcasted_iota(I32, x.shape, 1)
    nxt = pltpu.roll(x, n - 1, 1)
    prv = pltpu.roll(x, 1, 1)
    return jnp.where(lane % 2 == 0, nxt, prv)


def _premix_kernel(x_ref, mod_ref, n1g_ref, w_ref, qg_ref, kg_ref, bd_ref, cos_ref, sin_ref,
                   q_ref, kr_ref, vb_ref, kc_ref, vc_ref, u_ref, gate_ref, cv_ref):
    x = x_ref[...]
    sh1 = mod_ref[:, 0:D_MODEL]
    sc1 = mod_ref[:, D_MODEL:2 * D_MODEL]
    h = _rms(x, n1g_ref[...]) * (1.0 + sc1) + sh1
    z = jnp.dot(h.astype(BF16), w_ref[...], preferred_element_type=F32)
    o = 0
    q = z[:, o:o + ATTN_WIDTH]; o += ATTN_WIDTH
    k = z[:, o:o + KV_WIDTH]; o += KV_WIDTH
    v = z[:, o:o + KV_WIDTH]; o += KV_WIDTH
    u_ref[...] = z[:, o:o + LRU_WIDTH]; o += LRU_WIDTH
    gate_ref[...] = z[:, o:o + LRU_WIDTH]; o += LRU_WIDTH
    cv_ref[...] = z[:, o:o + 2 * CONV_WIDTH]

    bd = bd_ref[...]
    inv_hd = 1.0 / HEAD_DIM
    q_ms = _split_dot(q * q, bd) * inv_hd
    k_ms = _split_dot(k * k, bd[:KV_WIDTH, :KV_WIDTH]) * inv_hd
    qn = q * lax.rsqrt(q_ms + EPS) * qg_ref[...]
    kn = k * lax.rsqrt(k_ms + EPS) * kg_ref[...]
    kc_ref[...] = kn
    vc_ref[...] = v
    ones = jnp.ones((v.shape[0], HEAD_DIM), F32)
    vb_ref[...] = jnp.concatenate([v[:, :HEAD_DIM], ones, v[:, HEAD_DIM:], ones], axis=1).astype(BF16)

    cos = cos_ref[...]
    sin = sin_ref[...]
    reps = ATTN_WIDTH // cos.shape[-1]
    cos_q = jnp.concatenate([cos] * reps, axis=1)
    sin_q = jnp.concatenate([sin] * reps, axis=1)
    qr = qn * cos_q + _swap_pairs(qn) * sin_q
    kr = kn * cos + _swap_pairs(kn) * sin
    q_ref[...] = (qr * (HEAD_DIM ** -0.5 * math.log2(math.e))).astype(BF16)
    kr_ref[...] = kr.astype(BF16)


def _premix_call(x, mod3, l, P, rope_cos, rope_sin, n_ctx, lat_t):
    n = x.shape[0]
    tm = TOKEN_TILE
    n_ctx_tiles = n_ctx // tm
    tiles_per_seq = lat_t // tm

    def mod_idx(i):
        return (jnp.where(i < n_ctx_tiles, 0, 1 + (i - n_ctx_tiles) // tiles_per_seq), 0, 0)

    def rope_idx(i):
        return (jnp.where(i < n_ctx_tiles, tiles_per_seq, (i - n_ctx_tiles) % tiles_per_seq), 0)

    const = lambda i: (0, 0)
    row = lambda i: (i, 0)
    outs = pl.pallas_call(
        _premix_kernel,
        out_shape=(
            jax.ShapeDtypeStruct((n, ATTN_WIDTH), BF16),
            jax.ShapeDtypeStruct((n, KV_WIDTH), BF16),
            jax.ShapeDtypeStruct((n, 2 * KV_WIDTH), BF16),
            jax.ShapeDtypeStruct((n, KV_WIDTH), F32),
            jax.ShapeDtypeStruct((n, KV_WIDTH), F32),
            jax.ShapeDtypeStruct((n, LRU_WIDTH), F32),
            jax.ShapeDtypeStruct((n, LRU_WIDTH), F32),
            jax.ShapeDtypeStruct((n, 2 * CONV_WIDTH), F32),
        ),
        grid=(n // tm,),
        in_specs=[
            pl.BlockSpec((tm, D_MODEL), row),
            pl.BlockSpec((None, 1, 6 * D_MODEL), mod_idx),
            pl.BlockSpec((1, D_MODEL), const),
            pl.BlockSpec((D_MODEL, IN_WIDTH), const),
            pl.BlockSpec((1, ATTN_WIDTH), const),
            pl.BlockSpec((1, KV_WIDTH), const),
            pl.BlockSpec((ATTN_WIDTH, ATTN_WIDTH), const),
            pl.BlockSpec((tm, LANES), rope_idx),
            pl.BlockSpec((tm, LANES), rope_idx),
        ],
        out_specs=(
            pl.BlockSpec((tm, ATTN_WIDTH), row),
            pl.BlockSpec((tm, KV_WIDTH), row),
            pl.BlockSpec((tm, 2 * KV_WIDTH), row),
            pl.BlockSpec((tm, KV_WIDTH), row),
            pl.BlockSpec((tm, KV_WIDTH), row),
            pl.BlockSpec((tm, LRU_WIDTH), row),
            pl.BlockSpec((tm, LRU_WIDTH), row),
            pl.BlockSpec((tm, 2 * CONV_WIDTH), row),
        ),
        compiler_params=_params(("parallel",)),
        name="premix",
    )(x, mod3, P["norm1_g"][l][None, :], P["w_in_bf16"][l], P["q_norm_g_t"][l][None, :],
      P["k_norm_g_t"][l][None, :], P["head_blockdiag"], rope_cos, rope_sin)
    return outs


def _attn_kernel(q_ref, k_ref, v_ref, o_ref):
    outs = []
    for kh in range(N_KV_HEADS):
        k = k_ref[:, kh * HEAD_DIM:(kh + 1) * HEAD_DIM]
        v = v_ref[:, kh * 2 * HEAD_DIM:(kh + 1) * 2 * HEAD_DIM]
        for g in range(KV_GROUP):
            hd = kh * KV_GROUP + g
            q = q_ref[:, hd * HEAD_DIM:(hd + 1) * HEAD_DIM]
            s = lax.dot_general(q, k, (((1,), (1,)), ((), ())), preferred_element_type=F32)
            m = jnp.max(s, axis=-1, keepdims=True)
            p = jnp.exp2(s - m).astype(BF16)
            o = jnp.dot(p, v, preferred_element_type=F32)
            outs.append(o[:, :HEAD_DIM] / o[:, HEAD_DIM:HEAD_DIM + 1])
    o_ref[...] = jnp.concatenate(outs, axis=1)


def _attn_call(q3, k3, v3, tq, b, q_off, kv_off):
    s = q3.shape[1]
    t = k3.shape[1]
    return pl.pallas_call(
        _attn_kernel,
        out_shape=jax.ShapeDtypeStruct((b, s, ATTN_WIDTH), F32),
        grid=(b, s // tq),
        in_specs=[
            pl.BlockSpec((None, tq, ATTN_WIDTH), lambda i, j: (i + q_off, j, 0)),
            pl.BlockSpec((None, t, KV_WIDTH), lambda i, j: (i + kv_off, 0, 0)),
            pl.BlockSpec((None, t, 2 * KV_WIDTH), lambda i, j: (i + kv_off, 0, 0)),
        ],
        out_specs=pl.BlockSpec((None, tq, ATTN_WIDTH), lambda i, j: (i, j, 0)),
        compiler_params=_params(("parallel", "parallel")),
        name="attention",
    )(q3, k3, v3)


def _chunk_scan(a, b, reverse):
    n = a.shape[0]
    row = lax.broadcasted_iota(I32, a.shape, 0)
    d = 1
    while d < n:
        if reverse:
            a_s = pltpu.roll(a, n - d, 0)
            b_s = pltpu.roll(b, n - d, 0)
            ok = row < n - d
        else:
            a_s = pltpu.roll(a, d, 0)
            b_s = pltpu.roll(b, d, 0)
            ok = row >= d
        b = jnp.where(ok, a * b_s + b, b)
        a = jnp.where(ok, a * a_s, a)
        d *= 2
    return a, b


def _gelu_tanh(x):
    return 0.5 * x * (1.0 + jnp.tanh(math.sqrt(2.0 / math.pi) * (x + 0.044715 * (x * x * x))))


def _lru_kernel(u_ref, gate_ref, cw_ref, cb_ref, wg_ref, bg_ref, lam_ref, h0_ref,
                y_ref, fin_ref, upad, xc_scr, fwd_scr):
    t_len = u_ref.shape[0]
    ch = SCAN_CHUNK
    n_chunks = t_len // ch
    w = LRU_WIDTH
    zeros_halo = jnp.zeros((LRU_HALO, w), F32)
    upad[0:LRU_HALO, :] = zeros_halo
    upad[t_len + LRU_HALO:t_len + 2 * LRU_HALO, :] = zeros_halo

    def fill(c, carry):
        r0 = pl.multiple_of(c * ch, ch)
        upad[pl.ds(r0 + LRU_HALO, ch), :] = u_ref[pl.ds(r0, ch), :]
        return carry

    lax.fori_loop(0, n_chunks, fill, 0)

    lam = lam_ref[...]
    nlam = -lam
    softplus = jnp.maximum(nlam, 0.0) + jnp.log(1.0 + jnp.exp(-jnp.abs(nlam)))
    decay = -RG_C * softplus
    cw = cw_ref[...]
    cb = cb_ref[...]

    def gates(xc, d):
        z = jnp.dot(xc.astype(BF16), wg_ref[:, 2 * d * w:2 * (d + 1) * w],
                    preferred_element_type=F32) + bg_ref[:, 2 * d * w:2 * (d + 1) * w]
        r = _sigmoid(z[:, :w])
        i = _sigmoid(z[:, w:])
        log_a = decay[d:d + 1, :] * r
        a = jnp.exp(log_a)
        b = jnp.sqrt(-jnp.tanh(log_a) * (a * a + 1.0)) * (i * xc)
        return a, b

    def fwd(c, h):
        r0 = pl.multiple_of(c * ch, ch)
        win = upad[pl.ds(r0, ch + 2 * LRU_HALO), :]
        xc = cb
        for j in range(LRU_CONV):
            s0 = LRU_HALO - (LRU_CONV - 1) // 2 + j
            xc = xc + cw[j:j + 1, :] * win[s0:s0 + ch, :]
        xc_scr[pl.ds(r0, ch), :] = xc
        a, b = gates(xc, 0)
        a_cum, b_cum = _chunk_scan(a, b, reverse=False)
        hs = a_cum * h + b_cum
        fwd_scr[pl.ds(r0, ch), :] = hs
        return hs[ch - 1:ch, :]

    h_f = lax.fori_loop(0, n_chunks, fwd, h0_ref[0:1, :])

    def bwd(ci, h):
        c = n_chunks - 1 - ci
        r0 = pl.multiple_of(c * ch, ch)
        xc = xc_scr[pl.ds(r0, ch), :]
        a, b = gates(xc, 1)
        a_cum, b_cum = _chunk_scan(a, b, reverse=True)
        hs = a_cum * h + b_cum
        y_ref[pl.ds(r0, ch), :] = (fwd_scr[pl.ds(r0, ch), :] + hs) * _gelu_tanh(gate_ref[pl.ds(r0, ch), :])
        return hs[0:1, :]

    h_b = lax.fori_loop(0, n_chunks, bwd, h0_ref[1:2, :])
    fin_ref[0:1, :] = h_f
    fin_ref[1:2, :] = h_b


def _lru_call(u3, gate3, h0, l, P, b, seq_off):
    _, t, w = u3.shape
    const = lambda i: (0, 0)
    seq = lambda i: (i, 0, 0)
    seq_in = lambda i: (i + seq_off, 0, 0)
    return pl.pallas_call(
        _lru_kernel,
        out_shape=(jax.ShapeDtypeStruct((b, t, w), F32),
                   jax.ShapeDtypeStruct((b, 2, w), F32)),
        grid=(b,),
        in_specs=[
            pl.BlockSpec((None, t, w), seq_in),
            pl.BlockSpec((None, t, w), seq_in),
            pl.BlockSpec((LRU_CONV, w), const),
            pl.BlockSpec((1, w), const),
            pl.BlockSpec((w, 4 * w), const),
            pl.BlockSpec((1, 4 * w), const),
            pl.BlockSpec((2, w), const),
            pl.BlockSpec((None, 2, w), seq),
        ],
        out_specs=(pl.BlockSpec((None, t, w), seq),
                   pl.BlockSpec((None, 2, w), seq)),
        scratch_shapes=[pltpu.VMEM((t + 2 * LRU_HALO, w), F32),
                        pltpu.VMEM((t, w), F32),
                        pltpu.VMEM((t, w), F32)],
        compiler_params=_params(("parallel",)),
        name="rglru",
    )(u3, gate3, P["lru_conv_w"][l], P["lru_conv_b"][l][None, :], P["lru_gate_w"][l],
      P["lru_gate_b"][l][None, :], P["lru_lambda"][l], h0)


def _convmod_kernel(cv_ref, w_ref, b_ref, g_ref, beta_ref, o_ref, hpad):
    t_len = cv_ref.shape[0]
    ch = CONV_CHUNK
    n_chunks = t_len // ch
    w = CONV_WIDTH
    zeros_halo = jnp.zeros((CONV_HALO, w), F32)
    hpad[0:CONV_HALO, :] = zeros_halo
    hpad[t_len + CONV_HALO:t_len + 2 * CONV_HALO, :] = zeros_halo

    def glu(c, carry):
        r0 = pl.multiple_of(c * ch, ch)
        blk = cv_ref[pl.ds(r0, ch), :]
        hpad[pl.ds(r0 + CONV_HALO, ch), :] = blk[:, :w] * _sigmoid(blk[:, w:])
        return carry

    lax.fori_loop(0, n_chunks, glu, 0)

    taps = w_ref[...]
    bias = b_ref[...]
    gamma = g_ref[...]
    beta = beta_ref[...]

    def conv(c, carry):
        r0 = pl.multiple_of(c * ch, ch)
        win = hpad[pl.ds(r0, ch + 2 * CONV_HALO), :]
        acc = bias
        for j in range(CM_KERNEL):
            s0 = CONV_HALO - CM_KERNEL // 2 + j
            acc = acc + taps[j:j + 1, :] * win[s0:s0 + ch, :]
        mu = jnp.mean(acc, axis=-1, keepdims=True)
        cen = acc - mu
        var = jnp.mean(cen * cen, axis=-1, keepdims=True)
        y = cen * lax.rsqrt(var + EPS) * gamma + beta
        o_ref[pl.ds(r0, ch), :] = _silu(y)
        return carry

    lax.fori_loop(0, n_chunks, conv, 0)


def _convmod_call(cv3, l, P, b, seq_off):
    t = cv3.shape[1]
    w = CONV_WIDTH
    const = lambda i: (0, 0)
    seq = lambda i: (i, 0, 0)
    return pl.pallas_call(
        _convmod_kernel,
        out_shape=jax.ShapeDtypeStruct((b, t, w), F32),
        grid=(b,),
        in_specs=[
            pl.BlockSpec((None, t, 2 * w), lambda i: (i + seq_off, 0, 0)),
            pl.BlockSpec((CM_KERNEL, w), const),
            pl.BlockSpec((1, w), const),
            pl.BlockSpec((1, w), const),
            pl.BlockSpec((1, w), const),
        ],
        out_specs=pl.BlockSpec((None, t, w), seq),
        scratch_shapes=[pltpu.VMEM((t + 2 * CONV_HALO, w), F32)],
        compiler_params=_params(("parallel",)),
        name="convmod",
    )(cv3, P["cm_dw_w"][l], P["cm_dw_b"][l][None, :], P["cm_ln_g"][l][None, :], P["cm_ln_b"][l][None, :])


def _postmix_kernel(n_ctx_tiles, attn_c_ref, attn_l_ref, lru_c_ref, lru_l_ref, conv_c_ref, conv_l_ref,
                    x_ref, mod_ref, og_ref, wo_ref, n2g_ref, wr_hi_ref, wr_lo_ref,
                    x1_ref, h2_ref, lg_ref):
    is_ctx = pl.program_id(0) < n_ctx_tiles
    og = og_ref[...]
    a0, a1, a2 = ATTN_WIDTH, ATTN_WIDTH + LRU_WIDTH, D_MODEL
    y = jnp.concatenate([_rms(jnp.where(is_ctx, attn_c_ref[...], attn_l_ref[...]), og[:, :a0]),
                         _rms(jnp.where(is_ctx, lru_c_ref[...], lru_l_ref[...]), og[:, a0:a1]),
                         _rms(jnp.where(is_ctx, conv_c_ref[...], conv_l_ref[...]), og[:, a1:a2])], axis=1)
    y = jnp.dot(y.astype(BF16), wo_ref[...], preferred_element_type=F32)
    g1 = mod_ref[:, 2 * D_MODEL:3 * D_MODEL]
    sh2 = mod_ref[:, 3 * D_MODEL:4 * D_MODEL]
    sc2 = mod_ref[:, 4 * D_MODEL:5 * D_MODEL]
    x1 = x_ref[...] + g1 * y
    x1_ref[...] = x1
    h2 = _rms(x1, n2g_ref[...]) * (1.0 + sc2) + sh2
    h2_ref[...] = h2
    hi = h2.astype(BF16)
    lo = (h2 - hi.astype(F32)).astype(BF16)
    w_hi = wr_hi_ref[...]
    lg_ref[...] = (jnp.dot(hi, w_hi, preferred_element_type=F32)
                   + jnp.dot(lo, w_hi, preferred_element_type=F32)
                   + jnp.dot(hi, wr_lo_ref[...], preferred_element_type=F32))


def _postmix_call(attn_c, attn_l, lru_c, lru_l, conv_c, conv_l, x, mod3, l, P, n_ctx, lat_t):
    n = x.shape[0]
    tm = TOKEN_TILE
    n_ctx_tiles = n_ctx // tm
    tiles_per_seq = lat_t // tm

    def mod_idx(i):
        return (jnp.where(i < n_ctx_tiles, 0, 1 + (i - n_ctx_tiles) // tiles_per_seq), 0, 0)

    const = lambda i: (0, 0)
    row = lambda i: (i, 0)
    row_c = lambda i: (jnp.minimum(i, n_ctx_tiles - 1), 0)
    row_l = lambda i: (jnp.maximum(i - n_ctx_tiles, 0), 0)
    return pl.pallas_call(
        functools.partial(_postmix_kernel, n_ctx_tiles),
        out_shape=(jax.ShapeDtypeStruct((n, D_MODEL), F32),
                   jax.ShapeDtypeStruct((n, D_MODEL), F32),
                   jax.ShapeDtypeStruct((n, N_EXPERTS), F32)),
        grid=(n // tm,),
        in_specs=[
            pl.BlockSpec((tm, ATTN_WIDTH), row_c),
            pl.BlockSpec((tm, ATTN_WIDTH), row_l),
            pl.BlockSpec((tm, LRU_WIDTH), row_c),
            pl.BlockSpec((tm, LRU_WIDTH), row_l),
            pl.BlockSpec((tm, CONV_WIDTH), row_c),
            pl.BlockSpec((tm, CONV_WIDTH), row_l),
            pl.BlockSpec((tm, D_MODEL), row),
            pl.BlockSpec((None, 1, 6 * D_MODEL), mod_idx),
            pl.BlockSpec((1, D_MODEL), const),
            pl.BlockSpec((D_MODEL, D_MODEL), const),
            pl.BlockSpec((1, D_MODEL), const),
            pl.BlockSpec((D_MODEL, N_EXPERTS), const),
            pl.BlockSpec((D_MODEL, N_EXPERTS), const),
        ],
        out_specs=(pl.BlockSpec((tm, D_MODEL), row),
                   pl.BlockSpec((tm, D_MODEL), row),
                   pl.BlockSpec((tm, N_EXPERTS), row)),
        compiler_params=_params(("parallel",)),
        name="postmix",
    )(attn_c, attn_l, lru_c, lru_l, conv_c, conv_l, x, mod3, P["out_norm_g"][l][None, :], P["w_out_bf16"][l],
      P["norm2_g"][l][None, :], P["w_router_hi"][l], P["w_router_lo"][l])


def _route_kernel(lg_ref, br_ref, tri_ref, ids_ref, gw_ref, rank_ref, cnt_ref, carry):
    i = pl.program_id(0)

    @pl.when(i == 0)
    def _():
        carry[...] = jnp.zeros_like(carry)

    scores = _sigmoid(lg_ref[...])
    sel = scores + br_ref[...]
    tm = scores.shape[0]
    lane = lax.broadcasted_iota(I32, (tm, N_EXPERTS), 1).astype(F32)
    slot_lane = lax.broadcasted_iota(I32, (tm, LANES), 1)
    ids_acc = jnp.zeros((tm, LANES), F32)
    gw_acc = jnp.zeros((tm, LANES), F32)
    hot = jnp.zeros((tm, N_EXPERTS), F32)
    picked = []
    for k in range(TOP_K):
        m = jnp.max(sel, axis=-1, keepdims=True)
        idx = jnp.min(jnp.where(sel == m, lane, float(N_EXPERTS)), axis=-1, keepdims=True)
        one = lane == idx
        g = jnp.sum(jnp.where(one, scores, 0.0), axis=-1, keepdims=True)
        sel = jnp.where(one, -jnp.inf, sel)
        hot = jnp.where(one, 1.0, hot)
        ids_acc = jnp.where(slot_lane == k, idx, ids_acc)
        gw_acc = jnp.where(slot_lane == k, g, gw_acc)
        picked.append(idx)
    denom = jnp.sum(gw_acc, axis=-1, keepdims=True)
    gw_acc = ROUTED_SCALE * gw_acc / denom

    before = jnp.dot(tri_ref[...], hot.astype(BF16), preferred_element_type=F32) + carry[0:1, :]
    rank_acc = jnp.zeros((tm, LANES), F32)
    for k in range(TOP_K):
        r = jnp.sum(jnp.where(lane == picked[k], before, 0.0), axis=-1, keepdims=True)
        rank_acc = jnp.where(slot_lane == k, r, rank_acc)
    carry[0:1, :] = carry[0:1, :] + jnp.sum(hot, axis=0, keepdims=True)
    cnt_ref[...] = carry[...]
    ids_ref[...] = ids_acc[:, :TOP_K].astype(I32)
    gw_ref[...] = gw_acc[:, :TOP_K]
    rank_ref[...] = rank_acc[:, :TOP_K].astype(I32)


def _route_call(logits, b_router_l, tri):
    n = logits.shape[0]
    tm = TOKEN_TILE
    const = lambda i: (0, 0)
    row = lambda i: (i, 0)
    return pl.pallas_call(
        _route_kernel,
        out_shape=(jax.ShapeDtypeStruct((n, TOP_K), I32),
                   jax.ShapeDtypeStruct((n, TOP_K), F32),
                   jax.ShapeDtypeStruct((n, TOP_K), I32),
                   jax.ShapeDtypeStruct((SUBLANES, N_EXPERTS), F32)),
        grid=(n // tm,),
        in_specs=[
            pl.BlockSpec((tm, N_EXPERTS), row),
            pl.BlockSpec((1, N_EXPERTS), const),
            pl.BlockSpec((tm, tm), const),
        ],
        out_specs=(pl.BlockSpec((tm, TOP_K), row),
                   pl.BlockSpec((tm, TOP_K), row),
                   pl.BlockSpec((tm, TOP_K), row),
                   pl.BlockSpec((SUBLANES, N_EXPERTS), const)),
        scratch_shapes=[pltpu.VMEM((SUBLANES, N_EXPERTS), F32)],
        compiler_params=_params(("arbitrary",)),
        name="route",
    )(logits, b_router_l[None, :], tri)


def _slots_kernel(ids_ref, rank_ref, start_ref, slots_ref):
    ids = ids_ref[...].astype(F32)
    tm = ids.shape[0]
    lane = lax.broadcasted_iota(I32, (tm, N_EXPERTS), 1).astype(F32)
    slot_lane = lax.broadcasted_iota(I32, (tm, TOP_K), 1)
    start = start_ref[...]
    acc = jnp.zeros((tm, TOP_K), F32)
    for k in range(TOP_K):
        s = jnp.sum(jnp.where(lane == ids[:, k:k + 1], start, 0.0), axis=-1, keepdims=True)
        acc = jnp.where(slot_lane == k, s, acc)
    slots_ref[...] = acc.astype(I32) + rank_ref[...]


def _slots_call(ids, rank, group_start):
    n = ids.shape[0]
    tm = TOKEN_TILE
    row = lambda i: (i, 0)
    return pl.pallas_call(
        _slots_kernel,
        out_shape=jax.ShapeDtypeStruct((n, TOP_K), I32),
        grid=(n // tm,),
        in_specs=[pl.BlockSpec((tm, TOP_K), row), pl.BlockSpec((tm, TOP_K), row),
                  pl.BlockSpec((1, N_EXPERTS), lambda i: (0, 0))],
        out_specs=pl.BlockSpec((tm, TOP_K), row),
        compiler_params=_params(("parallel",)),
        name="slots",
    )(ids, rank, group_start)


def _wait_rows(src_ref, dst_ref, sem, n_waits):
    for _ in range(n_waits):
        pltpu.make_async_copy(src_ref, dst_ref, sem).wait()


def _dispatch_kernel(first_ref, slots_ref, h_ref, xs_ref, zeros, sem_rows, sem_zero):
    n_tok = h_ref.shape[0]
    bm = EXPERT_BLOCK
    n_blocks = xs_ref.shape[0] // bm
    nused = first_ref[N_EXPERTS]

    def zero_copy(b):
        return pltpu.make_async_copy(zeros, xs_ref.at[pl.ds(pl.multiple_of(b * bm, bm), bm), :], sem_zero)

    @pl.when(pl.program_id(0) == 0)
    def _():
        zeros[...] = jnp.zeros_like(zeros)

        def group(start):
            def body(e, c):
                @pl.when(first_ref[e + 1] > first_ref[e])
                def _():
                    cp = zero_copy(first_ref[e + 1] - 1)
                    cp.start() if start else cp.wait()
                return c
            lax.fori_loop(0, N_EXPERTS, body, 0)

        def tail(start):
            def body(b, c):
                cp = zero_copy(b)
                cp.start() if start else cp.wait()
                return c
            lax.fori_loop(nused, n_blocks, body, 0)

        group(True)
        tail(True)
        group(False)
        tail(False)

    def issue(t, c):
        for k in range(TOP_K):
            slot = slots_ref[t * TOP_K + k]
            pltpu.make_async_copy(h_ref.at[pl.ds(t, 1), :], xs_ref.at[pl.ds(slot, 1), :],
                                  sem_rows).start(priority=k % 2)
        return c

    lax.fori_loop(0, n_tok, issue, 0)
    _wait_rows(h_ref, xs_ref.at[pl.ds(0, n_tok), :], sem_rows, TOP_K)


def _dispatch_call(first_blk, slots_flat, h2, n_slots):
    n = h2.shape[0]
    tm = DISPATCH_TILE
    grid_spec = pltpu.PrefetchScalarGridSpec(
        num_scalar_prefetch=1,
        grid=(n // tm,),
        in_specs=[
            pl.BlockSpec((tm * TOP_K,), lambda i, first: (i,), memory_space=pltpu.SMEM),
            pl.BlockSpec((tm, D_MODEL), lambda i, first: (i, 0)),
        ],
        out_specs=pl.BlockSpec(memory_space=pl.ANY),
        scratch_shapes=[pltpu.VMEM((EXPERT_BLOCK, D_MODEL), F32),
                        pltpu.SemaphoreType.DMA,
                        pltpu.SemaphoreType.DMA],
    )
    return pl.pallas_call(
        _dispatch_kernel,
        out_shape=jax.ShapeDtypeStruct((n_slots, D_MODEL), F32),
        grid_spec=grid_spec,
        compiler_params=_params(("arbitrary",)),
        name="dispatch",
    )(first_blk, slots_flat, h2)


def _expert_kernel(first_ref, xs_ref, wg_ref, wu_ref, wd_ref, ys_ref,
                   xbuf, ybuf, wg_bf, wu_bf, wd_bf, sem_in, sem_out):
    e = pl.program_id(0)
    bm = EXPERT_BLOCK
    n_blocks = ys_ref.shape[0] // bm
    lo = first_ref[e]
    hi = first_ref[e + 1]
    nused = first_ref[N_EXPERTS]

    def rows(b):
        return pl.ds(pl.multiple_of(b * bm, bm), bm)

    def load(b, s):
        return pltpu.make_async_copy(xs_ref.at[rows(b), :], xbuf.at[s], sem_in.at[s])

    def store(b, s):
        return pltpu.make_async_copy(ybuf.at[s], ys_ref.at[rows(b), :], sem_out.at[s])

    @pl.when(e == 0)
    def _():
        for j in range(EXPERT_IN_SLOTS - 1):
            @pl.when(j < nused)
            def _():
                load(j, j).start()

    @pl.when(hi > lo)
    def _():
        wg_bf[...] = wg_ref[...].astype(BF16)
        wu_bf[...] = wu_ref[...].astype(BF16)
        wd_bf[...] = wd_ref[...].astype(BF16)

    def block(b, c):
        load(b, b % EXPERT_IN_SLOTS).wait()
        ahead = b + EXPERT_IN_SLOTS - 1

        @pl.when(ahead < nused)
        def _():
            load(ahead, ahead % EXPERT_IN_SLOTS).start()

        s = b % 2

        @pl.when(b >= 2)
        def _():
            store(b - 2, s).wait()

        x = xbuf[b % EXPERT_IN_SLOTS].astype(BF16)
        g = jnp.dot(x, wg_bf[...], preferred_element_type=F32)
        u = jnp.dot(x, wu_bf[...], preferred_element_type=F32)
        h = (_silu(g) * u).astype(BF16)
        ybuf[s] = jnp.dot(h, wd_bf[...], preferred_element_type=F32)
        store(b, s).start()
        return c

    lax.fori_loop(lo, hi, block, 0)

    @pl.when(e == pl.num_programs(0) - 1)
    def _():
        @pl.when(nused >= 2)
        def _():
            store(nused - 2, nused % 2).wait()

        @pl.when(nused >= 1)
        def _():
            store(nused - 1, (nused - 1) % 2).wait()

        ybuf[0] = jnp.zeros((bm, D_MODEL), F32)

        def fill(b, c):
            store(b, 0).start()
            return c

        def drain(b, c):
            store(b, 0).wait()
            return c

        lax.fori_loop(nused, n_blocks, fill, 0)
        lax.fori_loop(nused, n_blocks, drain, 0)


def _expert_call(first_blk, xs, w_e_gate, w_e_up, w_e_down, l):
    n_slots = xs.shape[0]
    bm = EXPERT_BLOCK

    def w_idx(e, first_ref):
        return (l, e, 0, 0)

    grid_spec = pltpu.PrefetchScalarGridSpec(
        num_scalar_prefetch=1,
        grid=(N_EXPERTS,),
        in_specs=[
            pl.BlockSpec(memory_space=pl.ANY),
            pl.BlockSpec((None, None, D_MODEL, EXPERT_FF), w_idx),
            pl.BlockSpec((None, None, D_MODEL, EXPERT_FF), w_idx),
            pl.BlockSpec((None, None, EXPERT_FF, D_MODEL), w_idx),
        ],
        out_specs=pl.BlockSpec(memory_space=pl.ANY),
        scratch_shapes=[pltpu.VMEM((EXPERT_IN_SLOTS, bm, D_MODEL), F32),
                        pltpu.VMEM((2, bm, D_MODEL), F32),
                        pltpu.VMEM((D_MODEL, EXPERT_FF), BF16),
                        pltpu.VMEM((D_MODEL, EXPERT_FF), BF16),
                        pltpu.VMEM((EXPERT_FF, D_MODEL), BF16),
                        pltpu.SemaphoreType.DMA((EXPERT_IN_SLOTS,)),
                        pltpu.SemaphoreType.DMA((2,))],
    )
    return pl.pallas_call(
        _expert_kernel,
        out_shape=jax.ShapeDtypeStruct((n_slots, D_MODEL), F32),
        grid_spec=grid_spec,
        compiler_params=_params(("arbitrary",)),
        name="experts",
    )(first_blk, xs, w_e_gate, w_e_up, w_e_down)


def _combine_kernel(slots_ref, slots_next_ref, gw_ref, x1_ref, h2_ref, mod_ref, wsg_ref, wsu_ref, wsd_ref, ys_ref,
                    o_ref, buf, sems):
    i = pl.program_id(0)
    n_tok = x1_ref.shape[0]
    half = i % 2

    def issue(idx_ref, dst_half):
        def body(t, c):
            for k in range(TOP_K):
                slot = idx_ref[t * TOP_K + k]
                pltpu.make_async_copy(ys_ref.at[pl.ds(slot, 1), :], buf.at[dst_half, k, pl.ds(t, 1), :],
                                      sems.at[dst_half]).start(priority=k % 2)
            return c
        lax.fori_loop(0, n_tok, body, 0)

    @pl.when(i == 0)
    def _():
        issue(slots_ref, 0)

    has_next = i + 1 < pl.num_programs(0)

    @pl.when(jnp.logical_and(has_next, half == 0))
    def _():
        issue(slots_next_ref, 1)

    @pl.when(jnp.logical_and(has_next, half == 1))
    def _():
        issue(slots_next_ref, 0)

    h = h2_ref[...].astype(BF16)
    hid = _silu(jnp.dot(h, wsg_ref[...], preferred_element_type=F32)) * jnp.dot(
        h, wsu_ref[...], preferred_element_type=F32)
    acc = jnp.dot(hid.astype(BF16), wsd_ref[...], preferred_element_type=F32)

    for k in range(TOP_K):
        pltpu.make_async_copy(ys_ref.at[pl.ds(0, n_tok), :], buf.at[half, k], sems.at[half]).wait()
    gw = gw_ref[...]
    for k in range(TOP_K):
        acc = acc + gw[:, k:k + 1] * buf[half, k]
    g2 = mod_ref[:, 5 * D_MODEL:6 * D_MODEL]
    o_ref[...] = x1_ref[...] + g2 * acc


def _combine_call(slots_flat, gw, x1, h2, mod3, ys, l, P, n_ctx, lat_t):
    n = x1.shape[0]
    tm = COMBINE_TILE
    n_ctx_tiles = n_ctx // tm
    tiles_per_seq = lat_t // tm

    def mod_idx(i):
        return (jnp.where(i < n_ctx_tiles, 0, 1 + (i - n_ctx_tiles) // tiles_per_seq), 0, 0)

    const = lambda i: (0, 0)
    row = lambda i: (i, 0)
    n_tiles = n // tm
    return pl.pallas_call(
        _combine_kernel,
        out_shape=jax.ShapeDtypeStruct((n, D_MODEL), F32),
        grid=(n_tiles,),
        in_specs=[
            pl.BlockSpec((tm * TOP_K,), lambda i: (i,), memory_space=pltpu.SMEM),
            pl.BlockSpec((tm * TOP_K,), lambda i: (jnp.minimum(i + 1, n_tiles - 1),), memory_space=pltpu.SMEM),
            pl.BlockSpec((tm, TOP_K), row),
            pl.BlockSpec((tm, D_MODEL), row),
            pl.BlockSpec((tm, D_MODEL), row),
            pl.BlockSpec((None, 1, 6 * D_MODEL), mod_idx),
            pl.BlockSpec((D_MODEL, EXPERT_FF), const),
            pl.BlockSpec((D_MODEL, EXPERT_FF), const),
            pl.BlockSpec((EXPERT_FF, D_MODEL), const),
            pl.BlockSpec(memory_space=pl.ANY),
        ],
        out_specs=pl.BlockSpec((tm, D_MODEL), row),
        scratch_shapes=[pltpu.VMEM((2, TOP_K, tm, D_MODEL), F32),
                        pltpu.SemaphoreType.DMA((2,))],
        compiler_params=_params(("arbitrary",)),
        name="combine",
    )(slots_flat, slots_flat, gw, x1, h2, mod3, P["w_s_gate_bf16"][l], P["w_s_up_bf16"][l],
      P["w_s_down_bf16"][l], ys)


def _rope_tables(lat_t):
    rows = lat_t // GRID_W
    row = jnp.repeat(jnp.arange(rows, dtype=F32), GRID_W)
    col = jnp.tile(jnp.arange(GRID_W, dtype=F32), rows)
    n_freq = HEAD_DIM // 4
    inv = jnp.power(ROPE_BASE, -jnp.arange(n_freq, dtype=F32) / n_freq)
    ang = jnp.concatenate([row[:, None] * inv, col[:, None] * inv], axis=-1)
    cos = jnp.repeat(jnp.cos(ang), 2, axis=-1)
    sign = jnp.tile(jnp.array([-1.0, 1.0], F32), HEAD_DIM // 2)
    sin = jnp.repeat(jnp.sin(ang), 2, axis=-1) * sign
    cos = jnp.concatenate([cos, jnp.ones((TOKEN_TILE, HEAD_DIM), F32)], axis=0)
    sin = jnp.concatenate([sin, jnp.zeros((TOKEN_TILE, HEAD_DIM), F32)], axis=0)
    reps = LANES // HEAD_DIM
    return jnp.tile(cos, (1, reps)), jnp.tile(sin, (1, reps))


def _block_diag_gates(w_a, w_x):
    def dense(w):
        eye = jnp.eye(LRU_HEADS, dtype=w.dtype)
        return jnp.einsum("ncd,nm->ncmd", w, eye).reshape(LRU_WIDTH, LRU_WIDTH)
    return jnp.concatenate([dense(w_a[0]), dense(w_x[0]), dense(w_a[1]), dense(w_x[1])], axis=1)


def _prepare(P):
    Q = dict(P)
    Q["w_in_bf16"] = P["w_in"].astype(BF16)
    Q["w_out_bf16"] = P["w_out"].astype(BF16)
    Q["q_norm_g_t"] = jnp.tile(P["q_norm_g"], (1, N_HEADS))
    Q["k_norm_g_t"] = jnp.tile(P["k_norm_g"], (1, N_KV_HEADS))
    head = jnp.arange(ATTN_WIDTH) // HEAD_DIM
    Q["head_blockdiag"] = (head[:, None] == head[None, :]).astype(BF16)
    Q["lru_gate_w"] = jnp.stack([_block_diag_gates(P["lru_w_a"][l], P["lru_w_x"][l])
                                 for l in range(DEPTH)]).astype(BF16)
    Q["lru_gate_b"] = jnp.concatenate([P["lru_b_a"][:, 0], P["lru_b_x"][:, 0],
                                       P["lru_b_a"][:, 1], P["lru_b_x"][:, 1]], axis=-1)
    w_hi = P["w_router"].astype(BF16)
    Q["w_router_hi"] = w_hi
    Q["w_router_lo"] = (P["w_router"] - w_hi.astype(F32)).astype(BF16)
    Q["w_s_gate_bf16"] = P["w_s_gate"].astype(BF16)
    Q["w_s_up_bf16"] = P["w_s_up"].astype(BF16)
    Q["w_s_down_bf16"] = P["w_s_down"].astype(BF16)
    return Q


def _group_tables(counts):
    bm = EXPERT_BLOCK
    blocks = (counts.astype(I32) + bm - 1) // bm
    first_blk = jnp.concatenate([jnp.zeros((1,), I32), jnp.cumsum(blocks).astype(I32)])
    group_start = (first_blk[:N_EXPERTS] * bm).astype(F32)[None, :]
    return group_start, first_blk


def _trunk(x_prompt, x_sample, c, cache_k, cache_v, state_lru, c_ctx, P):
    n_seq_c, ctx_t, _ = x_prompt.shape
    n_seq_l, lat_t, _ = x_sample.shape
    past = cache_k.shape[2]
    n_ctx = n_seq_c * ctx_t
    n_lat = n_seq_l * lat_t
    n = n_ctx + n_lat
    assert n_ctx % lat_t == 0, "latent sequences must start on a whole-sequence boundary of the merged token axis"
    lat_off = n_ctx // lat_t
    P = _prepare(P)

    n_cond = -(-(1 + n_seq_l) // SUBLANES) * SUBLANES
    cond = jnp.concatenate([c_ctx[None, :], c, jnp.zeros((n_cond - 1 - n_seq_l, D_MODEL), F32)], axis=0)
    mods = _mod_call(cond, P["w_mod"], P["b_mod"])
    rope_cos, rope_sin = _rope_tables(lat_t)
    tri = (jnp.arange(TOKEN_TILE)[:, None] > jnp.arange(TOKEN_TILE)[None, :]).astype(BF16)
    n_blocks = n * TOP_K // EXPERT_BLOCK + N_EXPERTS
    n_slots = n_blocks * EXPERT_BLOCK

    x = jnp.concatenate([x_prompt.reshape(n_ctx, D_MODEL), x_sample.reshape(n_lat, D_MODEL)], axis=0)
    ks, vs, ss = [], [], []
    for l in range(DEPTH):
        mod3 = mods[l].reshape(n_cond, 1, 6 * D_MODEL)
        q, kr, vb, kc, vc, u, gate, cv = _premix_call(x, mod3, l, P, rope_cos, rope_sin, n_ctx, lat_t)
        ks.append(kc[:n_ctx].reshape(n_seq_c, ctx_t, N_KV_HEADS, HEAD_DIM))
        vs.append(vc[:n_ctx].reshape(n_seq_c, ctx_t, N_KV_HEADS, HEAD_DIM))

        as_ctx = lambda a: a.reshape(n // ctx_t, ctx_t, a.shape[-1])
        as_lat = lambda a: a.reshape(n // lat_t, lat_t, a.shape[-1])

        attn_c = _attn_call(as_ctx(q), as_ctx(kr), as_ctx(vb), min(ctx_t, 256), n_seq_c, 0, 0)
        k_all = jnp.concatenate([kr[n_ctx:].reshape(n_seq_l, lat_t, KV_WIDTH),
                                 cache_k[:, l].reshape(n_seq_l, past, KV_WIDTH).astype(BF16)], axis=1)
        cv_ones = jnp.concatenate([cache_v[:, l], jnp.ones_like(cache_v[:, l])], axis=-1)
        v_all = jnp.concatenate([vb[n_ctx:].reshape(n_seq_l, lat_t, 2 * KV_WIDTH),
                                 cv_ones.reshape(n_seq_l, past, 2 * KV_WIDTH).astype(BF16)], axis=1)
        attn_l = _attn_call(as_lat(q), k_all, v_all, 256, n_seq_l, lat_off, 0)

        lru_c, fin_c = _lru_call(as_ctx(u), as_ctx(gate), jnp.zeros((n_seq_c, 2, LRU_WIDTH), F32), l, P,
                                 n_seq_c, 0)
        lru_l, _ = _lru_call(as_lat(u), as_lat(gate), state_lru[:, l], l, P, n_seq_l, lat_off)
        ss.append(fin_c)
        conv_c = _convmod_call(as_ctx(cv), l, P, n_seq_c, 0)
        conv_l = _convmod_call(as_lat(cv), l, P, n_seq_l, lat_off)

        x1, h2, logits = _postmix_call(
            attn_c.reshape(n_ctx, ATTN_WIDTH), attn_l.reshape(n_lat, ATTN_WIDTH),
            lru_c.reshape(n_ctx, LRU_WIDTH), lru_l.reshape(n_lat, LRU_WIDTH),
            conv_c.reshape(n_ctx, CONV_WIDTH), conv_l.reshape(n_lat, CONV_WIDTH),
            x, mod3, l, P, n_ctx, lat_t)

        ids, gw, rank, counts = _route_call(logits, P["b_router"][l], tri)
        group_start, first_blk = _group_tables(counts[0])
        slots = _slots_call(ids, rank, group_start).reshape(-1)
        xs = _dispatch_call(first_blk, slots, h2, n_slots)
        ys = _expert_call(first_blk, xs, P["w_e_gate"], P["w_e_up"], P["w_e_down"], l)
        x = _combine_call(slots, gw, x1, h2, mod3, ys, l, P, n_ctx, lat_t)

    y_prompt = x[:n_ctx].reshape(n_seq_c, ctx_t, D_MODEL)
    y_sample = x[n_ctx:].reshape(n_seq_l, lat_t, D_MODEL)
    return (y_prompt, y_sample, jnp.stack(ks, axis=1), jnp.stack(vs, axis=1), jnp.stack(ss, axis=1))


def kernel(x_prompt, x_sample, c, cache_k, cache_v, state_lru, c_ctx, w_mod, b_mod, norm1_g, w_in, q_norm_g, k_norm_g, lru_conv_w, lru_conv_b, lru_w_a, lru_b_a, lru_w_x, lru_b_x, lru_lambda, cm_dw_w, cm_dw_b, cm_ln_g, cm_ln_b, out_norm_g, w_out, norm2_g, w_router, b_router, w_e_gate, w_e_up, w_e_down, w_s_gate, w_s_up, w_s_down):
    P = {"w_mod": w_mod, "b_mod": b_mod, "norm1_g": norm1_g, "w_in": w_in, "q_norm_g": q_norm_g,
         "k_norm_g": k_norm_g, "lru_conv_w": lru_conv_w, "lru_conv_b": lru_conv_b, "lru_w_a": lru_w_a,
         "lru_b_a": lru_b_a, "lru_w_x": lru_w_x, "lru_b_x": lru_b_x, "lru_lambda": lru_lambda,
         "cm_dw_w": cm_dw_w, "cm_dw_b": cm_dw_b, "cm_ln_g": cm_ln_g, "cm_ln_b": cm_ln_b,
         "out_norm_g": out_norm_g, "w_out": w_out, "norm2_g": norm2_g, "w_router": w_router,
         "b_router": b_router, "w_e_gate": w_e_gate, "w_e_up": w_e_up, "w_e_down": w_e_down,
         "w_s_gate": w_s_gate, "w_s_up": w_s_up, "w_s_down": w_s_down}
    return _trunk(x_prompt, x_sample, c, cache_k, cache_v, state_lru, c_ctx, P)
```

```python
import functools
import math

import jax
import jax.numpy as jnp
from jax import lax
from jax.experimental import pallas as pl
from jax.experimental.pallas import tpu as pltpu

F32 = jnp.float32
BF16 = jnp.bfloat16
I32 = jnp.int32

D_MODEL = 1024
DEPTH = 2
GRID_W = 64
ATTN_WIDTH = 512
LRU_WIDTH = 256
CONV_WIDTH = 256
HEAD_DIM = 64
N_HEADS = 8
N_KV_HEADS = 2
KV_GROUP = N_HEADS // N_KV_HEADS
KV_WIDTH = N_KV_HEADS * HEAD_DIM
ROPE_BASE = 10000.0
LRU_HEADS = 4
LRU_HEAD_DIM = LRU_WIDTH // LRU_HEADS
LRU_CONV = 4
RG_C = 8.0
CM_KERNEL = 31
N_EXPERTS = 256
TOP_K = 8
EXPERT_FF = 256
ROUTED_SCALE = 2.5
EPS = 1e-6
IN_WIDTH = ATTN_WIDTH + 2 * KV_WIDTH + 2 * LRU_WIDTH + 2 * CONV_WIDTH

SUBLANES = 8
LANES = 128
VMEM_LIMIT = 56 * 1024 * 1024

TOKEN_TILE = 256
SCAN_CHUNK = 256
CONV_CHUNK = 128
CONV_HALO = 16
LRU_HALO = 8
EXPERT_BLOCK = 256
EXPERT_IN_SLOTS = 4
COMBINE_TILE = 256
DISPATCH_TILE = 512


def _params(sem):
    return pltpu.CompilerParams(dimension_semantics=sem, vmem_limit_bytes=VMEM_LIMIT)


def _sigmoid(x):
    return 1.0 / (1.0 + jnp.exp(-x))


def _silu(x):
    return x * _sigmoid(x)


def _bdot(a, b):
    return jnp.dot(a.astype(BF16), b.astype(BF16), preferred_element_type=F32)


def _split_dot(a, b_bf16):
    hi = a.astype(BF16)
    lo = (a - hi.astype(F32)).astype(BF16)
    return (jnp.dot(hi, b_bf16, preferred_element_type=F32)
            + jnp.dot(lo, b_bf16, preferred_element_type=F32))


U32 = jnp.uint32
PACKED = D_MODEL // 2


def _pack_bf16_pair(x):
    hi = lax.bitcast_convert_type(x[:, :PACKED].astype(BF16).astype(F32), U32)
    lo = lax.bitcast_convert_type(x[:, PACKED:].astype(BF16).astype(F32), U32)
    return (hi & jnp.uint32(0xFFFF0000)) | (lo >> 16)


def _unpack_bf16_pair(w):
    hi = lax.bitcast_convert_type(w & jnp.uint32(0xFFFF0000), F32)
    lo = lax.bitcast_convert_type(w << 16, F32)
    return hi, lo


def _rms(x, g):
    return x * lax.rsqrt(jnp.mean(x * x, axis=-1, keepdims=True) + EPS) * g


MOD_COLS = 1536


def _mod_kernel(c_ref, w_ref, b_ref, o_ref):
    a = _silu(c_ref[...])
    o_ref[...] = jnp.dot(a, w_ref[...], preferred_element_type=F32,
                         precision=lax.Precision.HIGHEST) + b_ref[...]


def _mod_call(cond, w_mod, b_mod):
    n_cond = cond.shape[0]
    width = 6 * D_MODEL
    return pl.pallas_call(
        _mod_kernel,
        out_shape=jax.ShapeDtypeStruct((DEPTH, n_cond, width), F32),
        grid=(DEPTH, width // MOD_COLS),
        in_specs=[
            pl.BlockSpec((n_cond, D_MODEL), lambda l, j: (0, 0)),
            pl.BlockSpec((None, D_MODEL, MOD_COLS), lambda l, j: (l, 0, j)),
            pl.BlockSpec((None, 1, MOD_COLS), lambda l, j: (l, 0, j)),
        ],
        out_specs=pl.BlockSpec((None, n_cond, MOD_COLS), lambda l, j: (l, 0, j)),
        compiler_params=_params(("arbitrary", "arbitrary")),
        name="mod",
    )(cond, w_mod, b_mod.reshape(DEPTH, 1, width))


def _swap_pairs(x):
    n = x.shape[-1]
    lane = lax.broadcasted_iota(I32, x.shape, 1)
    nxt = pltpu.roll(x, n - 1, 1)
    prv = pltpu.roll(x, 1, 1)
    return jnp.where(lane % 2 == 0, nxt, prv)


def _premix_kernel(x_ref, mod_ref, n1g_ref, w_ref, qg_ref, kg_ref, bd_ref, cos_ref, sin_ref,
                   q_ref, kr_ref, vb_ref, kc_ref, vc_ref, u_ref, gate_ref, cv_ref):
    x = x_ref[...]
    sh1 = mod_ref[:, 0:D_MODEL]
    sc1 = mod_ref[:, D_MODEL:2 * D_MODEL]
    h = _rms(x, n1g_ref[...]) * (1.0 + sc1) + sh1
    z = jnp.dot(h.astype(BF16), w_ref[...], preferred_element_type=F32)
    o = 0
    q = z[:, o:o + ATTN_WIDTH]; o += ATTN_WIDTH
    k = z[:, o:o + KV_WIDTH]; o += KV_WIDTH
    v = z[:, o:o + KV_WIDTH]; o += KV_WIDTH
    u_ref[...] = z[:, o:o + LRU_WIDTH]; o += LRU_WIDTH
    gate_ref[...] = z[:, o:o + LRU_WIDTH]; o += LRU_WIDTH
    cv_ref[...] = z[:, o:o + 2 * CONV_WIDTH]

    bd = bd_ref[...]
    inv_hd = 1.0 / HEAD_DIM
    q_ms = _split_dot(q * q, bd) * inv_hd
    k_ms = _split_dot(k * k, bd[:KV_WIDTH, :KV_WIDTH]) * inv_hd
    qn = q * lax.rsqrt(q_ms + EPS) * qg_ref[...]
    kn = k * lax.rsqrt(k_ms + EPS) * kg_ref[...]
    kc_ref[...] = kn
    vc_ref[...] = v
    ones = jnp.ones((v.shape[0], HEAD_DIM), F32)
    vb_ref[...] = jnp.concatenate([v[:, :HEAD_DIM], ones, v[:, HEAD_DIM:], ones], axis=1).astype(BF16)

    cos = cos_ref[...]
    sin = sin_ref[...]
    reps = ATTN_WIDTH // cos.shape[-1]
    cos_q = jnp.concatenate([cos] * reps, axis=1)
    sin_q = jnp.concatenate([sin] * reps, axis=1)
    qr = qn * cos_q + _swap_pairs(qn) * sin_q
    kr = kn * cos + _swap_pairs(kn) * sin
    q_ref[...] = (qr * (HEAD_DIM ** -0.5 * math.log2(math.e))).astype(BF16)
    kr_ref[...] = kr.astype(BF16)


def _premix_call(x, mod3, l, P, rope_cos, rope_sin, n_ctx, lat_t):
    n = x.shape[0]
    tm = TOKEN_TILE
    n_ctx_tiles = n_ctx // tm
    tiles_per_seq = lat_t // tm

    def mod_idx(i):
        return (jnp.where(i < n_ctx_tiles, 0, 1 + (i - n_ctx_tiles) // tiles_per_seq), 0, 0)

    def rope_idx(i):
        return (jnp.where(i < n_ctx_tiles, tiles_per_seq, (i - n_ctx_tiles) % tiles_per_seq), 0)

    const = lambda i: (0, 0)
    row = lambda i: (i, 0)
    outs = pl.pallas_call(
        _premix_kernel,
        out_shape=(
            jax.ShapeDtypeStruct((n, ATTN_WIDTH), BF16),
            jax.ShapeDtypeStruct((n, KV_WIDTH), BF16),
            jax.ShapeDtypeStruct((n, 2 * KV_WIDTH), BF16),
            jax.ShapeDtypeStruct((n, KV_WIDTH), F32),
            jax.ShapeDtypeStruct((n, KV_WIDTH), F32),
            jax.ShapeDtypeStruct((n, LRU_WIDTH), F32),
            jax.ShapeDtypeStruct((n, LRU_WIDTH), F32),
            jax.ShapeDtypeStruct((n, 2 * CONV_WIDTH), F32),
        ),
        grid=(n // tm,),
        in_specs=[
            pl.BlockSpec((tm, D_MODEL), row),
            pl.BlockSpec((None, 1, 6 * D_MODEL), mod_idx),
            pl.BlockSpec((1, D_MODEL), const),
            pl.BlockSpec((D_MODEL, IN_WIDTH), const),
            pl.BlockSpec((1, ATTN_WIDTH), const),
            pl.BlockSpec((1, KV_WIDTH), const),
            pl.BlockSpec((ATTN_WIDTH, ATTN_WIDTH), const),
            pl.BlockSpec((tm, LANES), rope_idx),
            pl.BlockSpec((tm, LANES), rope_idx),
        ],
        out_specs=(
            pl.BlockSpec((tm, ATTN_WIDTH), row),
            pl.BlockSpec((tm, KV_WIDTH), row),
            pl.BlockSpec((tm, 2 * KV_WIDTH), row),
            pl.BlockSpec((tm, KV_WIDTH), row),
            pl.BlockSpec((tm, KV_WIDTH), row),
            pl.BlockSpec((tm, LRU_WIDTH), row),
            pl.BlockSpec((tm, LRU_WIDTH), row),
            pl.BlockSpec((tm, 2 * CONV_WIDTH), row),
        ),
        compiler_params=_params(("parallel",)),
        name="premix",
    )(x, mod3, P["norm1_g"][l][None, :], P["w_in_bf16"][l], P["q_norm_g_t"][l][None, :],
      P["k_norm_g_t"][l][None, :], P["head_blockdiag"], rope_cos, rope_sin)
    return outs


def _attn_kernel(q_ref, k_ref, v_ref, o_ref):
    outs = []
    for kh in range(N_KV_HEADS):
        k = k_ref[:, kh * HEAD_DIM:(kh + 1) * HEAD_DIM]
        v = v_ref[:, kh * 2 * HEAD_DIM:(kh + 1) * 2 * HEAD_DIM]
        for g in range(KV_GROUP):
            hd = kh * KV_GROUP + g
            q = q_ref[:, hd * HEAD_DIM:(hd + 1) * HEAD_DIM]
            s = lax.dot_general(q, k, (((1,), (1,)), ((), ())), preferred_element_type=F32)
            m = jnp.max(s, axis=-1, keepdims=True)
            p = jnp.exp2(s - m).astype(BF16)
            o = jnp.dot(p, v, preferred_element_type=F32)
            outs.append(o[:, :HEAD_DIM] / o[:, HEAD_DIM:HEAD_DIM + 1])
    o_ref[...] = jnp.concatenate(outs, axis=1)


def _attn_call(q3, k3, v3, tq, b, q_off, kv_off):
    s = q3.shape[1]
    t = k3.shape[1]
    return pl.pallas_call(
        _attn_kernel,
        out_shape=jax.ShapeDtypeStruct((b, s, ATTN_WIDTH), F32),
        grid=(b, s // tq),
        in_specs=[
            pl.BlockSpec((None, tq, ATTN_WIDTH), lambda i, j: (i + q_off, j, 0)),
            pl.BlockSpec((None, t, KV_WIDTH), lambda i, j: (i + kv_off, 0, 0)),
            pl.BlockSpec((None, t, 2 * KV_WIDTH), lambda i, j: (i + kv_off, 0, 0)),
        ],
        out_specs=pl.BlockSpec((None, tq, ATTN_WIDTH), lambda i, j: (i, j, 0)),
        compiler_params=_params(("parallel", "parallel")),
        name="attention",
    )(q3, k3, v3)


def _chunk_scan(a, b, reverse):
    n = a.shape[0]
    row = lax.broadcasted_iota(I32, a.shape, 0)
    d = 1
    while d < n:
        if reverse:
            a_s = pltpu.roll(a, n - d, 0)
            b_s = pltpu.roll(b, n - d, 0)
            ok = row < n - d
        else:
            a_s = pltpu.roll(a, d, 0)
            b_s = pltpu.roll(b, d, 0)
            ok = row >= d
        b = jnp.where(ok, a * b_s + b, b)
        a = jnp.where(ok, a * a_s, a)
        d *= 2
    return a, b


def _gelu_tanh(x):
    return 0.5 * x * (1.0 + jnp.tanh(math.sqrt(2.0 / math.pi) * (x + 0.044715 * (x * x * x))))


def _lru_kernel(u_ref, gate_ref, cw_ref, cb_ref, wg_ref, bg_ref, lam_ref, h0_ref,
                y_ref, fin_ref, upad, xc_scr, fwd_scr):
    t_len = u_ref.shape[0]
    ch = SCAN_CHUNK
    n_chunks = t_len // ch
    w = LRU_WIDTH
    zeros_halo = jnp.zeros((LRU_HALO, w), F32)
    upad[0:LRU_HALO, :] = zeros_halo
    upad[t_len + LRU_HALO:t_len + 2 * LRU_HALO, :] = zeros_halo

    def fill(c, carry):
        r0 = pl.multiple_of(c * ch, ch)
        upad[pl.ds(r0 + LRU_HALO, ch), :] = u_ref[pl.ds(r0, ch), :]
        return carry

    lax.fori_loop(0, n_chunks, fill, 0)

    lam = lam_ref[...]
    nlam = -lam
    softplus = jnp.maximum(nlam, 0.0) + jnp.log(1.0 + jnp.exp(-jnp.abs(nlam)))
    decay = -RG_C * softplus
    cw = cw_ref[...]
    cb = cb_ref[...]

    def gates(xc, d):
        z = jnp.dot(xc.astype(BF16), wg_ref[:, 2 * d * w:2 * (d + 1) * w],
                    preferred_element_type=F32) + bg_ref[:, 2 * d * w:2 * (d + 1) * w]
        r = _sigmoid(z[:, :w])
        i = _sigmoid(z[:, w:])
        log_a = decay[d:d + 1, :] * r
        a = jnp.exp(log_a)
        b = jnp.sqrt(-jnp.tanh(log_a) * (a * a + 1.0)) * (i * xc)
        return a, b

    def fwd(c, h):
        r0 = pl.multiple_of(c * ch, ch)
        win = upad[pl.ds(r0, ch + 2 * LRU_HALO), :]
        xc = cb
        for j in range(LRU_CONV):
            s0 = LRU_HALO - (LRU_CONV - 1) // 2 + j
            xc = xc + cw[j:j + 1, :] * win[s0:s0 + ch, :]
        xc_scr[pl.ds(r0, ch), :] = xc
        a, b = gates(xc, 0)
        a_cum, b_cum = _chunk_scan(a, b, reverse=False)
        hs = a_cum * h + b_cum
        fwd_scr[pl.ds(r0, ch), :] = hs
        return hs[ch - 1:ch, :]

    h_f = lax.fori_loop(0, n_chunks, fwd, h0_ref[0:1, :])

    def bwd(ci, h):
        c = n_chunks - 1 - ci
        r0 = pl.multiple_of(c * ch, ch)
        xc = xc_scr[pl.ds(r0, ch), :]
        a, b = gates(xc, 1)
        a_cum, b_cum = _chunk_scan(a, b, reverse=True)
        hs = a_cum * h + b_cum
        y_ref[pl.ds(r0, ch), :] = (fwd_scr[pl.ds(r0, ch), :] + hs) * _gelu_tanh(gate_ref[pl.ds(r0, ch), :])
        return hs[0:1, :]

    h_b = lax.fori_loop(0, n_chunks, bwd, h0_ref[1:2, :])
    fin_ref[0:1, :] = h_f
    fin_ref[1:2, :] = h_b


def _lru_call(u3, gate3, h0, l, P, b, seq_off):
    _, t, w = u3.shape
    const = lambda i: (0, 0)
    seq = lambda i: (i, 0, 0)
    seq_in = lambda i: (i + seq_off, 0, 0)
    return pl.pallas_call(
        _lru_kernel,
        out_shape=(jax.ShapeDtypeStruct((b, t, w), F32),
                   jax.ShapeDtypeStruct((b, 2, w), F32)),
        grid=(b,),
        in_specs=[
            pl.BlockSpec((None, t, w), seq_in),
            pl.BlockSpec((None, t, w), seq_in),
            pl.BlockSpec((LRU_CONV, w), const),
            pl.BlockSpec((1, w), const),
            pl.BlockSpec((w, 4 * w), const),
            pl.BlockSpec((1, 4 * w), const),
            pl.BlockSpec((2, w), const),
            pl.BlockSpec((None, 2, w), seq),
        ],
        out_specs=(pl.BlockSpec((None, t, w), seq),
                   pl.BlockSpec((None, 2, w), seq)),
        scratch_shapes=[pltpu.VMEM((t + 2 * LRU_HALO, w), F32),
                        pltpu.VMEM((t, w), F32),
                        pltpu.VMEM((t, w), F32)],
        compiler_params=_params(("parallel",)),
        name="rglru",
    )(u3, gate3, P["lru_conv_w"][l], P["lru_conv_b"][l][None, :], P["lru_gate_w"][l],
      P["lru_gate_b"][l][None, :], P["lru_lambda"][l], h0)


def _convmod_kernel(cv_ref, w_ref, b_ref, g_ref, beta_ref, o_ref, hpad):
    t_len = cv_ref.shape[0]
    ch = CONV_CHUNK
    n_chunks = t_len // ch
    w = CONV_WIDTH
    zeros_halo = jnp.zeros((CONV_HALO, w), F32)
    hpad[0:CONV_HALO, :] = zeros_halo
    hpad[t_len + CONV_HALO:t_len + 2 * CONV_HALO, :] = zeros_halo

    def glu(c, carry):
        r0 = pl.multiple_of(c * ch, ch)
        blk = cv_ref[pl.ds(r0, ch), :]
        hpad[pl.ds(r0 + CONV_HALO, ch), :] = blk[:, :w] * _sigmoid(blk[:, w:])
        return carry

    lax.fori_loop(0, n_chunks, glu, 0)

    taps = w_ref[...]
    bias = b_ref[...]
    gamma = g_ref[...]
    beta = beta_ref[...]

    def conv(c, carry):
        r0 = pl.multiple_of(c * ch, ch)
        win = hpad[pl.ds(r0, ch + 2 * CONV_HALO), :]
        acc = bias
        for j in range(CM_KERNEL):
            s0 = CONV_HALO - CM_KERNEL // 2 + j
            acc = acc + taps[j:j + 1, :] * win[s0:s0 + ch, :]
        mu = jnp.mean(acc, axis=-1, keepdims=True)
        cen = acc - mu
        var = jnp.mean(cen * cen, axis=-1, keepdims=True)
        y = cen * lax.rsqrt(var + EPS) * gamma + beta
        o_ref[pl.ds(r0, ch), :] = _silu(y)
        return carry

    lax.fori_loop(0, n_chunks, conv, 0)


def _convmod_call(cv3, l, P, b, seq_off):
    t = cv3.shape[1]
    w = CONV_WIDTH
    const = lambda i: (0, 0)
    seq = lambda i: (i, 0, 0)
    return pl.pallas_call(
        _convmod_kernel,
        out_shape=jax.ShapeDtypeStruct((b, t, w), F32),
        grid=(b,),
        in_specs=[
            pl.BlockSpec((None, t, 2 * w), lambda i: (i + seq_off, 0, 0)),
            pl.BlockSpec((CM_KERNEL, w), const),
            pl.BlockSpec((1, w), const),
            pl.BlockSpec((1, w), const),
            pl.BlockSpec((1, w), const),
        ],
        out_specs=pl.BlockSpec((None, t, w), seq),
        scratch_shapes=[pltpu.VMEM((t + 2 * CONV_HALO, w), F32)],
        compiler_params=_params(("parallel",)),
        name="convmod",
    )(cv3, P["cm_dw_w"][l], P["cm_dw_b"][l][None, :], P["cm_ln_g"][l][None, :], P["cm_ln_b"][l][None, :])


def _postmix_kernel(n_ctx_tiles, attn_c_ref, attn_l_ref, lru_c_ref, lru_l_ref, conv_c_ref, conv_l_ref,
                    x_ref, mod_ref, og_ref, wo_ref, n2g_ref, wr_hi_ref, wr_lo_ref,
                    x1_ref, h2_ref, lg_ref):
    is_ctx = pl.program_id(0) < n_ctx_tiles
    og = og_ref[...]
    a0, a1, a2 = ATTN_WIDTH, ATTN_WIDTH + LRU_WIDTH, D_MODEL
    y = jnp.concatenate([_rms(jnp.where(is_ctx, attn_c_ref[...], attn_l_ref[...]), og[:, :a0]),
                         _rms(jnp.where(is_ctx, lru_c_ref[...], lru_l_ref[...]), og[:, a0:a1]),
                         _rms(jnp.where(is_ctx, conv_c_ref[...], conv_l_ref[...]), og[:, a1:a2])], axis=1)
    y = jnp.dot(y.astype(BF16), wo_ref[...], preferred_element_type=F32)
    g1 = mod_ref[:, 2 * D_MODEL:3 * D_MODEL]
    sh2 = mod_ref[:, 3 * D_MODEL:4 * D_MODEL]
    sc2 = mod_ref[:, 4 * D_MODEL:5 * D_MODEL]
    x1 = x_ref[...] + g1 * y
    x1_ref[...] = x1
    h2 = _rms(x1, n2g_ref[...]) * (1.0 + sc2) + sh2
    h2_ref[...] = _pack_bf16_pair(h2)
    hi = h2.astype(BF16)
    lo = (h2 - hi.astype(F32)).astype(BF16)
    w_hi = wr_hi_ref[...]
    lg_ref[...] = (jnp.dot(hi, w_hi, preferred_element_type=F32)
                   + jnp.dot(lo, w_hi, preferred_element_type=F32)
                   + jnp.dot(hi, wr_lo_ref[...], preferred_element_type=F32))


def _postmix_call(attn_c, attn_l, lru_c, lru_l, conv_c, conv_l, x, mod3, l, P, n_ctx, lat_t):
    n = x.shape[0]
    tm = TOKEN_TILE
    n_ctx_tiles = n_ctx // tm
    tiles_per_seq = lat_t // tm

    def mod_idx(i):
        return (jnp.where(i < n_ctx_tiles, 0, 1 + (i - n_ctx_tiles) // tiles_per_seq), 0, 0)

    const = lambda i: (0, 0)
    row = lambda i: (i, 0)
    row_c = lambda i: (jnp.minimum(i, n_ctx_tiles - 1), 0)
    row_l = lambda i: (jnp.maximum(i - n_ctx_tiles, 0), 0)
    return pl.pallas_call(
        functools.partial(_postmix_kernel, n_ctx_tiles),
        out_shape=(jax.ShapeDtypeStruct((n, D_MODEL), F32),
                   jax.ShapeDtypeStruct((n, PACKED), U32),
                   jax.ShapeDtypeStruct((n, N_EXPERTS), F32)),
        grid=(n // tm,),
        in_specs=[
            pl.BlockSpec((tm, ATTN_WIDTH), row_c),
            pl.BlockSpec((tm, ATTN_WIDTH), row_l),
            pl.BlockSpec((tm, LRU_WIDTH), row_c),
            pl.BlockSpec((tm, LRU_WIDTH), row_l),
            pl.BlockSpec((tm, CONV_WIDTH), row_c),
            pl.BlockSpec((tm, CONV_WIDTH), row_l),
            pl.BlockSpec((tm, D_MODEL), row),
            pl.BlockSpec((None, 1, 6 * D_MODEL), mod_idx),
            pl.BlockSpec((1, D_MODEL), const),
            pl.BlockSpec((D_MODEL, D_MODEL), const),
            pl.BlockSpec((1, D_MODEL), const),
            pl.BlockSpec((D_MODEL, N_EXPERTS), const),
            pl.BlockSpec((D_MODEL, N_EXPERTS), const),
        ],
        out_specs=(pl.BlockSpec((tm, D_MODEL), row),
                   pl.BlockSpec((tm, PACKED), row),
                   pl.BlockSpec((tm, N_EXPERTS), row)),
        compiler_params=_params(("parallel",)),
        name="postmix",
    )(attn_c, attn_l, lru_c, lru_l, conv_c, conv_l, x, mod3, P["out_norm_g"][l][None, :], P["w_out_bf16"][l],
      P["norm2_g"][l][None, :], P["w_router_hi"][l], P["w_router_lo"][l])


def _route_kernel(lg_ref, br_ref, tri_ref, ids_ref, gw_ref, rank_ref, cnt_ref, carry):
    i = pl.program_id(0)

    @pl.when(i == 0)
    def _():
        carry[...] = jnp.zeros_like(carry)

    scores = _sigmoid(lg_ref[...])
    sel = scores + br_ref[...]
    tm = scores.shape[0]
    lane = lax.broadcasted_iota(I32, (tm, N_EXPERTS), 1).astype(F32)
    slot_lane = lax.broadcasted_iota(I32, (tm, LANES), 1)
    ids_acc = jnp.zeros((tm, LANES), F32)
    gw_acc = jnp.zeros((tm, LANES), F32)
    hot = jnp.zeros((tm, N_EXPERTS), F32)
    picked = []
    for k in range(TOP_K):
        m = jnp.max(sel, axis=-1, keepdims=True)
        idx = jnp.min(jnp.where(sel == m, lane, float(N_EXPERTS)), axis=-1, keepdims=True)
        one = lane == idx
        g = jnp.sum(jnp.where(one, scores, 0.0), axis=-1, keepdims=True)
        sel = jnp.where(one, -jnp.inf, sel)
        hot = jnp.where(one, 1.0, hot)
        ids_acc = jnp.where(slot_lane == k, idx, ids_acc)
        gw_acc = jnp.where(slot_lane == k, g, gw_acc)
        picked.append(idx)
    denom = jnp.sum(gw_acc, axis=-1, keepdims=True)
    gw_acc = ROUTED_SCALE * gw_acc / denom

    before = jnp.dot(tri_ref[...], hot.astype(BF16), preferred_element_type=F32) + carry[0:1, :]
    rank_acc = jnp.zeros((tm, LANES), F32)
    for k in range(TOP_K):
        r = jnp.sum(jnp.where(lane == picked[k], before, 0.0), axis=-1, keepdims=True)
        rank_acc = jnp.where(slot_lane == k, r, rank_acc)
    carry[0:1, :] = carry[0:1, :] + jnp.sum(hot, axis=0, keepdims=True)
    cnt_ref[...] = carry[...]
    ids_ref[...] = ids_acc[:, :TOP_K].astype(I32)
    gw_ref[...] = gw_acc[:, :TOP_K]
    rank_ref[...] = rank_acc[:, :TOP_K].astype(I32)


def _route_call(logits, b_router_l, tri):
    n = logits.shape[0]
    tm = TOKEN_TILE
    const = lambda i: (0, 0)
    row = lambda i: (i, 0)
    return pl.pallas_call(
        _route_kernel,
        out_shape=(jax.ShapeDtypeStruct((n, TOP_K), I32),
                   jax.ShapeDtypeStruct((n, TOP_K), F32),
                   jax.ShapeDtypeStruct((n, TOP_K), I32),
                   jax.ShapeDtypeStruct((SUBLANES, N_EXPERTS), F32)),
        grid=(n // tm,),
        in_specs=[
            pl.BlockSpec((tm, N_EXPERTS), row),
            pl.BlockSpec((1, N_EXPERTS), const),
            pl.BlockSpec((tm, tm), const),
        ],
        out_specs=(pl.BlockSpec((tm, TOP_K), row),
                   pl.BlockSpec((tm, TOP_K), row),
                   pl.BlockSpec((tm, TOP_K), row),
                   pl.BlockSpec((SUBLANES, N_EXPERTS), const)),
        scratch_shapes=[pltpu.VMEM((SUBLANES, N_EXPERTS), F32)],
        compiler_params=_params(("arbitrary",)),
        name="route",
    )(logits, b_router_l[None, :], tri)


def _slots_kernel(ids_ref, rank_ref, start_ref, slots_ref):
    ids = ids_ref[...].astype(F32)
    tm = ids.shape[0]
    lane = lax.broadcasted_iota(I32, (tm, N_EXPERTS), 1).astype(F32)
    slot_lane = lax.broadcasted_iota(I32, (tm, TOP_K), 1)
    start = start_ref[...]
    acc = jnp.zeros((tm, TOP_K), F32)
    for k in range(TOP_K):
        s = jnp.sum(jnp.where(lane == ids[:, k:k + 1], start, 0.0), axis=-1, keepdims=True)
        acc = jnp.where(slot_lane == k, s, acc)
    slots_ref[...] = acc.astype(I32) + rank_ref[...]


def _slots_call(ids, rank, group_start):
    n = ids.shape[0]
    tm = TOKEN_TILE
    row = lambda i: (i, 0)
    return pl.pallas_call(
        _slots_kernel,
        out_shape=jax.ShapeDtypeStruct((n, TOP_K), I32),
        grid=(n // tm,),
        in_specs=[pl.BlockSpec((tm, TOP_K), row), pl.BlockSpec((tm, TOP_K), row),
                  pl.BlockSpec((1, N_EXPERTS), lambda i: (0, 0))],
        out_specs=pl.BlockSpec((tm, TOP_K), row),
        compiler_params=_params(("parallel",)),
        name="slots",
    )(ids, rank, group_start)


def _wait_rows(src_ref, dst_ref, sem, n_waits):
    for _ in range(n_waits):
        pltpu.make_async_copy(src_ref, dst_ref, sem).wait()


def _dispatch_kernel(first_ref, slots_ref, h_ref, xs_ref, zeros, sem_rows, sem_zero):
    n_tok = h_ref.shape[0]
    bm = EXPERT_BLOCK
    n_blocks = xs_ref.shape[0] // bm
    nused = first_ref[N_EXPERTS]

    def zero_copy(b):
        return pltpu.make_async_copy(zeros, xs_ref.at[pl.ds(pl.multiple_of(b * bm, bm), bm), :], sem_zero)

    @pl.when(pl.program_id(0) == 0)
    def _():
        zeros[...] = jnp.zeros_like(zeros)

        def group(start):
            def body(e, c):
                @pl.when(first_ref[e + 1] > first_ref[e])
                def _():
                    cp = zero_copy(first_ref[e + 1] - 1)
                    cp.start() if start else cp.wait()
                return c
            lax.fori_loop(0, N_EXPERTS, body, 0)

        def tail(start):
            def body(b, c):
                cp = zero_copy(b)
                cp.start() if start else cp.wait()
                return c
            lax.fori_loop(nused, n_blocks, body, 0)

        group(True)
        tail(True)
        group(False)
        tail(False)

    def issue(t, c):
        for k in range(TOP_K):
            slot = slots_ref[t * TOP_K + k]
            pltpu.make_async_copy(h_ref.at[pl.ds(t, 1), :], xs_ref.at[pl.ds(slot, 1), :],
                                  sem_rows).start(priority=k % 2)
        return c

    lax.fori_loop(0, n_tok, issue, 0)
    _wait_rows(h_ref, xs_ref.at[pl.ds(0, n_tok), :], sem_rows, TOP_K)


def _dispatch_call(first_blk, slots_flat, h2, n_slots):
    n = h2.shape[0]
    tm = DISPATCH_TILE
    grid_spec = pltpu.PrefetchScalarGridSpec(
        num_scalar_prefetch=1,
        grid=(n // tm,),
        in_specs=[
            pl.BlockSpec((tm * TOP_K,), lambda i, first: (i,), memory_space=pltpu.SMEM),
            pl.BlockSpec((tm, PACKED), lambda i, first: (i, 0)),
        ],
        out_specs=pl.BlockSpec(memory_space=pl.ANY),
        scratch_shapes=[pltpu.VMEM((EXPERT_BLOCK, PACKED), U32),
                        pltpu.SemaphoreType.DMA,
                        pltpu.SemaphoreType.DMA],
    )
    return pl.pallas_call(
        _dispatch_kernel,
        out_shape=jax.ShapeDtypeStruct((n_slots, PACKED), U32),
        grid_spec=grid_spec,
        compiler_params=_params(("arbitrary",)),
        name="dispatch",
    )(first_blk, slots_flat, h2)


def _expert_kernel(first_ref, xs_ref, wg_ref, wu_ref, wd_ref, ys_ref,
                   xbuf, ybuf, wg_bf, wu_bf, wd_bf, sem_in, sem_out):
    e = pl.program_id(0)
    bm = EXPERT_BLOCK
    n_blocks = ys_ref.shape[0] // bm
    lo = first_ref[e]
    hi = first_ref[e + 1]
    nused = first_ref[N_EXPERTS]

    def rows(b):
        return pl.ds(pl.multiple_of(b * bm, bm), bm)

    def load(b, s):
        return pltpu.make_async_copy(xs_ref.at[rows(b), :], xbuf.at[s], sem_in.at[s])

    def store(b, s):
        return pltpu.make_async_copy(ybuf.at[s], ys_ref.at[rows(b), :], sem_out.at[s])

    @pl.when(e == 0)
    def _():
        for j in range(EXPERT_IN_SLOTS - 1):
            @pl.when(j < nused)
            def _():
                load(j, j).start()

    @pl.when(hi > lo)
    def _():
        wg_bf[...] = wg_ref[...].astype(BF16)
        wu_bf[...] = wu_ref[...].astype(BF16)
        wd_bf[...] = wd_ref[...].astype(BF16)

    def block(b, c):
        load(b, b % EXPERT_IN_SLOTS).wait()
        ahead = b + EXPERT_IN_SLOTS - 1

        @pl.when(ahead < nused)
        def _():
            load(ahead, ahead % EXPERT_IN_SLOTS).start()

        s = b % 2

        @pl.when(b >= 2)
        def _():
            store(b - 2, s).wait()

        xa, xb = _unpack_bf16_pair(xbuf[b % EXPERT_IN_SLOTS])
        xa = xa.astype(BF16)
        xb = xb.astype(BF16)
        g = (jnp.dot(xa, wg_bf[:PACKED, :], preferred_element_type=F32)
             + jnp.dot(xb, wg_bf[PACKED:, :], preferred_element_type=F32))
        u = (jnp.dot(xa, wu_bf[:PACKED, :], preferred_element_type=F32)
             + jnp.dot(xb, wu_bf[PACKED:, :], preferred_element_type=F32))
        h = (_silu(g) * u).astype(BF16)
        ybuf[s] = _pack_bf16_pair(jnp.dot(h, wd_bf[...], preferred_element_type=F32))
        store(b, s).start()
        return c

    lax.fori_loop(lo, hi, block, 0)

    @pl.when(e == pl.num_programs(0) - 1)
    def _():
        @pl.when(nused >= 2)
        def _():
            store(nused - 2, nused % 2).wait()

        @pl.when(nused >= 1)
        def _():
            store(nused - 1, (nused - 1) % 2).wait()

        ybuf[0] = jnp.zeros((bm, PACKED), U32)

        def fill(b, c):
            store(b, 0).start()
            return c

        def drain(b, c):
            store(b, 0).wait()
            return c

        lax.fori_loop(nused, n_blocks, fill, 0)
        lax.fori_loop(nused, n_blocks, drain, 0)


def _expert_call(first_blk, xs, w_e_gate, w_e_up, w_e_down, l):
    n_slots = xs.shape[0]
    bm = EXPERT_BLOCK

    def w_idx(e, first_ref):
        return (l, e, 0, 0)

    grid_spec = pltpu.PrefetchScalarGridSpec(
        num_scalar_prefetch=1,
        grid=(N_EXPERTS,),
        in_specs=[
            pl.BlockSpec(memory_space=pl.ANY),
            pl.BlockSpec((None, None, D_MODEL, EXPERT_FF), w_idx),
            pl.BlockSpec((None, None, D_MODEL, EXPERT_FF), w_idx),
            pl.BlockSpec((None, None, EXPERT_FF, D_MODEL), w_idx),
        ],
        out_specs=pl.BlockSpec(memory_space=pl.ANY),
        scratch_shapes=[pltpu.VMEM((EXPERT_IN_SLOTS, bm, PACKED), U32),
                        pltpu.VMEM((2, bm, PACKED), U32),
                        pltpu.VMEM((D_MODEL, EXPERT_FF), BF16),
                        pltpu.VMEM((D_MODEL, EXPERT_FF), BF16),
                        pltpu.VMEM((EXPERT_FF, D_MODEL), BF16),
                        pltpu.SemaphoreType.DMA((EXPERT_IN_SLOTS,)),
                        pltpu.SemaphoreType.DMA((2,))],
    )
    return pl.pallas_call(
        _expert_kernel,
        out_shape=jax.ShapeDtypeStruct((n_slots, PACKED), U32),
        grid_spec=grid_spec,
        compiler_params=_params(("arbitrary",)),
        name="experts",
    )(first_blk, xs, w_e_gate, w_e_up, w_e_down)


def _combine_kernel(slots_ref, slots_next_ref, gw_ref, x1_ref, h2_ref, mod_ref, wsg_ref, wsu_ref, wsd_ref, ys_ref,
                    o_ref, buf, sems):
    i = pl.program_id(0)
    n_tok = x1_ref.shape[0]
    half = i % 2

    def issue(idx_ref, dst_half):
        def body(t, c):
            for k in range(TOP_K):
                slot = idx_ref[t * TOP_K + k]
                pltpu.make_async_copy(ys_ref.at[pl.ds(slot, 1), :], buf.at[dst_half, k, pl.ds(t, 1), :],
                                      sems.at[dst_half]).start(priority=k % 2)
            return c
        lax.fori_loop(0, n_tok, body, 0)

    @pl.when(i == 0)
    def _():
        issue(slots_ref, 0)

    has_next = i + 1 < pl.num_programs(0)

    @pl.when(jnp.logical_and(has_next, half == 0))
    def _():
        issue(slots_next_ref, 1)

    @pl.when(jnp.logical_and(has_next, half == 1))
    def _():
        issue(slots_next_ref, 0)

    ha, hb = _unpack_bf16_pair(h2_ref[...])
    ha = ha.astype(BF16)
    hb = hb.astype(BF16)
    gate = (jnp.dot(ha, wsg_ref[:PACKED, :], preferred_element_type=F32)
            + jnp.dot(hb, wsg_ref[PACKED:, :], preferred_element_type=F32))
    up = (jnp.dot(ha, wsu_ref[:PACKED, :], preferred_element_type=F32)
          + jnp.dot(hb, wsu_ref[PACKED:, :], preferred_element_type=F32))
    acc = jnp.dot((_silu(gate) * up).astype(BF16), wsd_ref[...], preferred_element_type=F32)
    acc_a = acc[:, :PACKED]
    acc_b = acc[:, PACKED:]

    for k in range(TOP_K):
        pltpu.make_async_copy(ys_ref.at[pl.ds(0, n_tok), :], buf.at[half, k], sems.at[half]).wait()
    gw = gw_ref[...]
    for k in range(TOP_K):
        ya, yb = _unpack_bf16_pair(buf[half, k])
        acc_a = acc_a + gw[:, k:k + 1] * ya
        acc_b = acc_b + gw[:, k:k + 1] * yb
    g2 = mod_ref[:, 5 * D_MODEL:6 * D_MODEL]
    o_ref[...] = x1_ref[...] + g2 * jnp.concatenate([acc_a, acc_b], axis=1)


def _combine_call(slots_flat, gw, x1, h2, mod3, ys, l, P, n_ctx, lat_t):
    n = x1.shape[0]
    tm = COMBINE_TILE
    n_ctx_tiles = n_ctx // tm
    tiles_per_seq = lat_t // tm

    def mod_idx(i):
        return (jnp.where(i < n_ctx_tiles, 0, 1 + (i - n_ctx_tiles) // tiles_per_seq), 0, 0)

    const = lambda i: (0, 0)
    row = lambda i: (i, 0)
    n_tiles = n // tm
    return pl.pallas_call(
        _combine_kernel,
        out_shape=jax.ShapeDtypeStruct((n, D_MODEL), F32),
        grid=(n_tiles,),
        in_specs=[
            pl.BlockSpec((tm * TOP_K,), lambda i: (i,), memory_space=pltpu.SMEM),
            pl.BlockSpec((tm * TOP_K,), lambda i: (jnp.minimum(i + 1, n_tiles - 1),), memory_space=pltpu.SMEM),
            pl.BlockSpec((tm, TOP_K), row),
            pl.BlockSpec((tm, D_MODEL), row),
            pl.BlockSpec((tm, PACKED), row),
            pl.BlockSpec((None, 1, 6 * D_MODEL), mod_idx),
            pl.BlockSpec((D_MODEL, EXPERT_FF), const),
            pl.BlockSpec((D_MODEL, EXPERT_FF), const),
            pl.BlockSpec((EXPERT_FF, D_MODEL), const),
            pl.BlockSpec(memory_space=pl.ANY),
        ],
        out_specs=pl.BlockSpec((tm, D_MODEL), row),
        scratch_shapes=[pltpu.VMEM((2, TOP_K, tm, PACKED), U32),
                        pltpu.SemaphoreType.DMA((2,))],
        compiler_params=_params(("arbitrary",)),
        name="combine",
    )(slots_flat, slots_flat, gw, x1, h2, mod3, P["w_s_gate_bf16"][l], P["w_s_up_bf16"][l],
      P["w_s_down_bf16"][l], ys)


def _rope_tables(lat_t):
    rows = lat_t // GRID_W
    row = jnp.repeat(jnp.arange(rows, dtype=F32), GRID_W)
    col = jnp.tile(jnp.arange(GRID_W, dtype=F32), rows)
    n_freq = HEAD_DIM // 4
    inv = jnp.power(ROPE_BASE, -jnp.arange(n_freq, dtype=F32) / n_freq)
    ang = jnp.concatenate([row[:, None] * inv, col[:, None] * inv], axis=-1)
    cos = jnp.repeat(jnp.cos(ang), 2, axis=-1)
    sign = jnp.tile(jnp.array([-1.0, 1.0], F32), HEAD_DIM // 2)
    sin = jnp.repeat(jnp.sin(ang), 2, axis=-1) * sign
    cos = jnp.concatenate([cos, jnp.ones((TOKEN_TILE, HEAD_DIM), F32)], axis=0)
    sin = jnp.concatenate([sin, jnp.zeros((TOKEN_TILE, HEAD_DIM), F32)], axis=0)
    reps = LANES // HEAD_DIM
    return jnp.tile(cos, (1, reps)), jnp.tile(sin, (1, reps))


def _block_diag_gates(w_a, w_x):
    def dense(w):
        eye = jnp.eye(LRU_HEADS, dtype=w.dtype)
        return jnp.einsum("ncd,nm->ncmd", w, eye).reshape(LRU_WIDTH, LRU_WIDTH)
    return jnp.concatenate([dense(w_a[0]), dense(w_x[0]), dense(w_a[1]), dense(w_x[1])], axis=1)


def _prepare(P):
    Q = dict(P)
    Q["w_in_bf16"] = P["w_in"].astype(BF16)
    Q["w_out_bf16"] = P["w_out"].astype(BF16)
    Q["q_norm_g_t"] = jnp.tile(P["q_norm_g"], (1, N_HEADS))
    Q["k_norm_g_t"] = jnp.tile(P["k_norm_g"], (1, N_KV_HEADS))
    head = jnp.arange(ATTN_WIDTH) // HEAD_DIM
    Q["head_blockdiag"] = (head[:, None] == head[None, :]).astype(BF16)
    Q["lru_gate_w"] = jnp.stack([_block_diag_gates(P["lru_w_a"][l], P["lru_w_x"][l])
                                 for l in range(DEPTH)]).astype(BF16)
    Q["lru_gate_b"] = jnp.concatenate([P["lru_b_a"][:, 0], P["lru_b_x"][:, 0],
                                       P["lru_b_a"][:, 1], P["lru_b_x"][:, 1]], axis=-1)
    w_hi = P["w_router"].astype(BF16)
    Q["w_router_hi"] = w_hi
    Q["w_router_lo"] = (P["w_router"] - w_hi.astype(F32)).astype(BF16)
    Q["w_s_gate_bf16"] = P["w_s_gate"].astype(BF16)
    Q["w_s_up_bf16"] = P["w_s_up"].astype(BF16)
    Q["w_s_down_bf16"] = P["w_s_down"].astype(BF16)
    return Q


def _group_tables(counts):
    bm = EXPERT_BLOCK
    blocks = (counts.astype(I32) + bm - 1) // bm
    first_blk = jnp.concatenate([jnp.zeros((1,), I32), jnp.cumsum(blocks).astype(I32)])
    group_start = (first_blk[:N_EXPERTS] * bm).astype(F32)[None, :]
    return group_start, first_blk


def _trunk(x_prompt, x_sample, c, cache_k, cache_v, state_lru, c_ctx, P):
    n_seq_c, ctx_t, _ = x_prompt.shape
    n_seq_l, lat_t, _ = x_sample.shape
    past = cache_k.shape[2]
    n_ctx = n_seq_c * ctx_t
    n_lat = n_seq_l * lat_t
    n = n_ctx + n_lat
    assert n_ctx % lat_t == 0, "latent sequences must start on a whole-sequence boundary of the merged token axis"
    lat_off = n_ctx // lat_t
    P = _prepare(P)

    n_cond = -(-(1 + n_seq_l) // SUBLANES) * SUBLANES
    cond = jnp.concatenate([c_ctx[None, :], c, jnp.zeros((n_cond - 1 - n_seq_l, D_MODEL), F32)], axis=0)
    mods = _mod_call(cond, P["w_mod"], P["b_mod"])
    rope_cos, rope_sin = _rope_tables(lat_t)
    tri = (jnp.arange(TOKEN_TILE)[:, None] > jnp.arange(TOKEN_TILE)[None, :]).astype(BF16)
    n_blocks = n * TOP_K // EXPERT_BLOCK + N_EXPERTS
    n_slots = n_blocks * EXPERT_BLOCK

    x = jnp.concatenate([x_prompt.reshape(n_ctx, D_MODEL), x_sample.reshape(n_lat, D_MODEL)], axis=0)
    ks, vs, ss = [], [], []
    for l in range(DEPTH):
        mod3 = mods[l].reshape(n_cond, 1, 6 * D_MODEL)
        q, kr, vb, kc, vc, u, gate, cv = _premix_call(x, mod3, l, P, rope_cos, rope_sin, n_ctx, lat_t)
        ks.append(kc[:n_ctx].reshape(n_seq_c, ctx_t, N_KV_HEADS, HEAD_DIM))
        vs.append(vc[:n_ctx].reshape(n_seq_c, ctx_t, N_KV_HEADS, HEAD_DIM))

        as_ctx = lambda a: a.reshape(n // ctx_t, ctx_t, a.shape[-1])
        as_lat = lambda a: a.reshape(n // lat_t, lat_t, a.shape[-1])

        attn_c = _attn_call(as_ctx(q), as_ctx(kr), as_ctx(vb), min(ctx_t, 256), n_seq_c, 0, 0)
        k_all = jnp.concatenate([kr[n_ctx:].reshape(n_seq_l, lat_t, KV_WIDTH),
                                 cache_k[:, l].reshape(n_seq_l, past, KV_WIDTH).astype(BF16)], axis=1)
        cv_ones = jnp.concatenate([cache_v[:, l], jnp.ones_like(cache_v[:, l])], axis=-1)
        v_all = jnp.concatenate([vb[n_ctx:].reshape(n_seq_l, lat_t, 2 * KV_WIDTH),
                                 cv_ones.reshape(n_seq_l, past, 2 * KV_WIDTH).astype(BF16)], axis=1)
        attn_l = _attn_call(as_lat(q), k_all, v_all, 256, n_seq_l, lat_off, 0)

        lru_c, fin_c = _lru_call(as_ctx(u), as_ctx(gate), jnp.zeros((n_seq_c, 2, LRU_WIDTH), F32), l, P,
                                 n_seq_c, 0)
        lru_l, _ = _lru_call(as_lat(u), as_lat(gate), state_lru[:, l], l, P, n_seq_l, lat_off)
        ss.append(fin_c)
        conv_c = _convmod_call(as_ctx(cv), l, P, n_seq_c, 0)
        conv_l = _convmod_call(as_lat(cv), l, P, n_seq_l, lat_off)

        x1, h2, logits = _postmix_call(
            attn_c.reshape(n_ctx, ATTN_WIDTH), attn_l.reshape(n_lat, ATTN_WIDTH),
            lru_c.reshape(n_ctx, LRU_WIDTH), lru_l.reshape(n_lat, LRU_WIDTH),
            conv_c.reshape(n_ctx, CONV_WIDTH), conv_l.reshape(n_lat, CONV_WIDTH),
            x, mod3, l, P, n_ctx, lat_t)

        ids, gw, rank, counts = _route_call(logits, P["b_router"][l], tri)
        group_start, first_blk = _group_tables(counts[0])
        slots = _slots_call(ids, rank, group_start).reshape(-1)
        xs = _dispatch_call(first_blk, slots, h2, n_slots)
        ys = _expert_call(first_blk, xs, P["w_e_gate"], P["w_e_up"], P["w_e_down"], l)
        x = _combine_call(slots, gw, x1, h2, mod3, ys, l, P, n_ctx, lat_t)

    y_prompt = x[:n_ctx].reshape(n_seq_c, ctx_t, D_MODEL)
    y_sample = x[n_ctx:].reshape(n_seq_l, lat_t, D_MODEL)
    return (y_prompt, y_sample, jnp.stack(ks, axis=1), jnp.stack(vs, axis=1), jnp.stack(ss, axis=1))


def kernel(x_prompt, x_sample, c, cache_k, cache_v, state_lru, c_ctx, w_mod, b_mod, norm1_g, w_in, q_norm_g, k_norm_g, lru_conv_w, lru_conv_b, lru_w_a, lru_b_a, lru_w_x, lru_b_x, lru_lambda, cm_dw_w, cm_dw_b, cm_ln_g, cm_ln_b, out_norm_g, w_out, norm2_g, w_router, b_router, w_e_gate, w_e_up, w_e_down, w_s_gate, w_s_up, w_s_down):
    P = {"w_mod": w_mod, "b_mod": b_mod, "norm1_g": norm1_g, "w_in": w_in, "q_norm_g": q_norm_g,
         "k_norm_g": k_norm_g, "lru_conv_w": lru_conv_w, "lru_conv_b": lru_conv_b, "lru_w_a": lru_w_a,
         "lru_b_a": lru_b_a, "lru_w_x": lru_w_x, "lru_b_x": lru_b_x, "lru_lambda": lru_lambda,
         "cm_dw_w": cm_dw_w, "cm_dw_b": cm_dw_b, "cm_ln_g": cm_ln_g, "cm_ln_b": cm_ln_b,
         "out_norm_g": out_norm_g, "w_out": w_out, "norm2_g": norm2_g, "w_router": w_router,
         "b_router": b_router, "w_e_gate": w_e_gate, "w_e_up": w_e_up, "w_e_down": w_e_down,
         "w_s_gate": w_s_gate, "w_s_up": w_s_up, "w_s_down": w_s_down}
    return _trunk(x_prompt, x_sample, c, cache_k, cache_v, state_lru, c_ctx, P)
```

```python
import functools
import math

import jax
import jax.numpy as jnp
from jax import lax
from jax.experimental import pallas as pl
from jax.experimental.pallas import tpu as pltpu

F32 = jnp.float32
BF16 = jnp.bfloat16
I32 = jnp.int32

D_MODEL = 1024
DEPTH = 2
GRID_W = 64
ATTN_WIDTH = 512
LRU_WIDTH = 256
CONV_WIDTH = 256
HEAD_DIM = 64
N_HEADS = 8
N_KV_HEADS = 2
KV_GROUP = N_HEADS // N_KV_HEADS
KV_WIDTH = N_KV_HEADS * HEAD_DIM
ROPE_BASE = 10000.0
LRU_HEADS = 4
LRU_HEAD_DIM = LRU_WIDTH // LRU_HEADS
LRU_CONV = 4
RG_C = 8.0
CM_KERNEL = 31
N_EXPERTS = 256
TOP_K = 8
EXPERT_FF = 256
ROUTED_SCALE = 2.5
EPS = 1e-6
IN_WIDTH = ATTN_WIDTH + 2 * KV_WIDTH + 2 * LRU_WIDTH + 2 * CONV_WIDTH

SUBLANES = 8
LANES = 128
VMEM_LIMIT = 56 * 1024 * 1024

TOKEN_TILE = 256
SCAN_CHUNK = 256
CONV_CHUNK = 128
CONV_HALO = 16
LRU_HALO = 8
EXPERT_BLOCK = 256
EXPERT_IN_SLOTS = 4
COMBINE_TILE = 256
DISPATCH_TILE = 512


def _params(sem):
    return pltpu.CompilerParams(dimension_semantics=sem, vmem_limit_bytes=VMEM_LIMIT)


def _sigmoid(x):
    return 1.0 / (1.0 + jnp.exp(-x))


def _silu(x):
    return x * _sigmoid(x)


def _bdot(a, b):
    return jnp.dot(a.astype(BF16), b.astype(BF16), preferred_element_type=F32)


def _split_dot(a, b_bf16):
    hi = a.astype(BF16)
    lo = (a - hi.astype(F32)).astype(BF16)
    return (jnp.dot(hi, b_bf16, preferred_element_type=F32)
            + jnp.dot(lo, b_bf16, preferred_element_type=F32))


U32 = jnp.uint32
PACKED = D_MODEL // 2


def _pack_bf16_pair(x):
    hi = lax.bitcast_convert_type(x[:, :PACKED].astype(BF16).astype(F32), U32)
    lo = lax.bitcast_convert_type(x[:, PACKED:].astype(BF16).astype(F32), U32)
    return (hi & jnp.uint32(0xFFFF0000)) | (lo >> 16)


def _unpack_bf16_pair(w):
    hi = lax.bitcast_convert_type(w & jnp.uint32(0xFFFF0000), F32)
    lo = lax.bitcast_convert_type(w << 16, F32)
    return hi, lo


ROW_SUB = PACKED // LANES


def _store_rows(ref, w):
    r = w.shape[0]
    for j in range(ROW_SUB):
        ref[pl.ds(j, r, stride=ROW_SUB), :] = w[:, j * LANES:(j + 1) * LANES]


def _load_rows(ref, r):
    his, los = [], []
    for j in range(ROW_SUB):
        hi, lo = _unpack_bf16_pair(ref[pl.ds(j, r, stride=ROW_SUB), :])
        his.append(hi)
        los.append(lo)
    return jnp.concatenate(his, axis=1), jnp.concatenate(los, axis=1)


def _rms(x, g):
    return x * lax.rsqrt(jnp.mean(x * x, axis=-1, keepdims=True) + EPS) * g


MOD_COLS = 1536


def _mod_kernel(c_ref, w_ref, b_ref, o_ref):
    a = _silu(c_ref[...])
    o_ref[...] = jnp.dot(a, w_ref[...], preferred_element_type=F32,
                         precision=lax.Precision.HIGHEST) + b_ref[...]


def _mod_call(cond, w_mod, b_mod):
    n_cond = cond.shape[0]
    width = 6 * D_MODEL
    return pl.pallas_call(
        _mod_kernel,
        out_shape=jax.ShapeDtypeStruct((DEPTH, n_cond, width), F32),
        grid=(DEPTH, width // MOD_COLS),
        in_specs=[
            pl.BlockSpec((n_cond, D_MODEL), lambda l, j: (0, 0)),
            pl.BlockSpec((None, D_MODEL, MOD_COLS), lambda l, j: (l, 0, j)),
            pl.BlockSpec((None, 1, MOD_COLS), lambda l, j: (l, 0, j)),
        ],
        out_specs=pl.BlockSpec((None, n_cond, MOD_COLS), lambda l, j: (l, 0, j)),
        compiler_params=_params(("arbitrary", "arbitrary")),
        name="mod",
    )(cond, w_mod, b_mod.reshape(DEPTH, 1, width))


def _swap_pairs(x):
    n = x.shape[-1]
    lane = lax.broadcasted_iota(I32, x.shape, 1)
    nxt = pltpu.roll(x, n - 1, 1)
    prv = pltpu.roll(x, 1, 1)
    return jnp.where(lane % 2 == 0, nxt, prv)


def _premix_kernel(x_ref, mod_ref, n1g_ref, w_ref, qg_ref, kg_ref, bd_ref, cos_ref, sin_ref,
                   q_ref, kr_ref, vb_ref, kc_ref, vc_ref, u_ref, gate_ref, cv_ref):
    x = x_ref[...]
    sh1 = mod_ref[:, 0:D_MODEL]
    sc1 = mod_ref[:, D_MODEL:2 * D_MODEL]
    h = _rms(x, n1g_ref[...]) * (1.0 + sc1) + sh1
    z = jnp.dot(h.astype(BF16), w_ref[...], preferred_element_type=F32)
    o = 0
    q = z[:, o:o + ATTN_WIDTH]; o += ATTN_WIDTH
    k = z[:, o:o + KV_WIDTH]; o += KV_WIDTH
    v = z[:, o:o + KV_WIDTH]; o += KV_WIDTH
    u_ref[...] = z[:, o:o + LRU_WIDTH]; o += LRU_WIDTH
    gate_ref[...] = z[:, o:o + LRU_WIDTH]; o += LRU_WIDTH
    cv_ref[...] = z[:, o:o + 2 * CONV_WIDTH]

    bd = bd_ref[...]
    inv_hd = 1.0 / HEAD_DIM
    q_ms = _split_dot(q * q, bd) * inv_hd
    k_ms = _split_dot(k * k, bd[:KV_WIDTH, :KV_WIDTH]) * inv_hd
    qn = q * lax.rsqrt(q_ms + EPS) * qg_ref[...]
    kn = k * lax.rsqrt(k_ms + EPS) * kg_ref[...]
    kc_ref[...] = kn
    vc_ref[...] = v
    ones = jnp.ones((v.shape[0], HEAD_DIM), F32)
    vb_ref[...] = jnp.concatenate([v[:, :HEAD_DIM], ones, v[:, HEAD_DIM:], ones], axis=1).astype(BF16)

    cos = cos_ref[...]
    sin = sin_ref[...]
    reps = ATTN_WIDTH // cos.shape[-1]
    cos_q = jnp.concatenate([cos] * reps, axis=1)
    sin_q = jnp.concatenate([sin] * reps, axis=1)
    qr = qn * cos_q + _swap_pairs(qn) * sin_q
    kr = kn * cos + _swap_pairs(kn) * sin
    q_ref[...] = (qr * (HEAD_DIM ** -0.5 * math.log2(math.e))).astype(BF16)
    kr_ref[...] = kr.astype(BF16)


def _premix_call(x, mod3, l, P, rope_cos, rope_sin, n_ctx, lat_t):
    n = x.shape[0]
    tm = TOKEN_TILE
    n_ctx_tiles = n_ctx // tm
    tiles_per_seq = lat_t // tm

    def mod_idx(i):
        return (jnp.where(i < n_ctx_tiles, 0, 1 + (i - n_ctx_tiles) // tiles_per_seq), 0, 0)

    def rope_idx(i):
        return (jnp.where(i < n_ctx_tiles, tiles_per_seq, (i - n_ctx_tiles) % tiles_per_seq), 0)

    const = lambda i: (0, 0)
    row = lambda i: (i, 0)
    outs = pl.pallas_call(
        _premix_kernel,
        out_shape=(
            jax.ShapeDtypeStruct((n, ATTN_WIDTH), BF16),
            jax.ShapeDtypeStruct((n, KV_WIDTH), BF16),
            jax.ShapeDtypeStruct((n, 2 * KV_WIDTH), BF16),
            jax.ShapeDtypeStruct((n, KV_WIDTH), F32),
            jax.ShapeDtypeStruct((n, KV_WIDTH), F32),
            jax.ShapeDtypeStruct((n, LRU_WIDTH), F32),
            jax.ShapeDtypeStruct((n, LRU_WIDTH), F32),
            jax.ShapeDtypeStruct((n, 2 * CONV_WIDTH), F32),
        ),
        grid=(n // tm,),
        in_specs=[
            pl.BlockSpec((tm, D_MODEL), row),
            pl.BlockSpec((None, 1, 6 * D_MODEL), mod_idx),
            pl.BlockSpec((1, D_MODEL), const),
            pl.BlockSpec((D_MODEL, IN_WIDTH), const),
            pl.BlockSpec((1, ATTN_WIDTH), const),
            pl.BlockSpec((1, KV_WIDTH), const),
            pl.BlockSpec((ATTN_WIDTH, ATTN_WIDTH), const),
            pl.BlockSpec((tm, LANES), rope_idx),
            pl.BlockSpec((tm, LANES), rope_idx),
        ],
        out_specs=(
            pl.BlockSpec((tm, ATTN_WIDTH), row),
            pl.BlockSpec((tm, KV_WIDTH), row),
            pl.BlockSpec((tm, 2 * KV_WIDTH), row),
            pl.BlockSpec((tm, KV_WIDTH), row),
            pl.BlockSpec((tm, KV_WIDTH), row),
            pl.BlockSpec((tm, LRU_WIDTH), row),
            pl.BlockSpec((tm, LRU_WIDTH), row),
            pl.BlockSpec((tm, 2 * CONV_WIDTH), row),
        ),
        compiler_params=_params(("parallel",)),
        name="premix",
    )(x, mod3, P["norm1_g"][l][None, :], P["w_in_bf16"][l], P["q_norm_g_t"][l][None, :],
      P["k_norm_g_t"][l][None, :], P["head_blockdiag"], rope_cos, rope_sin)
    return outs


def _attn_kernel(q_ref, k_ref, v_ref, o_ref):
    outs = []
    for kh in range(N_KV_HEADS):
        k = k_ref[:, kh * HEAD_DIM:(kh + 1) * HEAD_DIM]
        v = v_ref[:, kh * 2 * HEAD_DIM:(kh + 1) * 2 * HEAD_DIM]
        for g in range(KV_GROUP):
            hd = kh * KV_GROUP + g
            q = q_ref[:, hd * HEAD_DIM:(hd + 1) * HEAD_DIM]
            s = lax.dot_general(q, k, (((1,), (1,)), ((), ())), preferred_element_type=F32)
            m = jnp.max(s, axis=-1, keepdims=True)
            p = jnp.exp2(s - m).astype(BF16)
            o = jnp.dot(p, v, preferred_element_type=F32)
            outs.append(o[:, :HEAD_DIM] / o[:, HEAD_DIM:HEAD_DIM + 1])
    o_ref[...] = jnp.concatenate(outs, axis=1)


def _attn_call(q3, k3, v3, tq, b, q_off, kv_off):
    s = q3.shape[1]
    t = k3.shape[1]
    return pl.pallas_call(
        _attn_kernel,
        out_shape=jax.ShapeDtypeStruct((b, s, ATTN_WIDTH), F32),
        grid=(b, s // tq),
        in_specs=[
            pl.BlockSpec((None, tq, ATTN_WIDTH), lambda i, j: (i + q_off, j, 0)),
            pl.BlockSpec((None, t, KV_WIDTH), lambda i, j: (i + kv_off, 0, 0)),
            pl.BlockSpec((None, t, 2 * KV_WIDTH), lambda i, j: (i + kv_off, 0, 0)),
        ],
        out_specs=pl.BlockSpec((None, tq, ATTN_WIDTH), lambda i, j: (i, j, 0)),
        compiler_params=_params(("parallel", "parallel")),
        name="attention",
    )(q3, k3, v3)


def _chunk_scan(a, b, reverse):
    n = a.shape[0]
    row = lax.broadcasted_iota(I32, a.shape, 0)
    d = 1
    while d < n:
        if reverse:
            a_s = pltpu.roll(a, n - d, 0)
            b_s = pltpu.roll(b, n - d, 0)
            ok = row < n - d
        else:
            a_s = pltpu.roll(a, d, 0)
            b_s = pltpu.roll(b, d, 0)
            ok = row >= d
        b = jnp.where(ok, a * b_s + b, b)
        a = jnp.where(ok, a * a_s, a)
        d *= 2
    return a, b


def _gelu_tanh(x):
    return 0.5 * x * (1.0 + jnp.tanh(math.sqrt(2.0 / math.pi) * (x + 0.044715 * (x * x * x))))


def _lru_kernel(u_ref, gate_ref, cw_ref, cb_ref, wg_ref, bg_ref, lam_ref, h0_ref,
                y_ref, fin_ref, upad, xc_scr, fwd_scr):
    t_len = u_ref.shape[0]
    ch = SCAN_CHUNK
    n_chunks = t_len // ch
    w = LRU_WIDTH
    zeros_halo = jnp.zeros((LRU_HALO, w), F32)
    upad[0:LRU_HALO, :] = zeros_halo
    upad[t_len + LRU_HALO:t_len + 2 * LRU_HALO, :] = zeros_halo

    def fill(c, carry):
        r0 = pl.multiple_of(c * ch, ch)
        upad[pl.ds(r0 + LRU_HALO, ch), :] = u_ref[pl.ds(r0, ch), :]
        return carry

    lax.fori_loop(0, n_chunks, fill, 0)

    lam = lam_ref[...]
    nlam = -lam
    softplus = jnp.maximum(nlam, 0.0) + jnp.log(1.0 + jnp.exp(-jnp.abs(nlam)))
    decay = -RG_C * softplus
    cw = cw_ref[...]
    cb = cb_ref[...]

    def gates(xc, d):
        z = jnp.dot(xc.astype(BF16), wg_ref[:, 2 * d * w:2 * (d + 1) * w],
                    preferred_element_type=F32) + bg_ref[:, 2 * d * w:2 * (d + 1) * w]
        r = _sigmoid(z[:, :w])
        i = _sigmoid(z[:, w:])
        log_a = decay[d:d + 1, :] * r
        a = jnp.exp(log_a)
        b = jnp.sqrt(-jnp.tanh(log_a) * (a * a + 1.0)) * (i * xc)
        return a, b

    def fwd(c, h):
        r0 = pl.multiple_of(c * ch, ch)
        win = upad[pl.ds(r0, ch + 2 * LRU_HALO), :]
        xc = cb
        for j in range(LRU_CONV):
            s0 = LRU_HALO - (LRU_CONV - 1) // 2 + j
            xc = xc + cw[j:j + 1, :] * win[s0:s0 + ch, :]
        xc_scr[pl.ds(r0, ch), :] = xc
        a, b = gates(xc, 0)
        a_cum, b_cum = _chunk_scan(a, b, reverse=False)
        hs = a_cum * h + b_cum
        fwd_scr[pl.ds(r0, ch), :] = hs
        return hs[ch - 1:ch, :]

    h_f = lax.fori_loop(0, n_chunks, fwd, h0_ref[0:1, :])

    def bwd(ci, h):
        c = n_chunks - 1 - ci
        r0 = pl.multiple_of(c * ch, ch)
        xc = xc_scr[pl.ds(r0, ch), :]
        a, b = gates(xc, 1)
        a_cum, b_cum = _chunk_scan(a, b, reverse=True)
        hs = a_cum * h + b_cum
        y_ref[pl.ds(r0, ch), :] = (fwd_scr[pl.ds(r0, ch), :] + hs) * _gelu_tanh(gate_ref[pl.ds(r0, ch), :])
        return hs[0:1, :]

    h_b = lax.fori_loop(0, n_chunks, bwd, h0_ref[1:2, :])
    fin_ref[0:1, :] = h_f
    fin_ref[1:2, :] = h_b


def _lru_call(u3, gate3, h0, l, P, b, seq_off):
    _, t, w = u3.shape
    const = lambda i: (0, 0)
    seq = lambda i: (i, 0, 0)
    seq_in = lambda i: (i + seq_off, 0, 0)
    return pl.pallas_call(
        _lru_kernel,
        out_shape=(jax.ShapeDtypeStruct((b, t, w), F32),
                   jax.ShapeDtypeStruct((b, 2, w), F32)),
        grid=(b,),
        in_specs=[
            pl.BlockSpec((None, t, w), seq_in),
            pl.BlockSpec((None, t, w), seq_in),
            pl.BlockSpec((LRU_CONV, w), const),
            pl.BlockSpec((1, w), const),
            pl.BlockSpec((w, 4 * w), const),
            pl.BlockSpec((1, 4 * w), const),
            pl.BlockSpec((2, w), const),
            pl.BlockSpec((None, 2, w), seq),
        ],
        out_specs=(pl.BlockSpec((None, t, w), seq),
                   pl.BlockSpec((None, 2, w), seq)),
        scratch_shapes=[pltpu.VMEM((t + 2 * LRU_HALO, w), F32),
                        pltpu.VMEM((t, w), F32),
                        pltpu.VMEM((t, w), F32)],
        compiler_params=_params(("parallel",)),
        name="rglru",
    )(u3, gate3, P["lru_conv_w"][l], P["lru_conv_b"][l][None, :], P["lru_gate_w"][l],
      P["lru_gate_b"][l][None, :], P["lru_lambda"][l], h0)


def _convmod_kernel(cv_ref, w_ref, b_ref, g_ref, beta_ref, o_ref, hpad):
    t_len = cv_ref.shape[0]
    ch = CONV_CHUNK
    n_chunks = t_len // ch
    w = CONV_WIDTH
    zeros_halo = jnp.zeros((CONV_HALO, w), F32)
    hpad[0:CONV_HALO, :] = zeros_halo
    hpad[t_len + CONV_HALO:t_len + 2 * CONV_HALO, :] = zeros_halo

    def glu(c, carry):
        r0 = pl.multiple_of(c * ch, ch)
        blk = cv_ref[pl.ds(r0, ch), :]
        hpad[pl.ds(r0 + CONV_HALO, ch), :] = blk[:, :w] * _sigmoid(blk[:, w:])
        return carry

    lax.fori_loop(0, n_chunks, glu, 0)

    taps = w_ref[...]
    bias = b_ref[...]
    gamma = g_ref[...]
    beta = beta_ref[...]

    def conv(c, carry):
        r0 = pl.multiple_of(c * ch, ch)
        win = hpad[pl.ds(r0, ch + 2 * CONV_HALO), :]
        acc = bias
        for j in range(CM_KERNEL):
            s0 = CONV_HALO - CM_KERNEL // 2 + j
            acc = acc + taps[j:j + 1, :] * win[s0:s0 + ch, :]
        mu = jnp.mean(acc, axis=-1, keepdims=True)
        cen = acc - mu
        var = jnp.mean(cen * cen, axis=-1, keepdims=True)
        y = cen * lax.rsqrt(var + EPS) * gamma + beta
        o_ref[pl.ds(r0, ch), :] = _silu(y)
        return carry

    lax.fori_loop(0, n_chunks, conv, 0)


def _convmod_call(cv3, l, P, b, seq_off):
    t = cv3.shape[1]
    w = CONV_WIDTH
    const = lambda i: (0, 0)
    seq = lambda i: (i, 0, 0)
    return pl.pallas_call(
        _convmod_kernel,
        out_shape=jax.ShapeDtypeStruct((b, t, w), F32),
        grid=(b,),
        in_specs=[
            pl.BlockSpec((None, t, 2 * w), lambda i: (i + seq_off, 0, 0)),
            pl.BlockSpec((CM_KERNEL, w), const),
            pl.BlockSpec((1, w), const),
            pl.BlockSpec((1, w), const),
            pl.BlockSpec((1, w), const),
        ],
        out_specs=pl.BlockSpec((None, t, w), seq),
        scratch_shapes=[pltpu.VMEM((t + 2 * CONV_HALO, w), F32)],
        compiler_params=_params(("parallel",)),
        name="convmod",
    )(cv3, P["cm_dw_w"][l], P["cm_dw_b"][l][None, :], P["cm_ln_g"][l][None, :], P["cm_ln_b"][l][None, :])


def _postmix_kernel(n_ctx_tiles, attn_c_ref, attn_l_ref, lru_c_ref, lru_l_ref, conv_c_ref, conv_l_ref,
                    x_ref, mod_ref, og_ref, wo_ref, n2g_ref, wr_hi_ref, wr_lo_ref,
                    x1_ref, h2_ref, lg_ref):
    is_ctx = pl.program_id(0) < n_ctx_tiles
    og = og_ref[...]
    a0, a1, a2 = ATTN_WIDTH, ATTN_WIDTH + LRU_WIDTH, D_MODEL
    y = jnp.concatenate([_rms(jnp.where(is_ctx, attn_c_ref[...], attn_l_ref[...]), og[:, :a0]),
                         _rms(jnp.where(is_ctx, lru_c_ref[...], lru_l_ref[...]), og[:, a0:a1]),
                         _rms(jnp.where(is_ctx, conv_c_ref[...], conv_l_ref[...]), og[:, a1:a2])], axis=1)
    y = jnp.dot(y.astype(BF16), wo_ref[...], preferred_element_type=F32)
    g1 = mod_ref[:, 2 * D_MODEL:3 * D_MODEL]
    sh2 = mod_ref[:, 3 * D_MODEL:4 * D_MODEL]
    sc2 = mod_ref[:, 4 * D_MODEL:5 * D_MODEL]
    x1 = x_ref[...] + g1 * y
    x1_ref[...] = x1
    h2 = _rms(x1, n2g_ref[...]) * (1.0 + sc2) + sh2
    _store_rows(h2_ref, _pack_bf16_pair(h2))
    hi = h2.astype(BF16)
    lo = (h2 - hi.astype(F32)).astype(BF16)
    w_hi = wr_hi_ref[...]
    lg_ref[...] = (jnp.dot(hi, w_hi, preferred_element_type=F32)
                   + jnp.dot(lo, w_hi, preferred_element_type=F32)
                   + jnp.dot(hi, wr_lo_ref[...], preferred_element_type=F32))


def _postmix_call(attn_c, attn_l, lru_c, lru_l, conv_c, conv_l, x, mod3, l, P, n_ctx, lat_t):
    n = x.shape[0]
    tm = TOKEN_TILE
    n_ctx_tiles = n_ctx // tm
    tiles_per_seq = lat_t // tm

    def mod_idx(i):
        return (jnp.where(i < n_ctx_tiles, 0, 1 + (i - n_ctx_tiles) // tiles_per_seq), 0, 0)

    const = lambda i: (0, 0)
    row = lambda i: (i, 0)
    row_c = lambda i: (jnp.minimum(i, n_ctx_tiles - 1), 0)
    row_l = lambda i: (jnp.maximum(i - n_ctx_tiles, 0), 0)
    return pl.pallas_call(
        functools.partial(_postmix_kernel, n_ctx_tiles),
        out_shape=(jax.ShapeDtypeStruct((n, D_MODEL), F32),
                   jax.ShapeDtypeStruct((ROW_SUB * n, LANES), U32),
                   jax.ShapeDtypeStruct((n, N_EXPERTS), F32)),
        grid=(n // tm,),
        in_specs=[
            pl.BlockSpec((tm, ATTN_WIDTH), row_c),
            pl.BlockSpec((tm, ATTN_WIDTH), row_l),
            pl.BlockSpec((tm, LRU_WIDTH), row_c),
            pl.BlockSpec((tm, LRU_WIDTH), row_l),
            pl.BlockSpec((tm, CONV_WIDTH), row_c),
            pl.BlockSpec((tm, CONV_WIDTH), row_l),
            pl.BlockSpec((tm, D_MODEL), row),
            pl.BlockSpec((None, 1, 6 * D_MODEL), mod_idx),
            pl.BlockSpec((1, D_MODEL), const),
            pl.BlockSpec((D_MODEL, D_MODEL), const),
            pl.BlockSpec((1, D_MODEL), const),
            pl.BlockSpec((D_MODEL, N_EXPERTS), const),
            pl.BlockSpec((D_MODEL, N_EXPERTS), const),
        ],
        out_specs=(pl.BlockSpec((tm, D_MODEL), row),
                   pl.BlockSpec((ROW_SUB * tm, LANES), row),
                   pl.BlockSpec((tm, N_EXPERTS), row)),
        compiler_params=_params(("parallel",)),
        name="postmix",
    )(attn_c, attn_l, lru_c, lru_l, conv_c, conv_l, x, mod3, P["out_norm_g"][l][None, :], P["w_out_bf16"][l],
      P["norm2_g"][l][None, :], P["w_router_hi"][l], P["w_router_lo"][l])


def _route_kernel(lg_ref, br_ref, tri_ref, ids_ref, gw_ref, rank_ref, cnt_ref, carry):
    i = pl.program_id(0)

    @pl.when(i == 0)
    def _():
        carry[...] = jnp.zeros_like(carry)

    scores = _sigmoid(lg_ref[...])
    sel = scores + br_ref[...]
    tm = scores.shape[0]
    lane = lax.broadcasted_iota(I32, (tm, N_EXPERTS), 1).astype(F32)
    slot_lane = lax.broadcasted_iota(I32, (tm, LANES), 1)
    ids_acc = jnp.zeros((tm, LANES), F32)
    gw_acc = jnp.zeros((tm, LANES), F32)
    hot = jnp.zeros((tm, N_EXPERTS), F32)
    picked = []
    for k in range(TOP_K):
        m = jnp.max(sel, axis=-1, keepdims=True)
        idx = jnp.min(jnp.where(sel == m, lane, float(N_EXPERTS)), axis=-1, keepdims=True)
        one = lane == idx
        g = jnp.sum(jnp.where(one, scores, 0.0), axis=-1, keepdims=True)
        sel = jnp.where(one, -jnp.inf, sel)
        hot = jnp.where(one, 1.0, hot)
        ids_acc = jnp.where(slot_lane == k, idx, ids_acc)
        gw_acc = jnp.where(slot_lane == k, g, gw_acc)
        picked.append(idx)
    denom = jnp.sum(gw_acc, axis=-1, keepdims=True)
    gw_acc = ROUTED_SCALE * gw_acc / denom

    before = jnp.dot(tri_ref[...], hot.astype(BF16), preferred_element_type=F32) + carry[0:1, :]
    rank_acc = jnp.zeros((tm, LANES), F32)
    for k in range(TOP_K):
        r = jnp.sum(jnp.where(lane == picked[k], before, 0.0), axis=-1, keepdims=True)
        rank_acc = jnp.where(slot_lane == k, r, rank_acc)
    carry[0:1, :] = carry[0:1, :] + jnp.sum(hot, axis=0, keepdims=True)
    cnt_ref[...] = carry[...]
    ids_ref[...] = ids_acc[:, :TOP_K].astype(I32)
    gw_ref[...] = gw_acc[:, :TOP_K]
    rank_ref[...] = rank_acc[:, :TOP_K].astype(I32)


def _route_call(logits, b_router_l, tri):
    n = logits.shape[0]
    tm = TOKEN_TILE
    const = lambda i: (0, 0)
    row = lambda i: (i, 0)
    return pl.pallas_call(
        _route_kernel,
        out_shape=(jax.ShapeDtypeStruct((n, TOP_K), I32),
                   jax.ShapeDtypeStruct((n, TOP_K), F32),
                   jax.ShapeDtypeStruct((n, TOP_K), I32),
                   jax.ShapeDtypeStruct((SUBLANES, N_EXPERTS), F32)),
        grid=(n // tm,),
        in_specs=[
            pl.BlockSpec((tm, N_EXPERTS), row),
            pl.BlockSpec((1, N_EXPERTS), const),
            pl.BlockSpec((tm, tm), const),
        ],
        out_specs=(pl.BlockSpec((tm, TOP_K), row),
                   pl.BlockSpec((tm, TOP_K), row),
                   pl.BlockSpec((tm, TOP_K), row),
                   pl.BlockSpec((SUBLANES, N_EXPERTS), const)),
        scratch_shapes=[pltpu.VMEM((SUBLANES, N_EXPERTS), F32)],
        compiler_params=_params(("arbitrary",)),
        name="route",
    )(logits, b_router_l[None, :], tri)


def _slots_kernel(ids_ref, rank_ref, start_ref, slots_ref):
    ids = ids_ref[...].astype(F32)
    tm = ids.shape[0]
    lane = lax.broadcasted_iota(I32, (tm, N_EXPERTS), 1).astype(F32)
    slot_lane = lax.broadcasted_iota(I32, (tm, TOP_K), 1)
    start = start_ref[...]
    acc = jnp.zeros((tm, TOP_K), F32)
    for k in range(TOP_K):
        s = jnp.sum(jnp.where(lane == ids[:, k:k + 1], start, 0.0), axis=-1, keepdims=True)
        acc = jnp.where(slot_lane == k, s, acc)
    slots_ref[...] = (acc.astype(I32) + rank_ref[...]) * ROW_SUB


def _slots_call(ids, rank, group_start):
    n = ids.shape[0]
    tm = TOKEN_TILE
    row = lambda i: (i, 0)
    return pl.pallas_call(
        _slots_kernel,
        out_shape=jax.ShapeDtypeStruct((n, TOP_K), I32),
        grid=(n // tm,),
        in_specs=[pl.BlockSpec((tm, TOP_K), row), pl.BlockSpec((tm, TOP_K), row),
                  pl.BlockSpec((1, N_EXPERTS), lambda i: (0, 0))],
        out_specs=pl.BlockSpec((tm, TOP_K), row),
        compiler_params=_params(("parallel",)),
        name="slots",
    )(ids, rank, group_start)


def _wait_rows(src_ref, dst_ref, sem, n_waits):
    for _ in range(n_waits):
        pltpu.make_async_copy(src_ref, dst_ref, sem).wait()


def _dispatch_kernel(first_ref, slots_ref, h_ref, xs_ref, zeros, sem_rows, sem_zero):
    n_tok = h_ref.shape[0] // ROW_SUB
    bm = EXPERT_BLOCK * ROW_SUB
    n_blocks = xs_ref.shape[0] // bm
    nused = first_ref[N_EXPERTS]

    def zero_copy(b):
        return pltpu.make_async_copy(zeros, xs_ref.at[pl.ds(pl.multiple_of(b * bm, bm), bm), :], sem_zero)

    @pl.when(pl.program_id(0) == 0)
    def _():
        zeros[...] = jnp.zeros_like(zeros)

        def group(start):
            def body(e, c):
                @pl.when(first_ref[e + 1] > first_ref[e])
                def _():
                    cp = zero_copy(first_ref[e + 1] - 1)
                    cp.start() if start else cp.wait()
                return c
            lax.fori_loop(0, N_EXPERTS, body, 0)

        def tail(start):
            def body(b, c):
                cp = zero_copy(b)
                cp.start() if start else cp.wait()
                return c
            lax.fori_loop(nused, n_blocks, body, 0)

        group(True)
        tail(True)
        group(False)
        tail(False)

    def issue(t, c):
        src = h_ref.at[pl.ds(pl.multiple_of(t * ROW_SUB, ROW_SUB), ROW_SUB), :]
        for k in range(TOP_K):
            row0 = pl.multiple_of(slots_ref[t * TOP_K + k], ROW_SUB)
            pltpu.make_async_copy(src, xs_ref.at[pl.ds(row0, ROW_SUB), :], sem_rows).start(priority=k % 2)
        return c

    lax.fori_loop(0, n_tok, issue, 0)
    _wait_rows(h_ref, xs_ref.at[pl.ds(0, n_tok * ROW_SUB), :], sem_rows, TOP_K)


def _dispatch_call(first_blk, slots_flat, h2, n_slots):
    n = h2.shape[0] // ROW_SUB
    tm = DISPATCH_TILE
    grid_spec = pltpu.PrefetchScalarGridSpec(
        num_scalar_prefetch=1,
        grid=(n // tm,),
        in_specs=[
            pl.BlockSpec((tm * TOP_K,), lambda i, first: (i,), memory_space=pltpu.SMEM),
            pl.BlockSpec((ROW_SUB * tm, LANES), lambda i, first: (i, 0)),
        ],
        out_specs=pl.BlockSpec(memory_space=pl.ANY),
        scratch_shapes=[pltpu.VMEM((ROW_SUB * EXPERT_BLOCK, LANES), U32),
                        pltpu.SemaphoreType.DMA,
                        pltpu.SemaphoreType.DMA],
    )
    return pl.pallas_call(
        _dispatch_kernel,
        out_shape=jax.ShapeDtypeStruct((ROW_SUB * n_slots, LANES), U32),
        grid_spec=grid_spec,
        compiler_params=_params(("arbitrary",)),
        name="dispatch",
    )(first_blk, slots_flat, h2)


def _expert_kernel(first_ref, xs_ref, wg_ref, wu_ref, wd_ref, ys_ref,
                   xbuf, ybuf, wg_bf, wu_bf, wd_bf, sem_in, sem_out):
    e = pl.program_id(0)
    bm = EXPERT_BLOCK * ROW_SUB
    n_blocks = ys_ref.shape[0] // bm
    lo = first_ref[e]
    hi = first_ref[e + 1]
    nused = first_ref[N_EXPERTS]

    def rows(b):
        return pl.ds(pl.multiple_of(b * bm, bm), bm)

    def load(b, s):
        return pltpu.make_async_copy(xs_ref.at[rows(b), :], xbuf.at[s], sem_in.at[s])

    def store(b, s):
        return pltpu.make_async_copy(ybuf.at[s], ys_ref.at[rows(b), :], sem_out.at[s])

    @pl.when(e == 0)
    def _():
        for j in range(EXPERT_IN_SLOTS - 1):
            @pl.when(j < nused)
            def _():
                load(j, j).start()

    @pl.when(hi > lo)
    def _():
        wg_bf[...] = wg_ref[...].astype(BF16)
        wu_bf[...] = wu_ref[...].astype(BF16)
        wd_bf[...] = wd_ref[...].astype(BF16)

    def block(b, c):
        load(b, b % EXPERT_IN_SLOTS).wait()
        ahead = b + EXPERT_IN_SLOTS - 1

        @pl.when(ahead < nused)
        def _():
            load(ahead, ahead % EXPERT_IN_SLOTS).start()

        s = b % 2

        @pl.when(b >= 2)
        def _():
            store(b - 2, s).wait()

        xa, xb = _load_rows(xbuf.at[b % EXPERT_IN_SLOTS], EXPERT_BLOCK)
        xa = xa.astype(BF16)
        xb = xb.astype(BF16)
        g = (jnp.dot(xa, wg_bf[:PACKED, :], preferred_element_type=F32)
             + jnp.dot(xb, wg_bf[PACKED:, :], preferred_element_type=F32))
        u = (jnp.dot(xa, wu_bf[:PACKED, :], preferred_element_type=F32)
             + jnp.dot(xb, wu_bf[PACKED:, :], preferred_element_type=F32))
        h = (_silu(g) * u).astype(BF16)
        _store_rows(ybuf.at[s], _pack_bf16_pair(jnp.dot(h, wd_bf[...], preferred_element_type=F32)))
        store(b, s).start()
        return c

    lax.fori_loop(lo, hi, block, 0)

    @pl.when(e == pl.num_programs(0) - 1)
    def _():
        @pl.when(nused >= 2)
        def _():
            store(nused - 2, nused % 2).wait()

        @pl.when(nused >= 1)
        def _():
            store(nused - 1, (nused - 1) % 2).wait()

        ybuf[0] = jnp.zeros((bm, LANES), U32)

        def fill(b, c):
            store(b, 0).start()
            return c

        def drain(b, c):
            store(b, 0).wait()
            return c

        lax.fori_loop(nused, n_blocks, fill, 0)
        lax.fori_loop(nused, n_blocks, drain, 0)


def _expert_call(first_blk, xs, w_e_gate, w_e_up, w_e_down, l):
    n_slots = xs.shape[0]
    bm = EXPERT_BLOCK

    def w_idx(e, first_ref):
        return (l, e, 0, 0)

    grid_spec = pltpu.PrefetchScalarGridSpec(
        num_scalar_prefetch=1,
        grid=(N_EXPERTS,),
        in_specs=[
            pl.BlockSpec(memory_space=pl.ANY),
            pl.BlockSpec((None, None, D_MODEL, EXPERT_FF), w_idx),
            pl.BlockSpec((None, None, D_MODEL, EXPERT_FF), w_idx),
            pl.BlockSpec((None, None, EXPERT_FF, D_MODEL), w_idx),
        ],
        out_specs=pl.BlockSpec(memory_space=pl.ANY),
        scratch_shapes=[pltpu.VMEM((EXPERT_IN_SLOTS, ROW_SUB * bm, LANES), U32),
                        pltpu.VMEM((2, ROW_SUB * bm, LANES), U32),
                        pltpu.VMEM((D_MODEL, EXPERT_FF), BF16),
                        pltpu.VMEM((D_MODEL, EXPERT_FF), BF16),
                        pltpu.VMEM((EXPERT_FF, D_MODEL), BF16),
                        pltpu.SemaphoreType.DMA((EXPERT_IN_SLOTS,)),
                        pltpu.SemaphoreType.DMA((2,))],
    )
    return pl.pallas_call(
        _expert_kernel,
        out_shape=jax.ShapeDtypeStruct(xs.shape, U32),
        grid_spec=grid_spec,
        compiler_params=_params(("arbitrary",)),
        name="experts",
    )(first_blk, xs, w_e_gate, w_e_up, w_e_down)


def _combine_kernel(slots_ref, slots_next_ref, gw_ref, x1_ref, h2_ref, mod_ref, wsg_ref, wsu_ref, wsd_ref, ys_ref,
                    o_ref, buf, sems):
    i = pl.program_id(0)
    n_tok = x1_ref.shape[0]
    half = i % 2

    def issue(idx_ref, dst_half):
        def body(t, c):
            dst_rows = pl.ds(pl.multiple_of(t * ROW_SUB, ROW_SUB), ROW_SUB)
            for k in range(TOP_K):
                row0 = pl.multiple_of(idx_ref[t * TOP_K + k], ROW_SUB)
                pltpu.make_async_copy(ys_ref.at[pl.ds(row0, ROW_SUB), :], buf.at[dst_half, k, dst_rows, :],
                                      sems.at[dst_half]).start(priority=k % 2)
            return c
        lax.fori_loop(0, n_tok, body, 0)

    @pl.when(i == 0)
    def _():
        issue(slots_ref, 0)

    has_next = i + 1 < pl.num_programs(0)

    @pl.when(jnp.logical_and(has_next, half == 0))
    def _():
        issue(slots_next_ref, 1)

    @pl.when(jnp.logical_and(has_next, half == 1))
    def _():
        issue(slots_next_ref, 0)

    ha, hb = _load_rows(h2_ref, n_tok)
    ha = ha.astype(BF16)
    hb = hb.astype(BF16)
    gate = (jnp.dot(ha, wsg_ref[:PACKED, :], preferred_element_type=F32)
            + jnp.dot(hb, wsg_ref[PACKED:, :], preferred_element_type=F32))
    up = (jnp.dot(ha, wsu_ref[:PACKED, :], preferred_element_type=F32)
          + jnp.dot(hb, wsu_ref[PACKED:, :], preferred_element_type=F32))
    acc = jnp.dot((_silu(gate) * up).astype(BF16), wsd_ref[...], preferred_element_type=F32)
    acc_a = acc[:, :PACKED]
    acc_b = acc[:, PACKED:]

    for k in range(TOP_K):
        pltpu.make_async_copy(ys_ref.at[pl.ds(0, n_tok * ROW_SUB), :], buf.at[half, k], sems.at[half]).wait()
    gw = gw_ref[...]
    for k in range(TOP_K):
        ya, yb = _load_rows(buf.at[half, k], n_tok)
        acc_a = acc_a + gw[:, k:k + 1] * ya
        acc_b = acc_b + gw[:, k:k + 1] * yb
    g2 = mod_ref[:, 5 * D_MODEL:6 * D_MODEL]
    o_ref[...] = x1_ref[...] + g2 * jnp.concatenate([acc_a, acc_b], axis=1)


def _combine_call(slots_flat, gw, x1, h2, mod3, ys, l, P, n_ctx, lat_t):
    n = x1.shape[0]
    tm = COMBINE_TILE
    n_ctx_tiles = n_ctx // tm
    tiles_per_seq = lat_t // tm

    def mod_idx(i):
        return (jnp.where(i < n_ctx_tiles, 0, 1 + (i - n_ctx_tiles) // tiles_per_seq), 0, 0)

    const = lambda i: (0, 0)
    row = lambda i: (i, 0)
    n_tiles = n // tm
    return pl.pallas_call(
        _combine_kernel,
        out_shape=jax.ShapeDtypeStruct((n, D_MODEL), F32),
        grid=(n_tiles,),
        in_specs=[
            pl.BlockSpec((tm * TOP_K,), lambda i: (i,), memory_space=pltpu.SMEM),
            pl.BlockSpec((tm * TOP_K,), lambda i: (jnp.minimum(i + 1, n_tiles - 1),), memory_space=pltpu.SMEM),
            pl.BlockSpec((tm, TOP_K), row),
            pl.BlockSpec((tm, D_MODEL), row),
            pl.BlockSpec((ROW_SUB * tm, LANES), row),
            pl.BlockSpec((None, 1, 6 * D_MODEL), mod_idx),
            pl.BlockSpec((D_MODEL, EXPERT_FF), const),
            pl.BlockSpec((D_MODEL, EXPERT_FF), const),
            pl.BlockSpec((EXPERT_FF, D_MODEL), const),
            pl.BlockSpec(memory_space=pl.ANY),
        ],
        out_specs=pl.BlockSpec((tm, D_MODEL), row),
        scratch_shapes=[pltpu.VMEM((2, TOP_K, ROW_SUB * tm, LANES), U32),
                        pltpu.SemaphoreType.DMA((2,))],
        compiler_params=_params(("arbitrary",)),
        name="combine",
    )(slots_flat, slots_flat, gw, x1, h2, mod3, P["w_s_gate_bf16"][l], P["w_s_up_bf16"][l],
      P["w_s_down_bf16"][l], ys)


def _rope_tables(lat_t):
    rows = lat_t // GRID_W
    row = jnp.repeat(jnp.arange(rows, dtype=F32), GRID_W)
    col = jnp.tile(jnp.arange(GRID_W, dtype=F32), rows)
    n_freq = HEAD_DIM // 4
    inv = jnp.power(ROPE_BASE, -jnp.arange(n_freq, dtype=F32) / n_freq)
    ang = jnp.concatenate([row[:, None] * inv, col[:, None] * inv], axis=-1)
    cos = jnp.repeat(jnp.cos(ang), 2, axis=-1)
    sign = jnp.tile(jnp.array([-1.0, 1.0], F32), HEAD_DIM // 2)
    sin = jnp.repeat(jnp.sin(ang), 2, axis=-1) * sign
    cos = jnp.concatenate([cos, jnp.ones((TOKEN_TILE, HEAD_DIM), F32)], axis=0)
    sin = jnp.concatenate([sin, jnp.zeros((TOKEN_TILE, HEAD_DIM), F32)], axis=0)
    reps = LANES // HEAD_DIM
    return jnp.tile(cos, (1, reps)), jnp.tile(sin, (1, reps))


def _block_diag_gates(w_a, w_x):
    def dense(w):
        eye = jnp.eye(LRU_HEADS, dtype=w.dtype)
        return jnp.einsum("ncd,nm->ncmd", w, eye).reshape(LRU_WIDTH, LRU_WIDTH)
    return jnp.concatenate([dense(w_a[0]), dense(w_x[0]), dense(w_a[1]), dense(w_x[1])], axis=1)


def _prepare(P):
    Q = dict(P)
    Q["w_in_bf16"] = P["w_in"].astype(BF16)
    Q["w_out_bf16"] = P["w_out"].astype(BF16)
    Q["q_norm_g_t"] = jnp.tile(P["q_norm_g"], (1, N_HEADS))
    Q["k_norm_g_t"] = jnp.tile(P["k_norm_g"], (1, N_KV_HEADS))
    head = jnp.arange(ATTN_WIDTH) // HEAD_DIM
    Q["head_blockdiag"] = (head[:, None] == head[None, :]).astype(BF16)
    Q["lru_gate_w"] = jnp.stack([_block_diag_gates(P["lru_w_a"][l], P["lru_w_x"][l])
                                 for l in range(DEPTH)]).astype(BF16)
    Q["lru_gate_b"] = jnp.concatenate([P["lru_b_a"][:, 0], P["lru_b_x"][:, 0],
                                       P["lru_b_a"][:, 1], P["lru_b_x"][:, 1]], axis=-1)
    w_hi = P["w_router"].astype(BF16)
    Q["w_router_hi"] = w_hi
    Q["w_router_lo"] = (P["w_router"] - w_hi.astype(F32)).astype(BF16)
    Q["w_s_gate_bf16"] = P["w_s_gate"].astype(BF16)
    Q["w_s_up_bf16"] = P["w_s_up"].astype(BF16)
    Q["w_s_down_bf16"] = P["w_s_down"].astype(BF16)
    return Q


def _group_tables(counts):
    bm = EXPERT_BLOCK
    blocks = (counts.astype(I32) + bm - 1) // bm
    first_blk = jnp.concatenate([jnp.zeros((1,), I32), jnp.cumsum(blocks).astype(I32)])
    group_start = (first_blk[:N_EXPERTS] * bm).astype(F32)[None, :]
    return group_start, first_blk


def _trunk(x_prompt, x_sample, c, cache_k, cache_v, state_lru, c_ctx, P):
    n_seq_c, ctx_t, _ = x_prompt.shape
    n_seq_l, lat_t, _ = x_sample.shape
    past = cache_k.shape[2]
    n_ctx = n_seq_c * ctx_t
    n_lat = n_seq_l * lat_t
    n = n_ctx + n_lat
    assert n_ctx % lat_t == 0, "latent sequences must start on a whole-sequence boundary of the merged token axis"
    lat_off = n_ctx // lat_t
    P = _prepare(P)

    n_cond = -(-(1 + n_seq_l) // SUBLANES) * SUBLANES
    cond = jnp.concatenate([c_ctx[None, :], c, jnp.zeros((n_cond - 1 - n_seq_l, D_MODEL), F32)], axis=0)
    mods = _mod_call(cond, P["w_mod"], P["b_mod"])
    rope_cos, rope_sin = _rope_tables(lat_t)
    tri = (jnp.arange(TOKEN_TILE)[:, None] > jnp.arange(TOKEN_TILE)[None, :]).astype(BF16)
    n_blocks = n * TOP_K // EXPERT_BLOCK + N_EXPERTS
    n_slots = n_blocks * EXPERT_BLOCK

    x = jnp.concatenate([x_prompt.reshape(n_ctx, D_MODEL), x_sample.reshape(n_lat, D_MODEL)], axis=0)
    ks, vs, ss = [], [], []
    for l in range(DEPTH):
        mod3 = mods[l].reshape(n_cond, 1, 6 * D_MODEL)
        q, kr, vb, kc, vc, u, gate, cv = _premix_call(x, mod3, l, P, rope_cos, rope_sin, n_ctx, lat_t)
        ks.append(kc[:n_ctx].reshape(n_seq_c, ctx_t, N_KV_HEADS, HEAD_DIM))
        vs.append(vc[:n_ctx].reshape(n_seq_c, ctx_t, N_KV_HEADS, HEAD_DIM))

        as_ctx = lambda a: a.reshape(n // ctx_t, ctx_t, a.shape[-1])
        as_lat = lambda a: a.reshape(n // lat_t, lat_t, a.shape[-1])

        attn_c = _attn_call(as_ctx(q), as_ctx(kr), as_ctx(vb), min(ctx_t, 256), n_seq_c, 0, 0)
        k_all = jnp.concatenate([kr[n_ctx:].reshape(n_seq_l, lat_t, KV_WIDTH),
                                 cache_k[:, l].reshape(n_seq_l, past, KV_WIDTH).astype(BF16)], axis=1)
        cv_ones = jnp.concatenate([cache_v[:, l], jnp.ones_like(cache_v[:, l])], axis=-1)
        v_all = jnp.concatenate([vb[n_ctx:].reshape(n_seq_l, lat_t, 2 * KV_WIDTH),
                                 cv_ones.reshape(n_seq_l, past, 2 * KV_WIDTH).astype(BF16)], axis=1)
        attn_l = _attn_call(as_lat(q), k_all, v_all, 256, n_seq_l, lat_off, 0)

        lru_c, fin_c = _lru_call(as_ctx(u), as_ctx(gate), jnp.zeros((n_seq_c, 2, LRU_WIDTH), F32), l, P,
                                 n_seq_c, 0)
        lru_l, _ = _lru_call(as_lat(u), as_lat(gate), state_lru[:, l], l, P, n_seq_l, lat_off)
        ss.append(fin_c)
        conv_c = _convmod_call(as_ctx(cv), l, P, n_seq_c, 0)
        conv_l = _convmod_call(as_lat(cv), l, P, n_seq_l, lat_off)

        x1, h2, logits = _postmix_call(
            attn_c.reshape(n_ctx, ATTN_WIDTH), attn_l.reshape(n_lat, ATTN_WIDTH),
            lru_c.reshape(n_ctx, LRU_WIDTH), lru_l.reshape(n_lat, LRU_WIDTH),
            conv_c.reshape(n_ctx, CONV_WIDTH), conv_l.reshape(n_lat, CONV_WIDTH),
            x, mod3, l, P, n_ctx, lat_t)

        ids, gw, rank, counts = _route_call(logits, P["b_router"][l], tri)
        group_start, first_blk = _group_tables(counts[0])
        slots = _slots_call(ids, rank, group_start).reshape(-1)
        xs = _dispatch_call(first_blk, slots, h2, n_slots)
        ys = _expert_call(first_blk, xs, P["w_e_gate"], P["w_e_up"], P["w_e_down"], l)
        x = _combine_call(slots, gw, x1, h2, mod3, ys, l, P, n_ctx, lat_t)

    y_prompt = x[:n_ctx].reshape(n_seq_c, ctx_t, D_MODEL)
    y_sample = x[n_ctx:].reshape(n_seq_l, lat_t, D_MODEL)
    return (y_prompt, y_sample, jnp.stack(ks, axis=1), jnp.stack(vs, axis=1), jnp.stack(ss, axis=1))


def kernel(x_prompt, x_sample, c, cache_k, cache_v, state_lru, c_ctx, w_mod, b_mod, norm1_g, w_in, q_norm_g, k_norm_g, lru_conv_w, lru_conv_b, lru_w_a, lru_b_a, lru_w_x, lru_b_x, lru_lambda, cm_dw_w, cm_dw_b, cm_ln_g, cm_ln_b, out_norm_g, w_out, norm2_g, w_router, b_router, w_e_gate, w_e_up, w_e_down, w_s_gate, w_s_up, w_s_down):
    P = {"w_mod": w_mod, "b_mod": b_mod, "norm1_g": norm1_g, "w_in": w_in, "q_norm_g": q_norm_g,
         "k_norm_g": k_norm_g, "lru_conv_w": lru_conv_w, "lru_conv_b": lru_conv_b, "lru_w_a": lru_w_a,
         "lru_b_a": lru_b_a, "lru_w_x": lru_w_x, "lru_b_x": lru_b_x, "lru_lambda": lru_lambda,
         "cm_dw_w": cm_dw_w, "cm_dw_b": cm_dw_b, "cm_ln_g": cm_ln_g, "cm_ln_b": cm_ln_b,
         "out_norm_g": out_norm_g, "w_out": w_out, "norm2_g": norm2_g, "w_router": w_router,
         "b_router": b_router, "w_e_gate": w_e_gate, "w_e_up": w_e_up, "w_e_down": w_e_down,
         "w_s_gate": w_s_gate, "w_s_up": w_s_up, "w_s_down": w_s_down}
    return _trunk(x_prompt, x_sample, c, cache_k, cache_v, state_lru, c_ctx, P)
```

```python
import functools
import math

import jax
import jax.numpy as jnp
from jax import lax
from jax.experimental import pallas as pl
from jax.experimental.pallas import tpu as pltpu

F32 = jnp.float32
BF16 = jnp.bfloat16
I32 = jnp.int32

D_MODEL = 1024
DEPTH = 2
GRID_W = 64
ATTN_WIDTH = 512
LRU_WIDTH = 256
CONV_WIDTH = 256
HEAD_DIM = 64
N_HEADS = 8
N_KV_HEADS = 2
KV_GROUP = N_HEADS // N_KV_HEADS
KV_WIDTH = N_KV_HEADS * HEAD_DIM
ROPE_BASE = 10000.0
LRU_HEADS = 4
LRU_HEAD_DIM = LRU_WIDTH // LRU_HEADS
LRU_CONV = 4
RG_C = 8.0
CM_KERNEL = 31
N_EXPERTS = 256
TOP_K = 8
EXPERT_FF = 256
ROUTED_SCALE = 2.5
EPS = 1e-6
IN_WIDTH = ATTN_WIDTH + 2 * KV_WIDTH + 2 * LRU_WIDTH + 2 * CONV_WIDTH

SUBLANES = 8
LANES = 128
VMEM_LIMIT = 56 * 1024 * 1024

TOKEN_TILE = 256
ROUTE_TILE = 512
SCAN_CHUNK = 256
CONV_CHUNK = 128
CONV_HALO = 16
LRU_HALO = 8
EXPERT_BLOCK = 256
EXPERT_IN_SLOTS = 4
COMBINE_TILE = 256
DISPATCH_TILE = 512


def _params(sem):
    return pltpu.CompilerParams(dimension_semantics=sem, vmem_limit_bytes=VMEM_LIMIT)


def _sigmoid(x):
    return 1.0 / (1.0 + jnp.exp(-x))


def _silu(x):
    return x * _sigmoid(x)


def _bdot(a, b):
    return jnp.dot(a.astype(BF16), b.astype(BF16), preferred_element_type=F32)


def _split_dot(a, b_bf16):
    hi = a.astype(BF16)
    lo = (a - hi.astype(F32)).astype(BF16)
    return (jnp.dot(hi, b_bf16, preferred_element_type=F32)
            + jnp.dot(lo, b_bf16, preferred_element_type=F32))


U32 = jnp.uint32
PACKED = D_MODEL // 2


def _pack_bf16_pair(x):
    hi = lax.bitcast_convert_type(x[:, :PACKED].astype(BF16).astype(F32), U32)
    lo = lax.bitcast_convert_type(x[:, PACKED:].astype(BF16).astype(F32), U32)
    return (hi & jnp.uint32(0xFFFF0000)) | (lo >> 16)


def _unpack_bf16_pair(w):
    hi = lax.bitcast_convert_type(w & jnp.uint32(0xFFFF0000), F32)
    lo = lax.bitcast_convert_type(w << 16, F32)
    return hi, lo


ROW_SUB = PACKED // LANES


def _store_rows(ref, w):
    r = w.shape[0]
    for j in range(ROW_SUB):
        ref[pl.ds(j, r, stride=ROW_SUB), :] = w[:, j * LANES:(j + 1) * LANES]


def _load_rows(ref, r):
    his, los = [], []
    for j in range(ROW_SUB):
        hi, lo = _unpack_bf16_pair(ref[pl.ds(j, r, stride=ROW_SUB), :])
        his.append(hi)
        los.append(lo)
    return jnp.concatenate(his, axis=1), jnp.concatenate(los, axis=1)


def _rms(x, g):
    return x * lax.rsqrt(jnp.mean(x * x, axis=-1, keepdims=True) + EPS) * g


MOD_COLS = 1536


def _mod_kernel(c_ref, w_ref, b_ref, o_ref):
    a = _silu(c_ref[...])
    o_ref[...] = jnp.dot(a, w_ref[...], preferred_element_type=F32,
                         precision=lax.Precision.HIGHEST) + b_ref[...]


def _mod_call(cond, w_mod, b_mod):
    n_cond = cond.shape[0]
    width = 6 * D_MODEL
    return pl.pallas_call(
        _mod_kernel,
        out_shape=jax.ShapeDtypeStruct((DEPTH, n_cond, width), F32),
        grid=(DEPTH, width // MOD_COLS),
        in_specs=[
            pl.BlockSpec((n_cond, D_MODEL), lambda l, j: (0, 0)),
            pl.BlockSpec((None, D_MODEL, MOD_COLS), lambda l, j: (l, 0, j)),
            pl.BlockSpec((None, 1, MOD_COLS), lambda l, j: (l, 0, j)),
        ],
        out_specs=pl.BlockSpec((None, n_cond, MOD_COLS), lambda l, j: (l, 0, j)),
        compiler_params=_params(("arbitrary", "arbitrary")),
        name="mod",
    )(cond, w_mod, b_mod.reshape(DEPTH, 1, width))


def _swap_pairs(x):
    n = x.shape[-1]
    lane = lax.broadcasted_iota(I32, x.shape, 1)
    nxt = pltpu.roll(x, n - 1, 1)
    prv = pltpu.roll(x, 1, 1)
    return jnp.where(lane % 2 == 0, nxt, prv)


def _premix_kernel(n_ctx_tiles, xc_ref, xl_ref, mod_ref, n1g_ref, w_ref, qg_ref, kg_ref, bd_ref, cos_ref, sin_ref,
                   q_ref, kr_ref, vb_ref, kc_ref, vc_ref, u_ref, gate_ref, cv_ref):
    x = jnp.where(pl.program_id(0) < n_ctx_tiles, xc_ref[...], xl_ref[...])
    sh1 = mod_ref[:, 0:D_MODEL]
    sc1 = mod_ref[:, D_MODEL:2 * D_MODEL]
    h = _rms(x, n1g_ref[...]) * (1.0 + sc1) + sh1
    z = jnp.dot(h.astype(BF16), w_ref[...], preferred_element_type=F32)
    o = 0
    q = z[:, o:o + ATTN_WIDTH]; o += ATTN_WIDTH
    k = z[:, o:o + KV_WIDTH]; o += KV_WIDTH
    v = z[:, o:o + KV_WIDTH]; o += KV_WIDTH
    u_ref[...] = z[:, o:o + LRU_WIDTH]; o += LRU_WIDTH
    gate_ref[...] = z[:, o:o + LRU_WIDTH]; o += LRU_WIDTH
    cv_ref[...] = z[:, o:o + 2 * CONV_WIDTH]

    bd = bd_ref[...]
    inv_hd = 1.0 / HEAD_DIM
    q_ms = _split_dot(q * q, bd) * inv_hd
    k_ms = _split_dot(k * k, bd[:KV_WIDTH, :KV_WIDTH]) * inv_hd
    qn = q * lax.rsqrt(q_ms + EPS) * qg_ref[...]
    kn = k * lax.rsqrt(k_ms + EPS) * kg_ref[...]
    kc_ref[...] = kn
    vc_ref[...] = v
    ones = jnp.ones((v.shape[0], HEAD_DIM), F32)
    vb_ref[...] = jnp.concatenate([v[:, :HEAD_DIM], ones, v[:, HEAD_DIM:], ones], axis=1).astype(BF16)

    cos = cos_ref[...]
    sin = sin_ref[...]
    reps = ATTN_WIDTH // cos.shape[-1]
    cos_q = jnp.concatenate([cos] * reps, axis=1)
    sin_q = jnp.concatenate([sin] * reps, axis=1)
    qr = qn * cos_q + _swap_pairs(qn) * sin_q
    kr = kn * cos + _swap_pairs(kn) * sin
    q_ref[...] = (qr * (HEAD_DIM ** -0.5 * math.log2(math.e))).astype(BF16)
    kr_ref[...] = kr.astype(BF16)


def _premix_call(x_c, x_l, mod3, l, P, rope_cos, rope_sin, n_ctx, lat_t):
    n = x_c.shape[0] + x_l.shape[0]
    tm = TOKEN_TILE
    n_ctx_tiles = n_ctx // tm
    tiles_per_seq = lat_t // tm

    def mod_idx(i):
        return (jnp.where(i < n_ctx_tiles, 0, 1 + (i - n_ctx_tiles) // tiles_per_seq), 0, 0)

    def rope_idx(i):
        return (jnp.where(i < n_ctx_tiles, tiles_per_seq, (i - n_ctx_tiles) % tiles_per_seq), 0)

    const = lambda i: (0, 0)
    row = lambda i: (i, 0)
    row_c = lambda i: (jnp.minimum(i, n_ctx_tiles - 1), 0)
    row_l = lambda i: (jnp.maximum(i - n_ctx_tiles, 0), 0)
    outs = pl.pallas_call(
        functools.partial(_premix_kernel, n_ctx_tiles),
        out_shape=(
            jax.ShapeDtypeStruct((n, ATTN_WIDTH), BF16),
            jax.ShapeDtypeStruct((n, KV_WIDTH), BF16),
            jax.ShapeDtypeStruct((n, 2 * KV_WIDTH), BF16),
            jax.ShapeDtypeStruct((n, KV_WIDTH), F32),
            jax.ShapeDtypeStruct((n, KV_WIDTH), F32),
            jax.ShapeDtypeStruct((n, LRU_WIDTH), F32),
            jax.ShapeDtypeStruct((n, LRU_WIDTH), F32),
            jax.ShapeDtypeStruct((n, 2 * CONV_WIDTH), F32),
        ),
        grid=(n // tm,),
        in_specs=[
            pl.BlockSpec((tm, D_MODEL), row_c),
            pl.BlockSpec((tm, D_MODEL), row_l),
            pl.BlockSpec((None, 1, 6 * D_MODEL), mod_idx),
            pl.BlockSpec((1, D_MODEL), const),
            pl.BlockSpec((D_MODEL, IN_WIDTH), const),
            pl.BlockSpec((1, ATTN_WIDTH), const),
            pl.BlockSpec((1, KV_WIDTH), const),
            pl.BlockSpec((ATTN_WIDTH, ATTN_WIDTH), const),
            pl.BlockSpec((tm, LANES), rope_idx),
            pl.BlockSpec((tm, LANES), rope_idx),
        ],
        out_specs=(
            pl.BlockSpec((tm, ATTN_WIDTH), row),
            pl.BlockSpec((tm, KV_WIDTH), row),
            pl.BlockSpec((tm, 2 * KV_WIDTH), row),
            pl.BlockSpec((tm, KV_WIDTH), row),
            pl.BlockSpec((tm, KV_WIDTH), row),
            pl.BlockSpec((tm, LRU_WIDTH), row),
            pl.BlockSpec((tm, LRU_WIDTH), row),
            pl.BlockSpec((tm, 2 * CONV_WIDTH), row),
        ),
        compiler_params=_params(("parallel",)),
        name="premix",
    )(x_c, x_l, mod3, P["norm1_g"][l][None, :], P["w_in_bf16"][l], P["q_norm_g_t"][l][None, :],
      P["k_norm_g_t"][l][None, :], P["head_blockdiag"], rope_cos, rope_sin)
    return outs


def _attn_kernel(q_ref, k_ref, v_ref, o_ref):
    outs = []
    for kh in range(N_KV_HEADS):
        k = k_ref[:, kh * HEAD_DIM:(kh + 1) * HEAD_DIM]
        v = v_ref[:, kh * 2 * HEAD_DIM:(kh + 1) * 2 * HEAD_DIM]
        for g in range(KV_GROUP):
            hd = kh * KV_GROUP + g
            q = q_ref[:, hd * HEAD_DIM:(hd + 1) * HEAD_DIM]
            s = lax.dot_general(q, k, (((1,), (1,)), ((), ())), preferred_element_type=F32)
            m = jnp.max(s, axis=-1, keepdims=True)
            p = jnp.exp2(s - m).astype(BF16)
            o = jnp.dot(p, v, preferred_element_type=F32)
            outs.append(o[:, :HEAD_DIM] / o[:, HEAD_DIM:HEAD_DIM + 1])
    o_ref[...] = jnp.concatenate(outs, axis=1)


def _attn_call(q3, k3, v3, tq, b, q_off, kv_off):
    s = q3.shape[1]
    t = k3.shape[1]
    return pl.pallas_call(
        _attn_kernel,
        out_shape=jax.ShapeDtypeStruct((b, s, ATTN_WIDTH), F32),
        grid=(b, s // tq),
        in_specs=[
            pl.BlockSpec((None, tq, ATTN_WIDTH), lambda i, j: (i + q_off, j, 0)),
            pl.BlockSpec((None, t, KV_WIDTH), lambda i, j: (i + kv_off, 0, 0)),
            pl.BlockSpec((None, t, 2 * KV_WIDTH), lambda i, j: (i + kv_off, 0, 0)),
        ],
        out_specs=pl.BlockSpec((None, tq, ATTN_WIDTH), lambda i, j: (i, j, 0)),
        compiler_params=_params(("parallel", "parallel")),
        name="attention",
    )(q3, k3, v3)


def _tap_sum(win, taps, bias, first, n_taps, rows):
    n = win.shape[0]
    acc = bias
    for phase in range(SUBLANES):
        starts = [first + j for j in range(n_taps) if (first + j) % SUBLANES == phase]
        if not starts:
            continue
        rolled = win if phase == 0 else pltpu.roll(win, n - phase, 0)
        for s0 in starts:
            base = s0 - phase
            acc = acc + taps[s0 - first:s0 - first + 1, :] * rolled[base:base + rows, :]
    return acc


def _chunk_scan(a, b, reverse):
    n = a.shape[0]
    row = lax.broadcasted_iota(I32, a.shape, 0)
    d = 1
    while d < n:
        if reverse:
            a_s = pltpu.roll(a, n - d, 0)
            b_s = pltpu.roll(b, n - d, 0)
            ok = row < n - d
        else:
            a_s = pltpu.roll(a, d, 0)
            b_s = pltpu.roll(b, d, 0)
            ok = row >= d
        b = jnp.where(ok, a * b_s + b, b)
        a = jnp.where(ok, a * a_s, a)
        d *= 2
    return a, b


def _gelu_tanh(x):
    return 0.5 * x * (1.0 + jnp.tanh(math.sqrt(2.0 / math.pi) * (x + 0.044715 * (x * x * x))))


def _lru_kernel(u_ref, gate_ref, cw_ref, cb_ref, wg_ref, bg_ref, lam_ref, h0_ref,
                y_ref, fin_ref, upad, xc_scr, fwd_scr):
    t_len = u_ref.shape[0]
    ch = SCAN_CHUNK
    n_chunks = t_len // ch
    w = LRU_WIDTH
    zeros_halo = jnp.zeros((LRU_HALO, w), F32)
    upad[0:LRU_HALO, :] = zeros_halo
    upad[t_len + LRU_HALO:t_len + 2 * LRU_HALO, :] = zeros_halo

    def fill(c, carry):
        r0 = pl.multiple_of(c * ch, ch)
        upad[pl.ds(r0 + LRU_HALO, ch), :] = u_ref[pl.ds(r0, ch), :]
        return carry

    lax.fori_loop(0, n_chunks, fill, 0)

    lam = lam_ref[...]
    nlam = -lam
    softplus = jnp.maximum(nlam, 0.0) + jnp.log(1.0 + jnp.exp(-jnp.abs(nlam)))
    decay = -RG_C * softplus
    cw = cw_ref[...]
    cb = cb_ref[...]

    def gates(xc, d):
        z = jnp.dot(xc.astype(BF16), wg_ref[:, 2 * d * w:2 * (d + 1) * w],
                    preferred_element_type=F32) + bg_ref[:, 2 * d * w:2 * (d + 1) * w]
        r = _sigmoid(z[:, :w])
        i = _sigmoid(z[:, w:])
        log_a = decay[d:d + 1, :] * r
        a = jnp.exp(log_a)
        b = jnp.sqrt(-jnp.tanh(log_a) * (a * a + 1.0)) * (i * xc)
        return a, b

    def fwd(c, h):
        r0 = pl.multiple_of(c * ch, ch)
        win = upad[pl.ds(r0, ch + 2 * LRU_HALO), :]
        xc = _tap_sum(win, cw, cb, LRU_HALO - (LRU_CONV - 1) // 2, LRU_CONV, ch)
        xc_scr[pl.ds(r0, ch), :] = xc
        a, b = gates(xc, 0)
        a_cum, b_cum = _chunk_scan(a, b, reverse=False)
        hs = a_cum * h + b_cum
        fwd_scr[pl.ds(r0, ch), :] = hs
        return hs[ch - 1:ch, :]

    h_f = lax.fori_loop(0, n_chunks, fwd, h0_ref[0:1, :])

    def bwd(ci, h):
        c = n_chunks - 1 - ci
        r0 = pl.multiple_of(c * ch, ch)
        xc = xc_scr[pl.ds(r0, ch), :]
        a, b = gates(xc, 1)
        a_cum, b_cum = _chunk_scan(a, b, reverse=True)
        hs = a_cum * h + b_cum
        y_ref[pl.ds(r0, ch), :] = (fwd_scr[pl.ds(r0, ch), :] + hs) * _gelu_tanh(gate_ref[pl.ds(r0, ch), :])
        return hs[0:1, :]

    h_b = lax.fori_loop(0, n_chunks, bwd, h0_ref[1:2, :])
    fin_ref[0:1, :] = h_f
    fin_ref[1:2, :] = h_b


def _lru_call(u3, gate3, h0, l, P, b, seq_off):
    _, t, w = u3.shape
    const = lambda i: (0, 0)
    seq = lambda i: (i, 0, 0)
    seq_in = lambda i: (i + seq_off, 0, 0)
    return pl.pallas_call(
        _lru_kernel,
        out_shape=(jax.ShapeDtypeStruct((b, t, w), F32),
                   jax.ShapeDtypeStruct((b, 2, w), F32)),
        grid=(b,),
        in_specs=[
            pl.BlockSpec((None, t, w), seq_in),
            pl.BlockSpec((None, t, w), seq_in),
            pl.BlockSpec((LRU_CONV, w), const),
            pl.BlockSpec((1, w), const),
            pl.BlockSpec((w, 4 * w), const),
            pl.BlockSpec((1, 4 * w), const),
            pl.BlockSpec((2, w), const),
            pl.BlockSpec((None, 2, w), seq),
        ],
        out_specs=(pl.BlockSpec((None, t, w), seq),
                   pl.BlockSpec((None, 2, w), seq)),
        scratch_shapes=[pltpu.VMEM((t + 2 * LRU_HALO, w), F32),
                        pltpu.VMEM((t, w), F32),
                        pltpu.VMEM((t, w), F32)],
        compiler_params=_params(("parallel",)),
        name="rglru",
    )(u3, gate3, P["lru_conv_w"][l], P["lru_conv_b"][l][None, :], P["lru_gate_w"][l],
      P["lru_gate_b"][l][None, :], P["lru_lambda"][l], h0)


def _convmod_kernel(cv_ref, w_ref, b_ref, g_ref, beta_ref, o_ref, hpad):
    t_len = cv_ref.shape[0]
    ch = CONV_CHUNK
    n_chunks = t_len // ch
    w = CONV_WIDTH
    zeros_halo = jnp.zeros((CONV_HALO, w), F32)
    hpad[0:CONV_HALO, :] = zeros_halo
    hpad[t_len + CONV_HALO:t_len + 2 * CONV_HALO, :] = zeros_halo

    def glu(c, carry):
        r0 = pl.multiple_of(c * ch, ch)
        blk = cv_ref[pl.ds(r0, ch), :]
        hpad[pl.ds(r0 + CONV_HALO, ch), :] = blk[:, :w] * _sigmoid(blk[:, w:])
        return carry

    lax.fori_loop(0, n_chunks, glu, 0)

    taps = w_ref[...]
    bias = b_ref[...]
    gamma = g_ref[...]
    beta = beta_ref[...]

    def conv(c, carry):
        r0 = pl.multiple_of(c * ch, ch)
        win = hpad[pl.ds(r0, ch + 2 * CONV_HALO), :]
        acc = _tap_sum(win, taps, bias, CONV_HALO - CM_KERNEL // 2, CM_KERNEL, ch)
        mu = jnp.mean(acc, axis=-1, keepdims=True)
        cen = acc - mu
        var = jnp.mean(cen * cen, axis=-1, keepdims=True)
        y = cen * lax.rsqrt(var + EPS) * gamma + beta
        o_ref[pl.ds(r0, ch), :] = _silu(y)
        return carry

    lax.fori_loop(0, n_chunks, conv, 0)


def _convmod_call(cv3, l, P, b, seq_off):
    t = cv3.shape[1]
    w = CONV_WIDTH
    const = lambda i: (0, 0)
    seq = lambda i: (i, 0, 0)
    return pl.pallas_call(
        _convmod_kernel,
        out_shape=jax.ShapeDtypeStruct((b, t, w), F32),
        grid=(b,),
        in_specs=[
            pl.BlockSpec((None, t, 2 * w), lambda i: (i + seq_off, 0, 0)),
            pl.BlockSpec((CM_KERNEL, w), const),
            pl.BlockSpec((1, w), const),
            pl.BlockSpec((1, w), const),
            pl.BlockSpec((1, w), const),
        ],
        out_specs=pl.BlockSpec((None, t, w), seq),
        scratch_shapes=[pltpu.VMEM((t + 2 * CONV_HALO, w), F32)],
        compiler_params=_params(("parallel",)),
        name="convmod",
    )(cv3, P["cm_dw_w"][l], P["cm_dw_b"][l][None, :], P["cm_ln_g"][l][None, :], P["cm_ln_b"][l][None, :])


def _postmix_kernel(n_ctx_tiles, attn_c_ref, attn_l_ref, lru_c_ref, lru_l_ref, conv_c_ref, conv_l_ref,
                    xc_ref, xl_ref, mod_ref, og_ref, wo_ref, n2g_ref, wr_hi_ref, wr_lo_ref,
                    x1_ref, h2_ref, lg_ref):
    is_ctx = pl.program_id(0) < n_ctx_tiles
    og = og_ref[...]
    a0, a1, a2 = ATTN_WIDTH, ATTN_WIDTH + LRU_WIDTH, D_MODEL
    y = jnp.concatenate([_rms(jnp.where(is_ctx, attn_c_ref[...], attn_l_ref[...]), og[:, :a0]),
                         _rms(jnp.where(is_ctx, lru_c_ref[...], lru_l_ref[...]), og[:, a0:a1]),
                         _rms(jnp.where(is_ctx, conv_c_ref[...], conv_l_ref[...]), og[:, a1:a2])], axis=1)
    y = jnp.dot(y.astype(BF16), wo_ref[...], preferred_element_type=F32)
    g1 = mod_ref[:, 2 * D_MODEL:3 * D_MODEL]
    sh2 = mod_ref[:, 3 * D_MODEL:4 * D_MODEL]
    sc2 = mod_ref[:, 4 * D_MODEL:5 * D_MODEL]
    x1 = jnp.where(is_ctx, xc_ref[...], xl_ref[...]) + g1 * y
    x1_ref[...] = x1
    h2 = _rms(x1, n2g_ref[...]) * (1.0 + sc2) + sh2
    _store_rows(h2_ref, _pack_bf16_pair(h2))
    hi = h2.astype(BF16)
    lo = (h2 - hi.astype(F32)).astype(BF16)
    w_hi = wr_hi_ref[...]
    lg_ref[...] = (jnp.dot(hi, w_hi, preferred_element_type=F32)
                   + jnp.dot(lo, w_hi, preferred_element_type=F32)
                   + jnp.dot(hi, wr_lo_ref[...], preferred_element_type=F32))


def _postmix_call(attn_c, attn_l, lru_c, lru_l, conv_c, conv_l, x_c, x_l, mod3, l, P, n_ctx, lat_t):
    n = x_c.shape[0] + x_l.shape[0]
    tm = TOKEN_TILE
    n_ctx_tiles = n_ctx // tm
    tiles_per_seq = lat_t // tm

    def mod_idx(i):
        return (jnp.where(i < n_ctx_tiles, 0, 1 + (i - n_ctx_tiles) // tiles_per_seq), 0, 0)

    const = lambda i: (0, 0)
    row = lambda i: (i, 0)
    row_c = lambda i: (jnp.minimum(i, n_ctx_tiles - 1), 0)
    row_l = lambda i: (jnp.maximum(i - n_ctx_tiles, 0), 0)
    return pl.pallas_call(
        functools.partial(_postmix_kernel, n_ctx_tiles),
        out_shape=(jax.ShapeDtypeStruct((n, D_MODEL), F32),
                   jax.ShapeDtypeStruct((ROW_SUB * n, LANES), U32),
                   jax.ShapeDtypeStruct((n, N_EXPERTS), F32)),
        grid=(n // tm,),
        in_specs=[
            pl.BlockSpec((tm, ATTN_WIDTH), row_c),
            pl.BlockSpec((tm, ATTN_WIDTH), row_l),
            pl.BlockSpec((tm, LRU_WIDTH), row_c),
            pl.BlockSpec((tm, LRU_WIDTH), row_l),
            pl.BlockSpec((tm, CONV_WIDTH), row_c),
            pl.BlockSpec((tm, CONV_WIDTH), row_l),
            pl.BlockSpec((tm, D_MODEL), row_c),
            pl.BlockSpec((tm, D_MODEL), row_l),
            pl.BlockSpec((None, 1, 6 * D_MODEL), mod_idx),
            pl.BlockSpec((1, D_MODEL), const),
            pl.BlockSpec((D_MODEL, D_MODEL), const),
            pl.BlockSpec((1, D_MODEL), const),
            pl.BlockSpec((D_MODEL, N_EXPERTS), const),
            pl.BlockSpec((D_MODEL, N_EXPERTS), const),
        ],
        out_specs=(pl.BlockSpec((tm, D_MODEL), row),
                   pl.BlockSpec((ROW_SUB * tm, LANES), row),
                   pl.BlockSpec((tm, N_EXPERTS), row)),
        compiler_params=_params(("parallel",)),
        name="postmix",
    )(attn_c, attn_l, lru_c, lru_l, conv_c, conv_l, x_c, x_l, mod3, P["out_norm_g"][l][None, :], P["w_out_bf16"][l],
      P["norm2_g"][l][None, :], P["w_router_hi"][l], P["w_router_lo"][l])


def _route_kernel(lg_ref, br_ref, tri_ref, ids_ref, gw_ref, rank_ref, cnt_ref, carry):
    i = pl.program_id(0)

    @pl.when(i == 0)
    def _():
        carry[...] = jnp.zeros_like(carry)

    scores = _sigmoid(lg_ref[...])
    sel = scores + br_ref[...]
    tm = scores.shape[0]
    lane = lax.broadcasted_iota(I32, (tm, N_EXPERTS), 1).astype(F32)
    slot_lane = lax.broadcasted_iota(I32, (tm, LANES), 1)
    ids_acc = jnp.zeros((tm, LANES), F32)
    gw_acc = jnp.zeros((tm, LANES), F32)
    hot = jnp.zeros((tm, N_EXPERTS), F32)
    picked = []
    for k in range(TOP_K):
        m = jnp.max(sel, axis=-1, keepdims=True)
        idx = jnp.min(jnp.where(sel == m, lane, float(N_EXPERTS)), axis=-1, keepdims=True)
        one = lane == idx
        g = jnp.sum(jnp.where(one, scores, 0.0), axis=-1, keepdims=True)
        sel = jnp.where(one, -jnp.inf, sel)
        hot = jnp.where(one, 1.0, hot)
        ids_acc = jnp.where(slot_lane == k, idx, ids_acc)
        gw_acc = jnp.where(slot_lane == k, g, gw_acc)
        picked.append(idx)
    denom = jnp.sum(gw_acc, axis=-1, keepdims=True)
    gw_acc = ROUTED_SCALE * gw_acc / denom

    before = jnp.dot(tri_ref[...], hot.astype(BF16), preferred_element_type=F32) + carry[0:1, :]
    rank_acc = jnp.zeros((tm, LANES), F32)
    for k in range(TOP_K):
        r = jnp.sum(jnp.where(lane == picked[k], before, 0.0), axis=-1, keepdims=True)
        rank_acc = jnp.where(slot_lane == k, r, rank_acc)
    carry[0:1, :] = carry[0:1, :] + jnp.sum(hot, axis=0, keepdims=True)
    cnt_ref[...] = carry[...]
    ids_ref[...] = ids_acc[:, :TOP_K].astype(I32)
    gw_ref[...] = gw_acc[:, :TOP_K]
    rank_ref[...] = rank_acc[:, :TOP_K].astype(I32)


def _route_call(logits, b_router_l, tri):
    n = logits.shape[0]
    tm = ROUTE_TILE
    const = lambda i: (0, 0)
    row = lambda i: (i, 0)
    return pl.pallas_call(
        _route_kernel,
        out_shape=(jax.ShapeDtypeStruct((n, TOP_K), I32),
                   jax.ShapeDtypeStruct((n, TOP_K), F32),
                   jax.ShapeDtypeStruct((n, TOP_K), I32),
                   jax.ShapeDtypeStruct((SUBLANES, N_EXPERTS), F32)),
        grid=(n // tm,),
        in_specs=[
            pl.BlockSpec((tm, N_EXPERTS), row),
            pl.BlockSpec((1, N_EXPERTS), const),
            pl.BlockSpec((tm, tm), const),
        ],
        out_specs=(pl.BlockSpec((tm, TOP_K), row),
                   pl.BlockSpec((tm, TOP_K), row),
                   pl.BlockSpec((tm, TOP_K), row),
                   pl.BlockSpec((SUBLANES, N_EXPERTS), const)),
        scratch_shapes=[pltpu.VMEM((SUBLANES, N_EXPERTS), F32)],
        compiler_params=_params(("arbitrary",)),
        name="route",
    )(logits, b_router_l[None, :], tri)


def _slots_kernel(ids_ref, rank_ref, start_ref, slots_ref):
    ids = ids_ref[...].astype(F32)
    tm = ids.shape[0]
    lane = lax.broadcasted_iota(I32, (tm, N_EXPERTS), 1).astype(F32)
    slot_lane = lax.broadcasted_iota(I32, (tm, TOP_K), 1)
    start = start_ref[...]
    acc = jnp.zeros((tm, TOP_K), F32)
    for k in range(TOP_K):
        s = jnp.sum(jnp.where(lane == ids[:, k:k + 1], start, 0.0), axis=-1, keepdims=True)
        acc = jnp.where(slot_lane == k, s, acc)
    slots_ref[...] = (acc.astype(I32) + rank_ref[...]) * ROW_SUB


def _slots_call(ids, rank, group_start):
    n = ids.shape[0]
    tm = ROUTE_TILE
    row = lambda i: (i, 0)
    return pl.pallas_call(
        _slots_kernel,
        out_shape=jax.ShapeDtypeStruct((n, TOP_K), I32),
        grid=(n // tm,),
        in_specs=[pl.BlockSpec((tm, TOP_K), row), pl.BlockSpec((tm, TOP_K), row),
                  pl.BlockSpec((1, N_EXPERTS), lambda i: (0, 0))],
        out_specs=pl.BlockSpec((tm, TOP_K), row),
        compiler_params=_params(("parallel",)),
        name="slots",
    )(ids, rank, group_start)


def _wait_rows(src_ref, dst_ref, sem, n_waits):
    for _ in range(n_waits):
        pltpu.make_async_copy(src_ref, dst_ref, sem).wait()


def _dispatch_kernel(first_ref, slots_ref, h_ref, xs_ref, zeros, sem_rows, sem_zero):
    n_tok = h_ref.shape[0] // ROW_SUB
    bm = EXPERT_BLOCK * ROW_SUB
    n_blocks = xs_ref.shape[0] // bm
    nused = first_ref[N_EXPERTS]

    def zero_copy(b):
        return pltpu.make_async_copy(zeros, xs_ref.at[pl.ds(pl.multiple_of(b * bm, bm), bm), :], sem_zero)

    @pl.when(pl.program_id(0) == 0)
    def _():
        zeros[...] = jnp.zeros_like(zeros)

        def group(start):
            def body(e, c):
                @pl.when(first_ref[e + 1] > first_ref[e])
                def _():
                    cp = zero_copy(first_ref[e + 1] - 1)
                    cp.start() if start else cp.wait()
                return c
            lax.fori_loop(0, N_EXPERTS, body, 0)

        def tail(start):
            def body(b, c):
                cp = zero_copy(b)
                cp.start() if start else cp.wait()
                return c
            lax.fori_loop(nused, n_blocks, body, 0)

        group(True)
        tail(True)
        group(False)
        tail(False)

    def issue(t, c):
        src = h_ref.at[pl.ds(pl.multiple_of(t * ROW_SUB, ROW_SUB), ROW_SUB), :]
        for k in range(TOP_K):
            row0 = pl.multiple_of(slots_ref[t * TOP_K + k], ROW_SUB)
            pltpu.make_async_copy(src, xs_ref.at[pl.ds(row0, ROW_SUB), :], sem_rows).start(priority=k % 2)
        return c

    lax.fori_loop(0, n_tok, issue, 0)
    _wait_rows(h_ref, xs_ref.at[pl.ds(0, n_tok * ROW_SUB), :], sem_rows, TOP_K)


def _dispatch_call(first_blk, slots_flat, h2, n_slots):
    n = h2.shape[0] // ROW_SUB
    tm = DISPATCH_TILE
    grid_spec = pltpu.PrefetchScalarGridSpec(
        num_scalar_prefetch=1,
        grid=(n // tm,),
        in_specs=[
            pl.BlockSpec((tm * TOP_K,), lambda i, first: (i,), memory_space=pltpu.SMEM),
            pl.BlockSpec((ROW_SUB * tm, LANES), lambda i, first: (i, 0)),
        ],
        out_specs=pl.BlockSpec(memory_space=pl.ANY),
        scratch_shapes=[pltpu.VMEM((ROW_SUB * EXPERT_BLOCK, LANES), U32),
                        pltpu.SemaphoreType.DMA,
                        pltpu.SemaphoreType.DMA],
    )
    return pl.pallas_call(
        _dispatch_kernel,
        out_shape=jax.ShapeDtypeStruct((ROW_SUB * n_slots, LANES), U32),
        grid_spec=grid_spec,
        compiler_params=_params(("arbitrary",)),
        name="dispatch",
    )(first_blk, slots_flat, h2)


def _expert_kernel(first_ref, xs_ref, wg_ref, wu_ref, wd_ref, ys_ref,
                   xbuf, ybuf, wg_bf, wu_bf, wd_bf, sem_in, sem_out):
    e = pl.program_id(0)
    bm = EXPERT_BLOCK * ROW_SUB
    n_blocks = ys_ref.shape[0] // bm
    lo = first_ref[e]
    hi = first_ref[e + 1]
    nused = first_ref[N_EXPERTS]

    def rows(b):
        return pl.ds(pl.multiple_of(b * bm, bm), bm)

    def load(b, s):
        return pltpu.make_async_copy(xs_ref.at[rows(b), :], xbuf.at[s], sem_in.at[s])

    def store(b, s):
        return pltpu.make_async_copy(ybuf.at[s], ys_ref.at[rows(b), :], sem_out.at[s])

    @pl.when(e == 0)
    def _():
        for j in range(EXPERT_IN_SLOTS - 1):
            @pl.when(j < nused)
            def _():
                load(j, j).start()

    @pl.when(hi > lo)
    def _():
        wg_bf[...] = wg_ref[...].astype(BF16)
        wu_bf[...] = wu_ref[...].astype(BF16)
        wd_bf[...] = wd_ref[...].astype(BF16)

    def block(b, c):
        load(b, b % EXPERT_IN_SLOTS).wait()
        ahead = b + EXPERT_IN_SLOTS - 1

        @pl.when(ahead < nused)
        def _():
            load(ahead, ahead % EXPERT_IN_SLOTS).start()

        s = b % 2

        @pl.when(b >= 2)
        def _():
            store(b - 2, s).wait()

        xa, xb = _load_rows(xbuf.at[b % EXPERT_IN_SLOTS], EXPERT_BLOCK)
        xa = xa.astype(BF16)
        xb = xb.astype(BF16)
        g = (jnp.dot(xa, wg_bf[:PACKED, :], preferred_element_type=F32)
             + jnp.dot(xb, wg_bf[PACKED:, :], preferred_element_type=F32))
        u = (jnp.dot(xa, wu_bf[:PACKED, :], preferred_element_type=F32)
             + jnp.dot(xb, wu_bf[PACKED:, :], preferred_element_type=F32))
        h = (_silu(g) * u).astype(BF16)
        _store_rows(ybuf.at[s], _pack_bf16_pair(jnp.dot(h, wd_bf[...], preferred_element_type=F32)))
        store(b, s).start()
        return c

    lax.fori_loop(lo, hi, block, 0)

    @pl.when(e == pl.num_programs(0) - 1)
    def _():
        @pl.when(nused >= 2)
        def _():
            store(nused - 2, nused % 2).wait()

        @pl.when(nused >= 1)
        def _():
            store(nused - 1, (nused - 1) % 2).wait()

        ybuf[0] = jnp.zeros((bm, LANES), U32)

        def fill(b, c):
            store(b, 0).start()
            return c

        def drain(b, c):
            store(b, 0).wait()
            return c

        lax.fori_loop(nused, n_blocks, fill, 0)
        lax.fori_loop(nused, n_blocks, drain, 0)


def _expert_call(first_blk, xs, w_e_gate, w_e_up, w_e_down, l):
    n_slots = xs.shape[0]
    bm = EXPERT_BLOCK

    def w_idx(e, first_ref):
        return (l, e, 0, 0)

    grid_spec = pltpu.PrefetchScalarGridSpec(
        num_scalar_prefetch=1,
        grid=(N_EXPERTS,),
        in_specs=[
            pl.BlockSpec(memory_space=pl.ANY),
            pl.BlockSpec((None, None, D_MODEL, EXPERT_FF), w_idx),
            pl.BlockSpec((None, None, D_MODEL, EXPERT_FF), w_idx),
            pl.BlockSpec((None, None, EXPERT_FF, D_MODEL), w_idx),
        ],
        out_specs=pl.BlockSpec(memory_space=pl.ANY),
        scratch_shapes=[pltpu.VMEM((EXPERT_IN_SLOTS, ROW_SUB * bm, LANES), U32),
                        pltpu.VMEM((2, ROW_SUB * bm, LANES), U32),
                        pltpu.VMEM((D_MODEL, EXPERT_FF), BF16),
                        pltpu.VMEM((D_MODEL, EXPERT_FF), BF16),
                        pltpu.VMEM((EXPERT_FF, D_MODEL), BF16),
                        pltpu.SemaphoreType.DMA((EXPERT_IN_SLOTS,)),
                        pltpu.SemaphoreType.DMA((2,))],
    )
    return pl.pallas_call(
        _expert_kernel,
        out_shape=jax.ShapeDtypeStruct(xs.shape, U32),
        grid_spec=grid_spec,
        compiler_params=_params(("arbitrary",)),
        name="experts",
    )(first_blk, xs, w_e_gate, w_e_up, w_e_down)


def _combine_kernel(n_ctx_tiles, slots_ref, slots_next_ref, gw_ref, x1_ref, h2_ref, mod_ref, wsg_ref, wsu_ref,
                    wsd_ref, ys_ref, oc_ref, ol_ref, buf, sems):
    i = pl.program_id(0)
    n_tok = x1_ref.shape[0]
    half = i % 2

    def issue(idx_ref, dst_half):
        def body(t, c):
            dst_rows = pl.ds(pl.multiple_of(t * ROW_SUB, ROW_SUB), ROW_SUB)
            for k in range(TOP_K):
                row0 = pl.multiple_of(idx_ref[t * TOP_K + k], ROW_SUB)
                pltpu.make_async_copy(ys_ref.at[pl.ds(row0, ROW_SUB), :], buf.at[dst_half, k, dst_rows, :],
                                      sems.at[dst_half]).start(priority=k % 2)
            return c
        lax.fori_loop(0, n_tok, body, 0)

    @pl.when(i == 0)
    def _():
        issue(slots_ref, 0)

    has_next = i + 1 < pl.num_programs(0)

    @pl.when(jnp.logical_and(has_next, half == 0))
    def _():
        issue(slots_next_ref, 1)

    @pl.when(jnp.logical_and(has_next, half == 1))
    def _():
        issue(slots_next_ref, 0)

    ha, hb = _load_rows(h2_ref, n_tok)
    ha = ha.astype(BF16)
    hb = hb.astype(BF16)
    gate = (jnp.dot(ha, wsg_ref[:PACKED, :], preferred_element_type=F32)
            + jnp.dot(hb, wsg_ref[PACKED:, :], preferred_element_type=F32))
    up = (jnp.dot(ha, wsu_ref[:PACKED, :], preferred_element_type=F32)
          + jnp.dot(hb, wsu_ref[PACKED:, :], preferred_element_type=F32))
    acc = jnp.dot((_silu(gate) * up).astype(BF16), wsd_ref[...], preferred_element_type=F32)
    acc_a = acc[:, :PACKED]
    acc_b = acc[:, PACKED:]

    for k in range(TOP_K):
        pltpu.make_async_copy(ys_ref.at[pl.ds(0, n_tok * ROW_SUB), :], buf.at[half, k], sems.at[half]).wait()
    gw = gw_ref[...]
    for k in range(TOP_K):
        ya, yb = _load_rows(buf.at[half, k], n_tok)
        acc_a = acc_a + gw[:, k:k + 1] * ya
        acc_b = acc_b + gw[:, k:k + 1] * yb
    g2 = mod_ref[:, 5 * D_MODEL:6 * D_MODEL]
    out = x1_ref[...] + g2 * jnp.concatenate([acc_a, acc_b], axis=1)

    @pl.when(i < n_ctx_tiles)
    def _():
        oc_ref[...] = out

    @pl.when(i >= n_ctx_tiles)
    def _():
        ol_ref[...] = out


def _combine_call(slots_flat, gw, x1, h2, mod3, ys, l, P, n_ctx, lat_t):
    n = x1.shape[0]
    tm = COMBINE_TILE
    n_ctx_tiles = n_ctx // tm
    tiles_per_seq = lat_t // tm

    def mod_idx(i):
        return (jnp.where(i < n_ctx_tiles, 0, 1 + (i - n_ctx_tiles) // tiles_per_seq), 0, 0)

    const = lambda i: (0, 0)
    row = lambda i: (i, 0)
    n_tiles = n // tm
    return pl.pallas_call(
        functools.partial(_combine_kernel, n_ctx_tiles),
        out_shape=(jax.ShapeDtypeStruct((n_ctx, D_MODEL), F32),
                   jax.ShapeDtypeStruct((n - n_ctx, D_MODEL), F32)),
        grid=(n_tiles,),
        in_specs=[
            pl.BlockSpec((tm * TOP_K,), lambda i: (i,), memory_space=pltpu.SMEM),
            pl.BlockSpec((tm * TOP_K,), lambda i: (jnp.minimum(i + 1, n_tiles - 1),), memory_space=pltpu.SMEM),
            pl.BlockSpec((tm, TOP_K), row),
            pl.BlockSpec((tm, D_MODEL), row),
            pl.BlockSpec((ROW_SUB * tm, LANES), row),
            pl.BlockSpec((None, 1, 6 * D_MODEL), mod_idx),
            pl.BlockSpec((D_MODEL, EXPERT_FF), const),
            pl.BlockSpec((D_MODEL, EXPERT_FF), const),
            pl.BlockSpec((EXPERT_FF, D_MODEL), const),
            pl.BlockSpec(memory_space=pl.ANY),
        ],
        out_specs=(pl.BlockSpec((tm, D_MODEL), lambda i: (jnp.minimum(i, n_ctx_tiles - 1), 0)),
                   pl.BlockSpec((tm, D_MODEL), lambda i: (jnp.maximum(i - n_ctx_tiles, 0), 0))),
        scratch_shapes=[pltpu.VMEM((2, TOP_K, ROW_SUB * tm, LANES), U32),
                        pltpu.SemaphoreType.DMA((2,))],
        compiler_params=_params(("arbitrary",)),
        name="combine",
    )(slots_flat, slots_flat, gw, x1, h2, mod3, P["w_s_gate_bf16"][l], P["w_s_up_bf16"][l],
      P["w_s_down_bf16"][l], ys)


def _rope_tables(lat_t):
    rows = lat_t // GRID_W
    row = jnp.repeat(jnp.arange(rows, dtype=F32), GRID_W)
    col = jnp.tile(jnp.arange(GRID_W, dtype=F32), rows)
    n_freq = HEAD_DIM // 4
    inv = jnp.power(ROPE_BASE, -jnp.arange(n_freq, dtype=F32) / n_freq)
    ang = jnp.concatenate([row[:, None] * inv, col[:, None] * inv], axis=-1)
    cos = jnp.repeat(jnp.cos(ang), 2, axis=-1)
    sign = jnp.tile(jnp.array([-1.0, 1.0], F32), HEAD_DIM // 2)
    sin = jnp.repeat(jnp.sin(ang), 2, axis=-1) * sign
    cos = jnp.concatenate([cos, jnp.ones((TOKEN_TILE, HEAD_DIM), F32)], axis=0)
    sin = jnp.concatenate([sin, jnp.zeros((TOKEN_TILE, HEAD_DIM), F32)], axis=0)
    reps = LANES // HEAD_DIM
    return jnp.tile(cos, (1, reps)), jnp.tile(sin, (1, reps))


def _block_diag_gates(w_a, w_x):
    def dense(w):
        eye = jnp.eye(LRU_HEADS, dtype=w.dtype)
        return jnp.einsum("ncd,nm->ncmd", w, eye).reshape(LRU_WIDTH, LRU_WIDTH)
    return jnp.concatenate([dense(w_a[0]), dense(w_x[0]), dense(w_a[1]), dense(w_x[1])], axis=1)


def _prepare(P):
    Q = dict(P)
    Q["w_in_bf16"] = P["w_in"].astype(BF16)
    Q["w_out_bf16"] = P["w_out"].astype(BF16)
    Q["q_norm_g_t"] = jnp.tile(P["q_norm_g"], (1, N_HEADS))
    Q["k_norm_g_t"] = jnp.tile(P["k_norm_g"], (1, N_KV_HEADS))
    head = jnp.arange(ATTN_WIDTH) // HEAD_DIM
    Q["head_blockdiag"] = (head[:, None] == head[None, :]).astype(BF16)
    Q["lru_gate_w"] = jnp.stack([_block_diag_gates(P["lru_w_a"][l], P["lru_w_x"][l])
                                 for l in range(DEPTH)]).astype(BF16)
    Q["lru_gate_b"] = jnp.concatenate([P["lru_b_a"][:, 0], P["lru_b_x"][:, 0],
                                       P["lru_b_a"][:, 1], P["lru_b_x"][:, 1]], axis=-1)
    w_hi = P["w_router"].astype(BF16)
    Q["w_router_hi"] = w_hi
    Q["w_router_lo"] = (P["w_router"] - w_hi.astype(F32)).astype(BF16)
    Q["w_s_gate_bf16"] = P["w_s_gate"].astype(BF16)
    Q["w_s_up_bf16"] = P["w_s_up"].astype(BF16)
    Q["w_s_down_bf16"] = P["w_s_down"].astype(BF16)
    return Q


def _group_tables(counts):
    bm = EXPERT_BLOCK
    blocks = (counts.astype(I32) + bm - 1) // bm
    first_blk = jnp.concatenate([jnp.zeros((1,), I32), jnp.cumsum(blocks).astype(I32)])
    group_start = (first_blk[:N_EXPERTS] * bm).astype(F32)[None, :]
    return group_start, first_blk


def _trunk(x_prompt, x_sample, c, cache_k, cache_v, state_lru, c_ctx, P):
    n_seq_c, ctx_t, _ = x_prompt.shape
    n_seq_l, lat_t, _ = x_sample.shape
    past = cache_k.shape[2]
    n_ctx = n_seq_c * ctx_t
    n_lat = n_seq_l * lat_t
    n = n_ctx + n_lat
    assert n_ctx % lat_t == 0, "latent sequences must start on a whole-sequence boundary of the merged token axis"
    lat_off = n_ctx // lat_t
    P = _prepare(P)

    n_cond = -(-(1 + n_seq_l) // SUBLANES) * SUBLANES
    cond = jnp.concatenate([c_ctx[None, :], c, jnp.zeros((n_cond - 1 - n_seq_l, D_MODEL), F32)], axis=0)
    mods = _mod_call(cond, P["w_mod"], P["b_mod"])
    rope_cos, rope_sin = _rope_tables(lat_t)
    tri = (jnp.arange(ROUTE_TILE)[:, None] > jnp.arange(ROUTE_TILE)[None, :]).astype(BF16)
    n_blocks = n * TOP_K // EXPERT_BLOCK + N_EXPERTS
    n_slots = n_blocks * EXPERT_BLOCK

    x_c = x_prompt.reshape(n_ctx, D_MODEL)
    x_l = x_sample.reshape(n_lat, D_MODEL)
    ks, vs, ss = [], [], []
    for l in range(DEPTH):
        mod3 = mods[l].reshape(n_cond, 1, 6 * D_MODEL)
        q, kr, vb, kc, vc, u, gate, cv = _premix_call(x_c, x_l, mod3, l, P, rope_cos, rope_sin, n_ctx, lat_t)
        ks.append(kc[:n_ctx].reshape(n_seq_c, ctx_t, N_KV_HEADS, HEAD_DIM))
        vs.append(vc[:n_ctx].reshape(n_seq_c, ctx_t, N_KV_HEADS, HEAD_DIM))

        as_ctx = lambda a: a.reshape(n // ctx_t, ctx_t, a.shape[-1])
        as_lat = lambda a: a.reshape(n // lat_t, lat_t, a.shape[-1])

        attn_c = _attn_call(as_ctx(q), as_ctx(kr), as_ctx(vb), min(ctx_t, 256), n_seq_c, 0, 0)
        k_all = jnp.concatenate([kr[n_ctx:].reshape(n_seq_l, lat_t, KV_WIDTH),
                                 cache_k[:, l].reshape(n_seq_l, past, KV_WIDTH).astype(BF16)], axis=1)
        cv_ones = jnp.concatenate([cache_v[:, l], jnp.ones_like(cache_v[:, l])], axis=-1)
        v_all = jnp.concatenate([vb[n_ctx:].reshape(n_seq_l, lat_t, 2 * KV_WIDTH),
                                 cv_ones.reshape(n_seq_l, past, 2 * KV_WIDTH).astype(BF16)], axis=1)
        attn_l = _attn_call(as_lat(q), k_all, v_all, 256, n_seq_l, lat_off, 0)

        lru_c, fin_c = _lru_call(as_ctx(u), as_ctx(gate), jnp.zeros((n_seq_c, 2, LRU_WIDTH), F32), l, P,
                                 n_seq_c, 0)
        lru_l, _ = _lru_call(as_lat(u), as_lat(gate), state_lru[:, l], l, P, n_seq_l, lat_off)
        ss.append(fin_c)
        conv_c = _convmod_call(as_ctx(cv), l, P, n_seq_c, 0)
        conv_l = _convmod_call(as_lat(cv), l, P, n_seq_l, lat_off)

        x1, h2, logits = _postmix_call(
            attn_c.reshape(n_ctx, ATTN_WIDTH), attn_l.reshape(n_lat, ATTN_WIDTH),
            lru_c.reshape(n_ctx, LRU_WIDTH), lru_l.reshape(n_lat, LRU_WIDTH),
            conv_c.reshape(n_ctx, CONV_WIDTH), conv_l.reshape(n_lat, CONV_WIDTH),
            x_c, x_l, mod3, l, P, n_ctx, lat_t)

        ids, gw, rank, counts = _route_call(logits, P["b_router"][l], tri)
        group_start, first_blk = _group_tables(counts[0])
        slots = _slots_call(ids, rank, group_start).reshape(-1)
        xs = _dispatch_call(first_blk, slots, h2, n_slots)
        ys = _expert_call(first_blk, xs, P["w_e_gate"], P["w_e_up"], P["w_e_down"], l)
        x_c, x_l = _combine_call(slots, gw, x1, h2, mod3, ys, l, P, n_ctx, lat_t)

    y_prompt = x_c.reshape(n_seq_c, ctx_t, D_MODEL)
    y_sample = x_l.reshape(n_seq_l, lat_t, D_MODEL)
    return (y_prompt, y_sample, jnp.stack(ks, axis=1), jnp.stack(vs, axis=1), jnp.stack(ss, axis=1))


def kernel(x_prompt, x_sample, c, cache_k, cache_v, state_lru, c_ctx, w_mod, b_mod, norm1_g, w_in, q_norm_g, k_norm_g, lru_conv_w, lru_conv_b, lru_w_a, lru_b_a, lru_w_x, lru_b_x, lru_lambda, cm_dw_w, cm_dw_b, cm_ln_g, cm_ln_b, out_norm_g, w_out, norm2_g, w_router, b_router, w_e_gate, w_e_up, w_e_down, w_s_gate, w_s_up, w_s_down):
    P = {"w_mod": w_mod, "b_mod": b_mod, "norm1_g": norm1_g, "w_in": w_in, "q_norm_g": q_norm_g,
         "k_norm_g": k_norm_g, "lru_conv_w": lru_conv_w, "lru_conv_b": lru_conv_b, "lru_w_a": lru_w_a,
         "lru_b_a": lru_b_a, "lru_w_x": lru_w_x, "lru_b_x": lru_b_x, "lru_lambda": lru_lambda,
         "cm_dw_w": cm_dw_w, "cm_dw_b": cm_dw_b, "cm_ln_g": cm_ln_g, "cm_ln_b": cm_ln_b,
         "out_norm_g": out_norm_g, "w_out": w_out, "norm2_g": norm2_g, "w_router": w_router,
         "b_router": b_router, "w_e_gate": w_e_gate, "w_e_up": w_e_up, "w_e_down": w_e_down,
         "w_s_gate": w_s_gate, "w_s_up": w_s_up, "w_s_down": w_s_down}
    return _trunk(x_prompt, x_sample, c, cache_k, cache_v, state_lru, c_ctx, P)
```

```python
import functools
import math

import jax
import jax.numpy as jnp
from jax import lax
from jax.experimental import pallas as pl
from jax.experimental.pallas import tpu as pltpu

F32 = jnp.float32
BF16 = jnp.bfloat16
I32 = jnp.int32

D_MODEL = 1024
DEPTH = 2
GRID_W = 64
ATTN_WIDTH = 512
LRU_WIDTH = 256
CONV_WIDTH = 256
HEAD_DIM = 64
N_HEADS = 8
N_KV_HEADS = 2
KV_GROUP = N_HEADS // N_KV_HEADS
KV_WIDTH = N_KV_HEADS * HEAD_DIM
ROPE_BASE = 10000.0
LRU_HEADS = 4
LRU_HEAD_DIM = LRU_WIDTH // LRU_HEADS
LRU_CONV = 4
RG_C = 8.0
CM_KERNEL = 31
N_EXPERTS = 256
TOP_K = 8
EXPERT_FF = 256
ROUTED_SCALE = 2.5
EPS = 1e-6
IN_WIDTH = ATTN_WIDTH + 2 * KV_WIDTH + 2 * LRU_WIDTH + 2 * CONV_WIDTH

SUBLANES = 8
LANES = 128
VMEM_LIMIT = 56 * 1024 * 1024

TOKEN_TILE = 512
ROUTE_TILE = 512
SCAN_CHUNK = 256
CONV_CHUNK = 128
CONV_HALO = 16
LRU_HALO = 8
EXPERT_BLOCK = 256
EXPERT_CHUNK = 2
EXPERT_IN_SLOTS = 6
EXPERT_OUT_SLOTS = 4
COMBINE_TILE = 256
DISPATCH_TILE = 512


def _params(sem):
    return pltpu.CompilerParams(dimension_semantics=sem, vmem_limit_bytes=VMEM_LIMIT)


def _sigmoid(x):
    return 1.0 / (1.0 + jnp.exp(-x))


def _silu(x):
    return x * _sigmoid(x)


def _bdot(a, b):
    return jnp.dot(a.astype(BF16), b.astype(BF16), preferred_element_type=F32)


def _split_dot(a, b_bf16):
    hi = a.astype(BF16)
    lo = (a - hi.astype(F32)).astype(BF16)
    return (jnp.dot(hi, b_bf16, preferred_element_type=F32)
            + jnp.dot(lo, b_bf16, preferred_element_type=F32))


U32 = jnp.uint32
PACKED = D_MODEL // 2


def _pack_bf16_pair(x):
    hi = lax.bitcast_convert_type(x[:, :PACKED].astype(BF16).astype(F32), U32)
    lo = lax.bitcast_convert_type(x[:, PACKED:].astype(BF16).astype(F32), U32)
    return (hi & jnp.uint32(0xFFFF0000)) | (lo >> 16)


def _unpack_bf16_pair(w):
    hi = lax.bitcast_convert_type(w & jnp.uint32(0xFFFF0000), F32)
    lo = lax.bitcast_convert_type(w << 16, F32)
    return hi, lo


ROW_SUB = PACKED // LANES


def _store_rows(ref, w):
    r = w.shape[0]
    for j in range(ROW_SUB):
        ref[pl.ds(j, r, stride=ROW_SUB), :] = w[:, j * LANES:(j + 1) * LANES]


def _load_rows(ref, r):
    his, los = [], []
    for j in range(ROW_SUB):
        hi, lo = _unpack_bf16_pair(ref[pl.ds(j, r, stride=ROW_SUB), :])
        his.append(hi)
        los.append(lo)
    return jnp.concatenate(his, axis=1), jnp.concatenate(los, axis=1)


def _rms(x, g):
    return x * lax.rsqrt(jnp.mean(x * x, axis=-1, keepdims=True) + EPS) * g


MOD_COLS = 1536


def _mod_kernel(c_ref, w_ref, b_ref, o_ref):
    a = _silu(c_ref[...])
    o_ref[...] = jnp.dot(a, w_ref[...], preferred_element_type=F32,
                         precision=lax.Precision.HIGHEST) + b_ref[...]


def _mod_call(cond, w_mod, b_mod):
    n_cond = cond.shape[0]
    width = 6 * D_MODEL
    return pl.pallas_call(
        _mod_kernel,
        out_shape=jax.ShapeDtypeStruct((DEPTH, n_cond, width), F32),
        grid=(DEPTH, width // MOD_COLS),
        in_specs=[
            pl.BlockSpec((n_cond, D_MODEL), lambda l, j: (0, 0)),
            pl.BlockSpec((None, D_MODEL, MOD_COLS), lambda l, j: (l, 0, j)),
            pl.BlockSpec((None, 1, MOD_COLS), lambda l, j: (l, 0, j)),
        ],
        out_specs=pl.BlockSpec((None, n_cond, MOD_COLS), lambda l, j: (l, 0, j)),
        compiler_params=_params(("arbitrary", "arbitrary")),
        name="mod",
    )(cond, w_mod, b_mod.reshape(DEPTH, 1, width))


def _swap_pairs(x):
    n = x.shape[-1]
    lane = lax.broadcasted_iota(I32, x.shape, 1)
    nxt = pltpu.roll(x, n - 1, 1)
    prv = pltpu.roll(x, 1, 1)
    return jnp.where(lane % 2 == 0, nxt, prv)


def _premix_kernel(n_ctx_tiles, xc_ref, xl_ref, mod_ref, n1g_ref, w_ref, qg_ref, kg_ref, bd_ref, cos_ref, sin_ref,
                   q_ref, kr_ref, vb_ref, kc_ref, vc_ref, u_ref, gate_ref, cv_ref):
    x = jnp.where(pl.program_id(0) < n_ctx_tiles, xc_ref[...], xl_ref[...])
    sh1 = mod_ref[:, 0:D_MODEL]
    sc1 = mod_ref[:, D_MODEL:2 * D_MODEL]
    h = _rms(x, n1g_ref[...]) * (1.0 + sc1) + sh1
    z = jnp.dot(h.astype(BF16), w_ref[...], preferred_element_type=F32)
    o = 0
    q = z[:, o:o + ATTN_WIDTH]; o += ATTN_WIDTH
    k = z[:, o:o + KV_WIDTH]; o += KV_WIDTH
    v = z[:, o:o + KV_WIDTH]; o += KV_WIDTH
    u_ref[...] = z[:, o:o + LRU_WIDTH]; o += LRU_WIDTH
    gate_ref[...] = z[:, o:o + LRU_WIDTH]; o += LRU_WIDTH
    cv_ref[...] = z[:, o:o + 2 * CONV_WIDTH]

    bd = bd_ref[...]
    inv_hd = 1.0 / HEAD_DIM
    q_ms = _split_dot(q * q, bd) * inv_hd
    k_ms = _split_dot(k * k, bd[:KV_WIDTH, :KV_WIDTH]) * inv_hd
    qn = q * lax.rsqrt(q_ms + EPS) * qg_ref[...]
    kn = k * lax.rsqrt(k_ms + EPS) * kg_ref[...]
    kc_ref[...] = kn
    vc_ref[...] = v
    ones = jnp.ones((v.shape[0], HEAD_DIM), F32)
    vb_ref[...] = jnp.concatenate([v[:, :HEAD_DIM], ones, v[:, HEAD_DIM:], ones], axis=1).astype(BF16)

    cos = cos_ref[...]
    sin = sin_ref[...]
    reps = ATTN_WIDTH // cos.shape[-1]
    cos_q = jnp.concatenate([cos] * reps, axis=1)
    sin_q = jnp.concatenate([sin] * reps, axis=1)
    qr = qn * cos_q + _swap_pairs(qn) * sin_q
    kr = kn * cos + _swap_pairs(kn) * sin
    q_ref[...] = (qr * (HEAD_DIM ** -0.5 * math.log2(math.e))).astype(BF16)
    kr_ref[...] = kr.astype(BF16)


def _premix_call(x_c, x_l, mod3, l, P, rope_cos, rope_sin, n_ctx, lat_t):
    n = x_c.shape[0] + x_l.shape[0]
    tm = TOKEN_TILE
    n_ctx_tiles = n_ctx // tm
    tiles_per_seq = lat_t // tm

    def mod_idx(i):
        return (jnp.where(i < n_ctx_tiles, 0, 1 + (i - n_ctx_tiles) // tiles_per_seq), 0, 0)

    def rope_idx(i):
        return (jnp.where(i < n_ctx_tiles, tiles_per_seq, (i - n_ctx_tiles) % tiles_per_seq), 0)

    const = lambda i: (0, 0)
    row = lambda i: (i, 0)
    row_c = lambda i: (jnp.minimum(i, n_ctx_tiles - 1), 0)
    row_l = lambda i: (jnp.maximum(i - n_ctx_tiles, 0), 0)
    outs = pl.pallas_call(
        functools.partial(_premix_kernel, n_ctx_tiles),
        out_shape=(
            jax.ShapeDtypeStruct((n, ATTN_WIDTH), BF16),
            jax.ShapeDtypeStruct((n, KV_WIDTH), BF16),
            jax.ShapeDtypeStruct((n, 2 * KV_WIDTH), BF16),
            jax.ShapeDtypeStruct((n, KV_WIDTH), F32),
            jax.ShapeDtypeStruct((n, KV_WIDTH), F32),
            jax.ShapeDtypeStruct((n, LRU_WIDTH), F32),
            jax.ShapeDtypeStruct((n, LRU_WIDTH), F32),
            jax.ShapeDtypeStruct((n, 2 * CONV_WIDTH), F32),
        ),
        grid=(n // tm,),
        in_specs=[
            pl.BlockSpec((tm, D_MODEL), row_c),
            pl.BlockSpec((tm, D_MODEL), row_l),
            pl.BlockSpec((None, 1, 6 * D_MODEL), mod_idx),
            pl.BlockSpec((1, D_MODEL), const),
            pl.BlockSpec((D_MODEL, IN_WIDTH), const),
            pl.BlockSpec((1, ATTN_WIDTH), const),
            pl.BlockSpec((1, KV_WIDTH), const),
            pl.BlockSpec((ATTN_WIDTH, ATTN_WIDTH), const),
            pl.BlockSpec((tm, LANES), rope_idx),
            pl.BlockSpec((tm, LANES), rope_idx),
        ],
        out_specs=(
            pl.BlockSpec((tm, ATTN_WIDTH), row),
            pl.BlockSpec((tm, KV_WIDTH), row),
            pl.BlockSpec((tm, 2 * KV_WIDTH), row),
            pl.BlockSpec((tm, KV_WIDTH), row),
            pl.BlockSpec((tm, KV_WIDTH), row),
            pl.BlockSpec((tm, LRU_WIDTH), row),
            pl.BlockSpec((tm, LRU_WIDTH), row),
            pl.BlockSpec((tm, 2 * CONV_WIDTH), row),
        ),
        compiler_params=_params(("parallel",)),
        name="premix",
    )(x_c, x_l, mod3, P["norm1_g"][l][None, :], P["w_in_bf16"][l], P["q_norm_g_t"][l][None, :],
      P["k_norm_g_t"][l][None, :], P["head_blockdiag"], rope_cos, rope_sin)
    return outs


def _attn_kernel(q_ref, k_ref, v_ref, o_ref):
    outs = []
    for kh in range(N_KV_HEADS):
        k = k_ref[:, kh * HEAD_DIM:(kh + 1) * HEAD_DIM]
        v = v_ref[:, kh * 2 * HEAD_DIM:(kh + 1) * 2 * HEAD_DIM]
        for g in range(KV_GROUP):
            hd = kh * KV_GROUP + g
            q = q_ref[:, hd * HEAD_DIM:(hd + 1) * HEAD_DIM]
            s = lax.dot_general(q, k, (((1,), (1,)), ((), ())), preferred_element_type=F32)
            m = jnp.max(s, axis=-1, keepdims=True)
            p = jnp.exp2(s - m).astype(BF16)
            o = jnp.dot(p, v, preferred_element_type=F32)
            outs.append(o[:, :HEAD_DIM] / o[:, HEAD_DIM:HEAD_DIM + 1])
    o_ref[...] = jnp.concatenate(outs, axis=1)


def _attn_call(q3, k3, v3, tq, b, q_off, kv_off):
    s = q3.shape[1]
    t = k3.shape[1]
    return pl.pallas_call(
        _attn_kernel,
        out_shape=jax.ShapeDtypeStruct((b, s, ATTN_WIDTH), F32),
        grid=(b, s // tq),
        in_specs=[
            pl.BlockSpec((None, tq, ATTN_WIDTH), lambda i, j: (i + q_off, j, 0)),
            pl.BlockSpec((None, t, KV_WIDTH), lambda i, j: (i + kv_off, 0, 0)),
            pl.BlockSpec((None, t, 2 * KV_WIDTH), lambda i, j: (i + kv_off, 0, 0)),
        ],
        out_specs=pl.BlockSpec((None, tq, ATTN_WIDTH), lambda i, j: (i, j, 0)),
        compiler_params=_params(("parallel", "parallel")),
        name="attention",
    )(q3, k3, v3)


def _tap_sum(win, taps, bias, first, n_taps, rows):
    n = win.shape[0]
    acc = bias
    for phase in range(SUBLANES):
        starts = [first + j for j in range(n_taps) if (first + j) % SUBLANES == phase]
        if not starts:
            continue
        rolled = win if phase == 0 else pltpu.roll(win, n - phase, 0)
        for s0 in starts:
            base = s0 - phase
            acc = acc + taps[s0 - first:s0 - first + 1, :] * rolled[base:base + rows, :]
    return acc


def _chunk_scan(a, b, reverse):
    n = a.shape[0]
    row = lax.broadcasted_iota(I32, a.shape, 0)
    d = 1
    while d < n:
        if reverse:
            a_s = pltpu.roll(a, n - d, 0)
            b_s = pltpu.roll(b, n - d, 0)
            ok = row < n - d
        else:
            a_s = pltpu.roll(a, d, 0)
            b_s = pltpu.roll(b, d, 0)
            ok = row >= d
        b = jnp.where(ok, a * b_s + b, b)
        a = jnp.where(ok, a * a_s, a)
        d *= 2
    return a, b


def _gelu_tanh(x):
    return 0.5 * x * (1.0 + jnp.tanh(math.sqrt(2.0 / math.pi) * (x + 0.044715 * (x * x * x))))


def _lru_kernel(u_ref, gate_ref, cw_ref, cb_ref, wg_ref, bg_ref, lam_ref, h0_ref,
                y_ref, fin_ref, upad, xc_scr, fwd_scr):
    t_len = u_ref.shape[0]
    ch = SCAN_CHUNK
    n_chunks = t_len // ch
    w = LRU_WIDTH
    zeros_halo = jnp.zeros((LRU_HALO, w), F32)
    upad[0:LRU_HALO, :] = zeros_halo
    upad[t_len + LRU_HALO:t_len + 2 * LRU_HALO, :] = zeros_halo

    def fill(c, carry):
        r0 = pl.multiple_of(c * ch, ch)
        upad[pl.ds(r0 + LRU_HALO, ch), :] = u_ref[pl.ds(r0, ch), :]
        return carry

    lax.fori_loop(0, n_chunks, fill, 0)

    lam = lam_ref[...]
    nlam = -lam
    softplus = jnp.maximum(nlam, 0.0) + jnp.log(1.0 + jnp.exp(-jnp.abs(nlam)))
    decay = -RG_C * softplus
    cw = cw_ref[...]
    cb = cb_ref[...]

    def gates(xc, d):
        z = jnp.dot(xc.astype(BF16), wg_ref[:, 2 * d * w:2 * (d + 1) * w],
                    preferred_element_type=F32) + bg_ref[:, 2 * d * w:2 * (d + 1) * w]
        r = _sigmoid(z[:, :w])
        i = _sigmoid(z[:, w:])
        log_a = decay[d:d + 1, :] * r
        a = jnp.exp(log_a)
        b = jnp.sqrt(-jnp.tanh(log_a) * (a * a + 1.0)) * (i * xc)
        return a, b

    def fwd(c, h):
        r0 = pl.multiple_of(c * ch, ch)
        win = upad[pl.ds(r0, ch + 2 * LRU_HALO), :]
        xc = _tap_sum(win, cw, cb, LRU_HALO - (LRU_CONV - 1) // 2, LRU_CONV, ch)
        xc_scr[pl.ds(r0, ch), :] = xc
        a, b = gates(xc, 0)
        a_cum, b_cum = _chunk_scan(a, b, reverse=False)
        hs = a_cum * h + b_cum
        fwd_scr[pl.ds(r0, ch), :] = hs
        return hs[ch - 1:ch, :]

    h_f = lax.fori_loop(0, n_chunks, fwd, h0_ref[0:1, :])

    def bwd(ci, h):
        c = n_chunks - 1 - ci
        r0 = pl.multiple_of(c * ch, ch)
        xc = xc_scr[pl.ds(r0, ch), :]
        a, b = gates(xc, 1)
        a_cum, b_cum = _chunk_scan(a, b, reverse=True)
        hs = a_cum * h + b_cum
        y_ref[pl.ds(r0, ch), :] = (fwd_scr[pl.ds(r0, ch), :] + hs) * _gelu_tanh(gate_ref[pl.ds(r0, ch), :])
        return hs[0:1, :]

    h_b = lax.fori_loop(0, n_chunks, bwd, h0_ref[1:2, :])
    fin_ref[0:1, :] = h_f
    fin_ref[1:2, :] = h_b


def _lru_call(u3, gate3, h0, l, P, b, seq_off):
    _, t, w = u3.shape
    const = lambda i: (0, 0)
    seq = lambda i: (i, 0, 0)
    seq_in = lambda i: (i + seq_off, 0, 0)
    return pl.pallas_call(
        _lru_kernel,
        out_shape=(jax.ShapeDtypeStruct((b, t, w), F32),
                   jax.ShapeDtypeStruct((b, 2, w), F32)),
        grid=(b,),
        in_specs=[
            pl.BlockSpec((None, t, w), seq_in),
            pl.BlockSpec((None, t, w), seq_in),
            pl.BlockSpec((LRU_CONV, w), const),
            pl.BlockSpec((1, w), const),
            pl.BlockSpec((w, 4 * w), const),
            pl.BlockSpec((1, 4 * w), const),
            pl.BlockSpec((2, w), const),
            pl.BlockSpec((None, 2, w), seq),
        ],
        out_specs=(pl.BlockSpec((None, t, w), seq),
                   pl.BlockSpec((None, 2, w), seq)),
        scratch_shapes=[pltpu.VMEM((t + 2 * LRU_HALO, w), F32),
                        pltpu.VMEM((t, w), F32),
                        pltpu.VMEM((t, w), F32)],
        compiler_params=_params(("parallel",)),
        name="rglru",
    )(u3, gate3, P["lru_conv_w"][l], P["lru_conv_b"][l][None, :], P["lru_gate_w"][l],
      P["lru_gate_b"][l][None, :], P["lru_lambda"][l], h0)


def _convmod_kernel(cv_ref, w_ref, b_ref, g_ref, beta_ref, o_ref, hpad):
    t_len = cv_ref.shape[0]
    ch = CONV_CHUNK
    n_chunks = t_len // ch
    w = CONV_WIDTH
    zeros_halo = jnp.zeros((CONV_HALO, w), F32)
    hpad[0:CONV_HALO, :] = zeros_halo
    hpad[t_len + CONV_HALO:t_len + 2 * CONV_HALO, :] = zeros_halo

    def glu(c, carry):
        r0 = pl.multiple_of(c * ch, ch)
        blk = cv_ref[pl.ds(r0, ch), :]
        hpad[pl.ds(r0 + CONV_HALO, ch), :] = blk[:, :w] * _sigmoid(blk[:, w:])
        return carry

    lax.fori_loop(0, n_chunks, glu, 0)

    taps = w_ref[...]
    bias = b_ref[...]
    gamma = g_ref[...]
    beta = beta_ref[...]

    def conv(c, carry):
        r0 = pl.multiple_of(c * ch, ch)
        win = hpad[pl.ds(r0, ch + 2 * CONV_HALO), :]
        acc = _tap_sum(win, taps, bias, CONV_HALO - CM_KERNEL // 2, CM_KERNEL, ch)
        mu = jnp.mean(acc, axis=-1, keepdims=True)
        cen = acc - mu
        var = jnp.mean(cen * cen, axis=-1, keepdims=True)
        y = cen * lax.rsqrt(var + EPS) * gamma + beta
        o_ref[pl.ds(r0, ch), :] = _silu(y)
        return carry

    lax.fori_loop(0, n_chunks, conv, 0)


def _convmod_call(cv3, l, P, b, seq_off):
    t = cv3.shape[1]
    w = CONV_WIDTH
    const = lambda i: (0, 0)
    seq = lambda i: (i, 0, 0)
    return pl.pallas_call(
        _convmod_kernel,
        out_shape=jax.ShapeDtypeStruct((b, t, w), F32),
        grid=(b,),
        in_specs=[
            pl.BlockSpec((None, t, 2 * w), lambda i: (i + seq_off, 0, 0)),
            pl.BlockSpec((CM_KERNEL, w), const),
            pl.BlockSpec((1, w), const),
            pl.BlockSpec((1, w), const),
            pl.BlockSpec((1, w), const),
        ],
        out_specs=pl.BlockSpec((None, t, w), seq),
        scratch_shapes=[pltpu.VMEM((t + 2 * CONV_HALO, w), F32)],
        compiler_params=_params(("parallel",)),
        name="convmod",
    )(cv3, P["cm_dw_w"][l], P["cm_dw_b"][l][None, :], P["cm_ln_g"][l][None, :], P["cm_ln_b"][l][None, :])


def _postmix_kernel(n_ctx_tiles, attn_c_ref, attn_l_ref, lru_c_ref, lru_l_ref, conv_c_ref, conv_l_ref,
                    xc_ref, xl_ref, mod_ref, og_ref, wo_ref, n2g_ref, wr_hi_ref, wr_lo_ref,
                    x1_ref, h2_ref, lg_ref):
    is_ctx = pl.program_id(0) < n_ctx_tiles
    og = og_ref[...]
    a0, a1, a2 = ATTN_WIDTH, ATTN_WIDTH + LRU_WIDTH, D_MODEL
    y = jnp.concatenate([_rms(jnp.where(is_ctx, attn_c_ref[...], attn_l_ref[...]), og[:, :a0]),
                         _rms(jnp.where(is_ctx, lru_c_ref[...], lru_l_ref[...]), og[:, a0:a1]),
                         _rms(jnp.where(is_ctx, conv_c_ref[...], conv_l_ref[...]), og[:, a1:a2])], axis=1)
    y = jnp.dot(y.astype(BF16), wo_ref[...], preferred_element_type=F32)
    g1 = mod_ref[:, 2 * D_MODEL:3 * D_MODEL]
    sh2 = mod_ref[:, 3 * D_MODEL:4 * D_MODEL]
    sc2 = mod_ref[:, 4 * D_MODEL:5 * D_MODEL]
    x1 = jnp.where(is_ctx, xc_ref[...], xl_ref[...]) + g1 * y
    x1_ref[...] = x1
    h2 = _rms(x1, n2g_ref[...]) * (1.0 + sc2) + sh2
    _store_rows(h2_ref, _pack_bf16_pair(h2))
    hi = h2.astype(BF16)
    lo = (h2 - hi.astype(F32)).astype(BF16)
    w_hi = wr_hi_ref[...]
    lg_ref[...] = (jnp.dot(hi, w_hi, preferred_element_type=F32)
                   + jnp.dot(lo, w_hi, preferred_element_type=F32)
                   + jnp.dot(hi, wr_lo_ref[...], preferred_element_type=F32))


def _postmix_call(attn_c, attn_l, lru_c, lru_l, conv_c, conv_l, x_c, x_l, mod3, l, P, n_ctx, lat_t):
    n = x_c.shape[0] + x_l.shape[0]
    tm = TOKEN_TILE
    n_ctx_tiles = n_ctx // tm
    tiles_per_seq = lat_t // tm

    def mod_idx(i):
        return (jnp.where(i < n_ctx_tiles, 0, 1 + (i - n_ctx_tiles) // tiles_per_seq), 0, 0)

    const = lambda i: (0, 0)
    row = lambda i: (i, 0)
    row_c = lambda i: (jnp.minimum(i, n_ctx_tiles - 1), 0)
    row_l = lambda i: (jnp.maximum(i - n_ctx_tiles, 0), 0)
    return pl.pallas_call(
        functools.partial(_postmix_kernel, n_ctx_tiles),
        out_shape=(jax.ShapeDtypeStruct((n, D_MODEL), F32),
                   jax.ShapeDtypeStruct((ROW_SUB * n, LANES), U32),
                   jax.ShapeDtypeStruct((n, N_EXPERTS), F32)),
        grid=(n // tm,),
        in_specs=[
            pl.BlockSpec((tm, ATTN_WIDTH), row_c),
            pl.BlockSpec((tm, ATTN_WIDTH), row_l),
            pl.BlockSpec((tm, LRU_WIDTH), row_c),
            pl.BlockSpec((tm, LRU_WIDTH), row_l),
            pl.BlockSpec((tm, CONV_WIDTH), row_c),
            pl.BlockSpec((tm, CONV_WIDTH), row_l),
            pl.BlockSpec((tm, D_MODEL), row_c),
            pl.BlockSpec((tm, D_MODEL), row_l),
            pl.BlockSpec((None, 1, 6 * D_MODEL), mod_idx),
            pl.BlockSpec((1, D_MODEL), const),
            pl.BlockSpec((D_MODEL, D_MODEL), const),
            pl.BlockSpec((1, D_MODEL), const),
            pl.BlockSpec((D_MODEL, N_EXPERTS), const),
            pl.BlockSpec((D_MODEL, N_EXPERTS), const),
        ],
        out_specs=(pl.BlockSpec((tm, D_MODEL), row),
                   pl.BlockSpec((ROW_SUB * tm, LANES), row),
                   pl.BlockSpec((tm, N_EXPERTS), row)),
        compiler_params=_params(("parallel",)),
        name="postmix",
    )(attn_c, attn_l, lru_c, lru_l, conv_c, conv_l, x_c, x_l, mod3, P["out_norm_g"][l][None, :], P["w_out_bf16"][l],
      P["norm2_g"][l][None, :], P["w_router_hi"][l], P["w_router_lo"][l])


def _route_kernel(lg_ref, br_ref, tri_ref, ids_ref, gw_ref, rank_ref, cnt_ref, carry):
    i = pl.program_id(0)

    @pl.when(i == 0)
    def _():
        carry[...] = jnp.zeros_like(carry)

    scores = _sigmoid(lg_ref[...])
    sel = scores + br_ref[...]
    tm = scores.shape[0]
    lane = lax.broadcasted_iota(I32, (tm, N_EXPERTS), 1).astype(F32)
    slot_lane = lax.broadcasted_iota(I32, (tm, LANES), 1)
    ids_acc = jnp.zeros((tm, LANES), F32)
    gw_acc = jnp.zeros((tm, LANES), F32)
    hot = jnp.zeros((tm, N_EXPERTS), F32)
    picked = []
    for k in range(TOP_K):
        m = jnp.max(sel, axis=-1, keepdims=True)
        idx = jnp.min(jnp.where(sel == m, lane, float(N_EXPERTS)), axis=-1, keepdims=True)
        one = lane == idx
        g = jnp.sum(jnp.where(one, scores, 0.0), axis=-1, keepdims=True)
        sel = jnp.where(one, -jnp.inf, sel)
        hot = jnp.where(one, 1.0, hot)
        ids_acc = jnp.where(slot_lane == k, idx, ids_acc)
        gw_acc = jnp.where(slot_lane == k, g, gw_acc)
        picked.append(idx)
    denom = jnp.sum(gw_acc, axis=-1, keepdims=True)
    gw_acc = ROUTED_SCALE * gw_acc / denom

    before = jnp.dot(tri_ref[...], hot.astype(BF16), preferred_element_type=F32) + carry[0:1, :]
    rank_acc = jnp.zeros((tm, LANES), F32)
    for k in range(TOP_K):
        r = jnp.sum(jnp.where(lane == picked[k], before, 0.0), axis=-1, keepdims=True)
        rank_acc = jnp.where(slot_lane == k, r, rank_acc)
    carry[0:1, :] = carry[0:1, :] + jnp.sum(hot, axis=0, keepdims=True)
    cnt_ref[...] = carry[...]
    ids_ref[...] = ids_acc[:, :TOP_K].astype(I32)
    gw_ref[...] = gw_acc[:, :TOP_K]
    rank_ref[...] = rank_acc[:, :TOP_K].astype(I32)


def _route_call(logits, b_router_l, tri):
    n = logits.shape[0]
    tm = ROUTE_TILE
    const = lambda i: (0, 0)
    row = lambda i: (i, 0)
    return pl.pallas_call(
        _route_kernel,
        out_shape=(jax.ShapeDtypeStruct((n, TOP_K), I32),
                   jax.ShapeDtypeStruct((n, TOP_K), F32),
                   jax.ShapeDtypeStruct((n, TOP_K), I32),
                   jax.ShapeDtypeStruct((SUBLANES, N_EXPERTS), F32)),
        grid=(n // tm,),
        in_specs=[
            pl.BlockSpec((tm, N_EXPERTS), row),
            pl.BlockSpec((1, N_EXPERTS), const),
            pl.BlockSpec((tm, tm), const),
        ],
        out_specs=(pl.BlockSpec((tm, TOP_K), row),
                   pl.BlockSpec((tm, TOP_K), row),
                   pl.BlockSpec((tm, TOP_K), row),
                   pl.BlockSpec((SUBLANES, N_EXPERTS), const)),
        scratch_shapes=[pltpu.VMEM((SUBLANES, N_EXPERTS), F32)],
        compiler_params=_params(("arbitrary",)),
        name="route",
    )(logits, b_router_l[None, :], tri)


def _slots_kernel(ids_ref, rank_ref, start_ref, slots_ref):
    ids = ids_ref[...].astype(F32)
    tm = ids.shape[0]
    lane = lax.broadcasted_iota(I32, (tm, N_EXPERTS), 1).astype(F32)
    slot_lane = lax.broadcasted_iota(I32, (tm, TOP_K), 1)
    start = start_ref[...]
    acc = jnp.zeros((tm, TOP_K), F32)
    for k in range(TOP_K):
        s = jnp.sum(jnp.where(lane == ids[:, k:k + 1], start, 0.0), axis=-1, keepdims=True)
        acc = jnp.where(slot_lane == k, s, acc)
    slots_ref[...] = (acc.astype(I32) + rank_ref[...]) * ROW_SUB


def _slots_call(ids, rank, group_start):
    n = ids.shape[0]
    tm = ROUTE_TILE
    row = lambda i: (i, 0)
    return pl.pallas_call(
        _slots_kernel,
        out_shape=jax.ShapeDtypeStruct((n, TOP_K), I32),
        grid=(n // tm,),
        in_specs=[pl.BlockSpec((tm, TOP_K), row), pl.BlockSpec((tm, TOP_K), row),
                  pl.BlockSpec((1, N_EXPERTS), lambda i: (0, 0))],
        out_specs=pl.BlockSpec((tm, TOP_K), row),
        compiler_params=_params(("parallel",)),
        name="slots",
    )(ids, rank, group_start)


def _wait_rows(src_ref, dst_ref, sem, n_waits):
    for _ in range(n_waits):
        pltpu.make_async_copy(src_ref, dst_ref, sem).wait()


def _dispatch_kernel(first_ref, slots_ref, h_ref, xs_ref, zeros, sem_rows, sem_zero):
    n_tok = h_ref.shape[0] // ROW_SUB
    bm = EXPERT_BLOCK * ROW_SUB
    n_blocks = xs_ref.shape[0] // bm
    nused = first_ref[N_EXPERTS]

    def zero_copy(b):
        return pltpu.make_async_copy(zeros, xs_ref.at[pl.ds(pl.multiple_of(b * bm, bm), bm), :], sem_zero)

    @pl.when(pl.program_id(0) == 0)
    def _():
        zeros[...] = jnp.zeros_like(zeros)

        def group(start):
            def body(e, c):
                @pl.when(first_ref[e + 1] > first_ref[e])
                def _():
                    cp = zero_copy(first_ref[e + 1] - 1)
                    cp.start() if start else cp.wait()
                return c
            lax.fori_loop(0, N_EXPERTS, body, 0)

        def tail(start):
            def body(b, c):
                cp = zero_copy(b)
                cp.start() if start else cp.wait()
                return c
            lax.fori_loop(nused, n_blocks, body, 0)

        group(True)
        tail(True)
        group(False)
        tail(False)

    def issue(t, c):
        src = h_ref.at[pl.ds(pl.multiple_of(t * ROW_SUB, ROW_SUB), ROW_SUB), :]
        for k in range(TOP_K):
            row0 = pl.multiple_of(slots_ref[t * TOP_K + k], ROW_SUB)
            pltpu.make_async_copy(src, xs_ref.at[pl.ds(row0, ROW_SUB), :], sem_rows).start(priority=k % 2)
        return c

    lax.fori_loop(0, n_tok, issue, 0)
    _wait_rows(h_ref, xs_ref.at[pl.ds(0, n_tok * ROW_SUB), :], sem_rows, TOP_K)


def _dispatch_call(first_blk, slots_flat, h2, n_slots):
    n = h2.shape[0] // ROW_SUB
    tm = DISPATCH_TILE
    grid_spec = pltpu.PrefetchScalarGridSpec(
        num_scalar_prefetch=1,
        grid=(n // tm,),
        in_specs=[
            pl.BlockSpec((tm * TOP_K,), lambda i, first: (i,), memory_space=pltpu.SMEM),
            pl.BlockSpec((ROW_SUB * tm, LANES), lambda i, first: (i, 0)),
        ],
        out_specs=pl.BlockSpec(memory_space=pl.ANY),
        scratch_shapes=[pltpu.VMEM((ROW_SUB * EXPERT_BLOCK, LANES), U32),
                        pltpu.SemaphoreType.DMA,
                        pltpu.SemaphoreType.DMA],
    )
    return pl.pallas_call(
        _dispatch_kernel,
        out_shape=jax.ShapeDtypeStruct((ROW_SUB * n_slots, LANES), U32),
        grid_spec=grid_spec,
        compiler_params=_params(("arbitrary",)),
        name="dispatch",
    )(first_blk, slots_flat, h2)


def _expert_kernel(first_ref, xs_ref, wg_ref, wu_ref, wd_ref, ys_ref,
                   xbuf, ybuf, wg_bf, wu_bf, wd_bf, sem_in, sem_out):
    e = pl.program_id(0)
    bm = EXPERT_BLOCK * ROW_SUB
    n_blocks = ys_ref.shape[0] // bm
    lo = first_ref[e]
    hi = first_ref[e + 1]
    nused = first_ref[N_EXPERTS]

    def rows(b):
        return pl.ds(pl.multiple_of(b * bm, bm), bm)

    def load(b, s):
        return pltpu.make_async_copy(xs_ref.at[rows(b), :], xbuf.at[s], sem_in.at[s])

    def store(b, s):
        return pltpu.make_async_copy(ybuf.at[s], ys_ref.at[rows(b), :], sem_out.at[s])

    ahead = EXPERT_IN_SLOTS - EXPERT_CHUNK

    @pl.when(e == 0)
    def _():
        for j in range(ahead):
            @pl.when(j < nused)
            def _():
                load(j, j).start()

    @pl.when(hi > lo)
    def _():
        wg_bf[...] = wg_ref[...].astype(BF16)
        wu_bf[...] = wu_ref[...].astype(BF16)
        wd_bf[...] = wd_ref[...].astype(BF16)

    def chunk(b, nblk):
        for d in range(nblk):
            nxt = b + ahead + d

            @pl.when(nxt < nused)
            def _():
                load(nxt, nxt % EXPERT_IN_SLOTS).start()

        halves = []
        for d in range(nblk):
            load(b + d, (b + d) % EXPERT_IN_SLOTS).wait()
            halves.append(_load_rows(xbuf.at[(b + d) % EXPERT_IN_SLOTS], EXPERT_BLOCK))
        xa = jnp.concatenate([h[0] for h in halves], axis=0).astype(BF16)
        xb = jnp.concatenate([h[1] for h in halves], axis=0).astype(BF16)
        g = (jnp.dot(xa, wg_bf[:PACKED, :], preferred_element_type=F32)
             + jnp.dot(xb, wg_bf[PACKED:, :], preferred_element_type=F32))
        u = (jnp.dot(xa, wu_bf[:PACKED, :], preferred_element_type=F32)
             + jnp.dot(xb, wu_bf[PACKED:, :], preferred_element_type=F32))
        h = (_silu(g) * u).astype(BF16)
        y = _pack_bf16_pair(jnp.dot(h, wd_bf[...], preferred_element_type=F32))
        for d in range(nblk):
            s = (b + d) % EXPERT_OUT_SLOTS

            @pl.when(b + d >= EXPERT_OUT_SLOTS)
            def _():
                store(b + d - EXPERT_OUT_SLOTS, s).wait()

            _store_rows(ybuf.at[s], y[d * EXPERT_BLOCK:(d + 1) * EXPERT_BLOCK, :])
            store(b + d, s).start()

    def full_chunk(p, c):
        chunk(lo + p * EXPERT_CHUNK, EXPERT_CHUNK)
        return c

    n_full = (hi - lo) // EXPERT_CHUNK
    lax.fori_loop(0, n_full, full_chunk, 0)
    for rest in range(1, EXPERT_CHUNK):
        @pl.when((hi - lo) % EXPERT_CHUNK == rest)
        def _():
            chunk(hi - rest, rest)

    @pl.when(e == pl.num_programs(0) - 1)
    def _():
        for back in range(EXPERT_OUT_SLOTS, 0, -1):
            @pl.when(nused >= back)
            def _():
                store(nused - back, (nused - back) % EXPERT_OUT_SLOTS).wait()

        ybuf[0] = jnp.zeros((bm, LANES), U32)

        def fill(b, c):
            store(b, 0).start()
            return c

        def drain(b, c):
            store(b, 0).wait()
            return c

        lax.fori_loop(nused, n_blocks, fill, 0)
        lax.fori_loop(nused, n_blocks, drain, 0)


def _expert_call(first_blk, xs, w_e_gate, w_e_up, w_e_down, l):
    n_slots = xs.shape[0]
    bm = EXPERT_BLOCK

    def w_idx(e, first_ref):
        return (l, e, 0, 0)

    grid_spec = pltpu.PrefetchScalarGridSpec(
        num_scalar_prefetch=1,
        grid=(N_EXPERTS,),
        in_specs=[
            pl.BlockSpec(memory_space=pl.ANY),
            pl.BlockSpec((None, None, D_MODEL, EXPERT_FF), w_idx),
            pl.BlockSpec((None, None, D_MODEL, EXPERT_FF), w_idx),
            pl.BlockSpec((None, None, EXPERT_FF, D_MODEL), w_idx),
        ],
        out_specs=pl.BlockSpec(memory_space=pl.ANY),
        scratch_shapes=[pltpu.VMEM((EXPERT_IN_SLOTS, ROW_SUB * bm, LANES), U32),
                        pltpu.VMEM((EXPERT_OUT_SLOTS, ROW_SUB * bm, LANES), U32),
                        pltpu.VMEM((D_MODEL, EXPERT_FF), BF16),
                        pltpu.VMEM((D_MODEL, EXPERT_FF), BF16),
                        pltpu.VMEM((EXPERT_FF, D_MODEL), BF16),
                        pltpu.SemaphoreType.DMA((EXPERT_IN_SLOTS,)),
                        pltpu.SemaphoreType.DMA((EXPERT_OUT_SLOTS,))],
    )
    return pl.pallas_call(
        _expert_kernel,
        out_shape=jax.ShapeDtypeStruct(xs.shape, U32),
        grid_spec=grid_spec,
        compiler_params=_params(("arbitrary",)),
        name="experts",
    )(first_blk, xs, w_e_gate, w_e_up, w_e_down)


def _combine_kernel(n_ctx_tiles, slots_ref, slots_next_ref, gw_ref, x1_ref, h2_ref, mod_ref, wsg_ref, wsu_ref,
                    wsd_ref, ys_ref, oc_ref, ol_ref, buf, sems, acc_scr):
    i = pl.program_id(0)
    n_tok = x1_ref.shape[0]
    half = i % 2
    grp = SUBLANES

    def issue(idx_ref, dst_half, t):
        dst_rows = pl.ds(pl.multiple_of(t * ROW_SUB, ROW_SUB), ROW_SUB)
        for k in range(TOP_K):
            row0 = pl.multiple_of(idx_ref[t * TOP_K + k], ROW_SUB)
            pltpu.make_async_copy(ys_ref.at[pl.ds(row0, ROW_SUB), :], buf.at[dst_half, k, dst_rows, :],
                                  sems.at[dst_half]).start(priority=k % 2)

    @pl.when(i == 0)
    def _():
        def body(t, c):
            issue(slots_ref, 0, t)
            return c
        lax.fori_loop(0, n_tok, body, 0)

    ha, hb = _load_rows(h2_ref, n_tok)
    ha = ha.astype(BF16)
    hb = hb.astype(BF16)
    gate = (jnp.dot(ha, wsg_ref[:PACKED, :], preferred_element_type=F32)
            + jnp.dot(hb, wsg_ref[PACKED:, :], preferred_element_type=F32))
    up = (jnp.dot(ha, wsu_ref[:PACKED, :], preferred_element_type=F32)
          + jnp.dot(hb, wsu_ref[PACKED:, :], preferred_element_type=F32))
    acc_scr[...] = jnp.dot((_silu(gate) * up).astype(BF16), wsd_ref[...], preferred_element_type=F32)

    for k in range(TOP_K):
        pltpu.make_async_copy(ys_ref.at[pl.ds(0, n_tok * ROW_SUB), :], buf.at[half, k], sems.at[half]).wait()
    g2 = mod_ref[:, 5 * D_MODEL:6 * D_MODEL]

    def reduce_group(src_half, r0):
        gw = gw_ref[pl.ds(r0, grp), :]
        acc = [acc_scr[pl.ds(r0, grp), j * LANES:(j + 1) * LANES] for j in range(2 * ROW_SUB)]
        for k in range(TOP_K):
            wk = gw[:, k:k + 1]
            src = buf.at[src_half, k]
            for j in range(ROW_SUB):
                hi, lo = _unpack_bf16_pair(src[pl.ds(r0 * ROW_SUB + j, grp, stride=ROW_SUB), :])
                acc[j] = acc[j] + wk * hi
                acc[ROW_SUB + j] = acc[ROW_SUB + j] + wk * lo
        acc_scr[pl.ds(r0, grp), :] = x1_ref[pl.ds(r0, grp), :] + g2 * jnp.concatenate(acc, axis=1)

    def run(src_half, prefetch):
        def body(g, c):
            r0 = pl.multiple_of(g * grp, grp)
            if prefetch:
                for dt in range(grp):
                    issue(slots_next_ref, 1 - src_half, r0 + dt)
            reduce_group(src_half, r0)
            return c
        lax.fori_loop(0, n_tok // grp, body, 0)

    has_next = i + 1 < pl.num_programs(0)
    for src_half in range(2):
        @pl.when(jnp.logical_and(has_next, half == src_half))
        def _():
            run(src_half, True)

        @pl.when(jnp.logical_and(jnp.logical_not(has_next), half == src_half))
        def _():
            run(src_half, False)

    @pl.when(i < n_ctx_tiles)
    def _():
        oc_ref[...] = acc_scr[...]

    @pl.when(i >= n_ctx_tiles)
    def _():
        ol_ref[...] = acc_scr[...]


def _combine_call(slots_flat, gw, x1, h2, mod3, ys, l, P, n_ctx, lat_t):
    n = x1.shape[0]
    tm = COMBINE_TILE
    n_ctx_tiles = n_ctx // tm
    tiles_per_seq = lat_t // tm

    def mod_idx(i):
        return (jnp.where(i < n_ctx_tiles, 0, 1 + (i - n_ctx_tiles) // tiles_per_seq), 0, 0)

    const = lambda i: (0, 0)
    row = lambda i: (i, 0)
    n_tiles = n // tm
    return pl.pallas_call(
        functools.partial(_combine_kernel, n_ctx_tiles),
        out_shape=(jax.ShapeDtypeStruct((n_ctx, D_MODEL), F32),
                   jax.ShapeDtypeStruct((n - n_ctx, D_MODEL), F32)),
        grid=(n_tiles,),
        in_specs=[
            pl.BlockSpec((tm * TOP_K,), lambda i: (i,), memory_space=pltpu.SMEM),
            pl.BlockSpec((tm * TOP_K,), lambda i: (jnp.minimum(i + 1, n_tiles - 1),), memory_space=pltpu.SMEM),
            pl.BlockSpec((tm, TOP_K), row),
            pl.BlockSpec((tm, D_MODEL), row),
            pl.BlockSpec((ROW_SUB * tm, LANES), row),
            pl.BlockSpec((None, 1, 6 * D_MODEL), mod_idx),
            pl.BlockSpec((D_MODEL, EXPERT_FF), const),
            pl.BlockSpec((D_MODEL, EXPERT_FF), const),
            pl.BlockSpec((EXPERT_FF, D_MODEL), const),
            pl.BlockSpec(memory_space=pl.ANY),
        ],
        out_specs=(pl.BlockSpec((tm, D_MODEL), lambda i: (jnp.minimum(i, n_ctx_tiles - 1), 0)),
                   pl.BlockSpec((tm, D_MODEL), lambda i: (jnp.maximum(i - n_ctx_tiles, 0), 0))),
        scratch_shapes=[pltpu.VMEM((2, TOP_K, ROW_SUB * tm, LANES), U32),
                        pltpu.SemaphoreType.DMA((2,)),
                        pltpu.VMEM((tm, D_MODEL), F32)],
        compiler_params=_params(("arbitrary",)),
        name="combine",
    )(slots_flat, slots_flat, gw, x1, h2, mod3, P["w_s_gate_bf16"][l], P["w_s_up_bf16"][l],
      P["w_s_down_bf16"][l], ys)


def _rope_tables(lat_t):
    rows = lat_t // GRID_W
    row = jnp.repeat(jnp.arange(rows, dtype=F32), GRID_W)
    col = jnp.tile(jnp.arange(GRID_W, dtype=F32), rows)
    n_freq = HEAD_DIM // 4
    inv = jnp.power(ROPE_BASE, -jnp.arange(n_freq, dtype=F32) / n_freq)
    ang = jnp.concatenate([row[:, None] * inv, col[:, None] * inv], axis=-1)
    cos = jnp.repeat(jnp.cos(ang), 2, axis=-1)
    sign = jnp.tile(jnp.array([-1.0, 1.0], F32), HEAD_DIM // 2)
    sin = jnp.repeat(jnp.sin(ang), 2, axis=-1) * sign
    cos = jnp.concatenate([cos, jnp.ones((TOKEN_TILE, HEAD_DIM), F32)], axis=0)
    sin = jnp.concatenate([sin, jnp.zeros((TOKEN_TILE, HEAD_DIM), F32)], axis=0)
    reps = LANES // HEAD_DIM
    return jnp.tile(cos, (1, reps)), jnp.tile(sin, (1, reps))


def _block_diag_gates(w_a, w_x):
    def dense(w):
        eye = jnp.eye(LRU_HEADS, dtype=w.dtype)
        return jnp.einsum("ncd,nm->ncmd", w, eye).reshape(LRU_WIDTH, LRU_WIDTH)
    return jnp.concatenate([dense(w_a[0]), dense(w_x[0]), dense(w_a[1]), dense(w_x[1])], axis=1)


def _prepare(P):
    Q = dict(P)
    Q["w_in_bf16"] = P["w_in"].astype(BF16)
    Q["w_out_bf16"] = P["w_out"].astype(BF16)
    Q["q_norm_g_t"] = jnp.tile(P["q_norm_g"], (1, N_HEADS))
    Q["k_norm_g_t"] = jnp.tile(P["k_norm_g"], (1, N_KV_HEADS))
    head = jnp.arange(ATTN_WIDTH) // HEAD_DIM
    Q["head_blockdiag"] = (head[:, None] == head[None, :]).astype(BF16)
    Q["lru_gate_w"] = jnp.stack([_block_diag_gates(P["lru_w_a"][l], P["lru_w_x"][l])
                                 for l in range(DEPTH)]).astype(BF16)
    Q["lru_gate_b"] = jnp.concatenate([P["lru_b_a"][:, 0], P["lru_b_x"][:, 0],
                                       P["lru_b_a"][:, 1], P["lru_b_x"][:, 1]], axis=-1)
    w_hi = P["w_router"].astype(BF16)
    Q["w_router_hi"] = w_hi
    Q["w_router_lo"] = (P["w_router"] - w_hi.astype(F32)).astype(BF16)
    Q["w_s_gate_bf16"] = P["w_s_gate"].astype(BF16)
    Q["w_s_up_bf16"] = P["w_s_up"].astype(BF16)
    Q["w_s_down_bf16"] = P["w_s_down"].astype(BF16)
    return Q


def _group_tables(counts):
    bm = EXPERT_BLOCK
    blocks = (counts.astype(I32) + bm - 1) // bm
    first_blk = jnp.concatenate([jnp.zeros((1,), I32), jnp.cumsum(blocks).astype(I32)])
    group_start = (first_blk[:N_EXPERTS] * bm).astype(F32)[None, :]
    return group_start, first_blk


def _trunk(x_prompt, x_sample, c, cache_k, cache_v, state_lru, c_ctx, P):
    n_seq_c, ctx_t, _ = x_prompt.shape
    n_seq_l, lat_t, _ = x_sample.shape
    past = cache_k.shape[2]
    n_ctx = n_seq_c * ctx_t
    n_lat = n_seq_l * lat_t
    n = n_ctx + n_lat
    assert n_ctx % lat_t == 0, "latent sequences must start on a whole-sequence boundary of the merged token axis"
    lat_off = n_ctx // lat_t
    P = _prepare(P)

    n_cond = -(-(1 + n_seq_l) // SUBLANES) * SUBLANES
    cond = jnp.concatenate([c_ctx[None, :], c, jnp.zeros((n_cond - 1 - n_seq_l, D_MODEL), F32)], axis=0)
    mods = _mod_call(cond, P["w_mod"], P["b_mod"])
    rope_cos, rope_sin = _rope_tables(lat_t)
    tri = (jnp.arange(ROUTE_TILE)[:, None] > jnp.arange(ROUTE_TILE)[None, :]).astype(BF16)
    n_blocks = n * TOP_K // EXPERT_BLOCK + N_EXPERTS
    n_slots = n_blocks * EXPERT_BLOCK

    x_c = x_prompt.reshape(n_ctx, D_MODEL)
    x_l = x_sample.reshape(n_lat, D_MODEL)
    ks, vs, ss = [], [], []
    for l in range(DEPTH):
        mod3 = mods[l].reshape(n_cond, 1, 6 * D_MODEL)
        q, kr, vb, kc, vc, u, gate, cv = _premix_call(x_c, x_l, mod3, l, P, rope_cos, rope_sin, n_ctx, lat_t)
        ks.append(kc[:n_ctx].reshape(n_seq_c, ctx_t, N_KV_HEADS, HEAD_DIM))
        vs.append(vc[:n_ctx].reshape(n_seq_c, ctx_t, N_KV_HEADS, HEAD_DIM))

        as_ctx = lambda a: a.reshape(n // ctx_t, ctx_t, a.shape[-1])
        as_lat = lambda a: a.reshape(n // lat_t, lat_t, a.shape[-1])

        attn_c = _attn_call(as_ctx(q), as_ctx(kr), as_ctx(vb), min(ctx_t, 256), n_seq_c, 0, 0)
        k_all = jnp.concatenate([kr[n_ctx:].reshape(n_seq_l, lat_t, KV_WIDTH),
                                 cache_k[:, l].reshape(n_seq_l, past, KV_WIDTH).astype(BF16)], axis=1)
        cv_ones = jnp.concatenate([cache_v[:, l], jnp.ones_like(cache_v[:, l])], axis=-1)
        v_all = jnp.concatenate([vb[n_ctx:].reshape(n_seq_l, lat_t, 2 * KV_WIDTH),
                                 cv_ones.reshape(n_seq_l, past, 2 * KV_WIDTH).astype(BF16)], axis=1)
        attn_l = _attn_call(as_lat(q), k_all, v_all, 256, n_seq_l, lat_off, 0)

        lru_c, fin_c = _lru_call(as_ctx(u), as_ctx(gate), jnp.zeros((n_seq_c, 2, LRU_WIDTH), F32), l, P,
                                 n_seq_c, 0)
        lru_l, _ = _lru_call(as_lat(u), as_lat(gate), state_lru[:, l], l, P, n_seq_l, lat_off)
        ss.append(fin_c)
        conv_c = _convmod_call(as_ctx(cv), l, P, n_seq_c, 0)
        conv_l = _convmod_call(as_lat(cv), l, P, n_seq_l, lat_off)

        x1, h2, logits = _postmix_call(
            attn_c.reshape(n_ctx, ATTN_WIDTH), attn_l.reshape(n_lat, ATTN_WIDTH),
            lru_c.reshape(n_ctx, LRU_WIDTH), lru_l.reshape(n_lat, LRU_WIDTH),
            conv_c.reshape(n_ctx, CONV_WIDTH), conv_l.reshape(n_lat, CONV_WIDTH),
            x_c, x_l, mod3, l, P, n_ctx, lat_t)

        ids, gw, rank, counts = _route_call(logits, P["b_router"][l], tri)
        group_start, first_blk = _group_tables(counts[0])
        slots = _slots_call(ids, rank, group_start).reshape(-1)
        xs = _dispatch_call(first_blk, slots, h2, n_slots)
        ys = _expert_call(first_blk, xs, P["w_e_gate"], P["w_e_up"], P["w_e_down"], l)
        x_c, x_l = _combine_call(slots, gw, x1, h2, mod3, ys, l, P, n_ctx, lat_t)

    y_prompt = x_c.reshape(n_seq_c, ctx_t, D_MODEL)
    y_sample = x_l.reshape(n_seq_l, lat_t, D_MODEL)
    return (y_prompt, y_sample, jnp.stack(ks, axis=1), jnp.stack(vs, axis=1), jnp.stack(ss, axis=1))


def kernel(x_prompt, x_sample, c, cache_k, cache_v, state_lru, c_ctx, w_mod, b_mod, norm1_g, w_in, q_norm_g, k_norm_g, lru_conv_w, lru_conv_b, lru_w_a, lru_b_a, lru_w_x, lru_b_x, lru_lambda, cm_dw_w, cm_dw_b, cm_ln_g, cm_ln_b, out_norm_g, w_out, norm2_g, w_router, b_router, w_e_gate, w_e_up, w_e_down, w_s_gate, w_s_up, w_s_down):
    P = {"w_mod": w_mod, "b_mod": b_mod, "norm1_g": norm1_g, "w_in": w_in, "q_norm_g": q_norm_g,
         "k_norm_g": k_norm_g, "lru_conv_w": lru_conv_w, "lru_conv_b": lru_conv_b, "lru_w_a": lru_w_a,
         "lru_b_a": lru_b_a, "lru_w_x": lru_w_x, "lru_b_x": lru_b_x, "lru_lambda": lru_lambda,
         "cm_dw_w": cm_dw_w, "cm_dw_b": cm_dw_b, "cm_ln_g": cm_ln_g, "cm_ln_b": cm_ln_b,
         "out_norm_g": out_norm_g, "w_out": w_out, "norm2_g": norm2_g, "w_router": w_router,
         "b_router": b_router, "w_e_gate": w_e_gate, "w_e_up": w_e_up, "w_e_down": w_e_down,
         "w_s_gate": w_s_gate, "w_s_up": w_s_up, "w_s_down": w_s_down}
    return _trunk(x_prompt, x_sample, c, cache_k, cache_v, state_lru, c_ctx, P)
```

```python
import functools
import math

import jax
import jax.numpy as jnp
from jax import lax
from jax.experimental import pallas as pl
from jax.experimental.pallas import tpu as pltpu

F32 = jnp.float32
BF16 = jnp.bfloat16
I32 = jnp.int32

D_MODEL = 1024
DEPTH = 2
GRID_W = 64
ATTN_WIDTH = 512
LRU_WIDTH = 256
CONV_WIDTH = 256
HEAD_DIM = 64
N_HEADS = 8
N_KV_HEADS = 2
KV_GROUP = N_HEADS // N_KV_HEADS
KV_WIDTH = N_KV_HEADS * HEAD_DIM
ROPE_BASE = 10000.0
LRU_HEADS = 4
LRU_HEAD_DIM = LRU_WIDTH // LRU_HEADS
LRU_CONV = 4
RG_C = 8.0
CM_KERNEL = 31
N_EXPERTS = 256
TOP_K = 8
EXPERT_FF = 256
ROUTED_SCALE = 2.5
EPS = 1e-6
IN_WIDTH = ATTN_WIDTH + 2 * KV_WIDTH + 2 * LRU_WIDTH + 2 * CONV_WIDTH

SUBLANES = 8
LANES = 128
VMEM_LIMIT = 56 * 1024 * 1024

TOKEN_TILE = 512
ROUTE_TILE = 512
SCAN_CHUNK = 256
CONV_CHUNK = 128
CONV_HALO = 16
LRU_HALO = 8
EXPERT_BLOCK = 256
EXPERT_CHUNK = 2
EXPERT_IN_SLOTS = 6
EXPERT_OUT_SLOTS = 4
COMBINE_TILE = 256
DISPATCH_TILE = 512


def _params(sem):
    return pltpu.CompilerParams(dimension_semantics=sem, vmem_limit_bytes=VMEM_LIMIT)


def _sigmoid(x):
    return 1.0 / (1.0 + jnp.exp(-x))


def _silu(x):
    return x * _sigmoid(x)


def _bdot(a, b):
    return jnp.dot(a.astype(BF16), b.astype(BF16), preferred_element_type=F32)


def _split_dot(a, b_bf16):
    hi = a.astype(BF16)
    lo = (a - hi.astype(F32)).astype(BF16)
    return (jnp.dot(hi, b_bf16, preferred_element_type=F32)
            + jnp.dot(lo, b_bf16, preferred_element_type=F32))


U32 = jnp.uint32
PACKED = D_MODEL // 2


def _pack_bf16_pair(x):
    hi = lax.bitcast_convert_type(x[:, :PACKED].astype(BF16).astype(F32), U32)
    lo = lax.bitcast_convert_type(x[:, PACKED:].astype(BF16).astype(F32), U32)
    return (hi & jnp.uint32(0xFFFF0000)) | (lo >> 16)


def _unpack_bf16_pair(w):
    hi = lax.bitcast_convert_type(w & jnp.uint32(0xFFFF0000), F32)
    lo = lax.bitcast_convert_type(w << 16, F32)
    return hi, lo


ROW_SUB = PACKED // LANES


def _store_rows(ref, w):
    r = w.shape[0]
    for j in range(ROW_SUB):
        ref[pl.ds(j, r, stride=ROW_SUB), :] = w[:, j * LANES:(j + 1) * LANES]


def _load_rows(ref, r):
    his, los = [], []
    for j in range(ROW_SUB):
        hi, lo = _unpack_bf16_pair(ref[pl.ds(j, r, stride=ROW_SUB), :])
        his.append(hi)
        los.append(lo)
    return jnp.concatenate(his, axis=1), jnp.concatenate(los, axis=1)


def _rms(x, g):
    return x * lax.rsqrt(jnp.mean(x * x, axis=-1, keepdims=True) + EPS) * g


MOD_COLS = 1536


def _mod_kernel(c_ref, w_ref, b_ref, o_ref):
    a = _silu(c_ref[...])
    o_ref[...] = jnp.dot(a, w_ref[...], preferred_element_type=F32,
                         precision=lax.Precision.HIGHEST) + b_ref[...]


def _mod_call(cond, w_mod, b_mod):
    n_cond = cond.shape[0]
    width = 6 * D_MODEL
    return pl.pallas_call(
        _mod_kernel,
        out_shape=jax.ShapeDtypeStruct((DEPTH, n_cond, width), F32),
        grid=(DEPTH, width // MOD_COLS),
        in_specs=[
            pl.BlockSpec((n_cond, D_MODEL), lambda l, j: (0, 0)),
            pl.BlockSpec((None, D_MODEL, MOD_COLS), lambda l, j: (l, 0, j)),
            pl.BlockSpec((None, 1, MOD_COLS), lambda l, j: (l, 0, j)),
        ],
        out_specs=pl.BlockSpec((None, n_cond, MOD_COLS), lambda l, j: (l, 0, j)),
        compiler_params=_params(("arbitrary", "arbitrary")),
        name="mod",
    )(cond, w_mod, b_mod.reshape(DEPTH, 1, width))


def _swap_pairs(x):
    n = x.shape[-1]
    lane = lax.broadcasted_iota(I32, x.shape, 1)
    nxt = pltpu.roll(x, n - 1, 1)
    prv = pltpu.roll(x, 1, 1)
    return jnp.where(lane % 2 == 0, nxt, prv)


def _premix_kernel(n_ctx_tiles, xc_ref, xl_ref, mod_ref, n1g_ref, w_ref, qg_ref, kg_ref, bd_ref, cos_ref, sin_ref,
                   q_ref, kr_ref, vb_ref, kc_ref, vc_ref, u_ref, gate_ref, cv_ref):
    x = jnp.where(pl.program_id(0) < n_ctx_tiles, xc_ref[...], xl_ref[...])
    sh1 = mod_ref[:, 0:D_MODEL]
    sc1 = mod_ref[:, D_MODEL:2 * D_MODEL]
    h = _rms(x, n1g_ref[...]) * (1.0 + sc1) + sh1
    z = jnp.dot(h.astype(BF16), w_ref[...], preferred_element_type=F32)
    o = 0
    q = z[:, o:o + ATTN_WIDTH]; o += ATTN_WIDTH
    k = z[:, o:o + KV_WIDTH]; o += KV_WIDTH
    v = z[:, o:o + KV_WIDTH]; o += KV_WIDTH
    u_ref[...] = z[:, o:o + LRU_WIDTH]; o += LRU_WIDTH
    gate_ref[...] = z[:, o:o + LRU_WIDTH]; o += LRU_WIDTH
    cv_ref[...] = z[:, o:o + 2 * CONV_WIDTH]

    bd = bd_ref[...]
    inv_hd = 1.0 / HEAD_DIM
    q_ms = _split_dot(q * q, bd) * inv_hd
    k_ms = _split_dot(k * k, bd[:KV_WIDTH, :KV_WIDTH]) * inv_hd
    qn = q * lax.rsqrt(q_ms + EPS) * qg_ref[...]
    kn = k * lax.rsqrt(k_ms + EPS) * kg_ref[...]
    kc_ref[...] = kn
    vc_ref[...] = v
    ones = jnp.ones((v.shape[0], HEAD_DIM), F32)
    vb_ref[...] = jnp.concatenate([v[:, :HEAD_DIM], ones, v[:, HEAD_DIM:], ones], axis=1).astype(BF16)

    cos = cos_ref[...]
    sin = sin_ref[...]
    reps = ATTN_WIDTH // cos.shape[-1]
    cos_q = jnp.concatenate([cos] * reps, axis=1)
    sin_q = jnp.concatenate([sin] * reps, axis=1)
    qr = qn * cos_q + _swap_pairs(qn) * sin_q
    kr = kn * cos + _swap_pairs(kn) * sin
    q_ref[...] = (qr * (HEAD_DIM ** -0.5 * math.log2(math.e))).astype(BF16)
    kr_ref[...] = kr.astype(BF16)


def _premix_call(x_c, x_l, mod3, l, P, rope_cos, rope_sin, n_ctx, lat_t):
    n = x_c.shape[0] + x_l.shape[0]
    tm = TOKEN_TILE
    n_ctx_tiles = n_ctx // tm
    tiles_per_seq = lat_t // tm

    def mod_idx(i):
        return (jnp.where(i < n_ctx_tiles, 0, 1 + (i - n_ctx_tiles) // tiles_per_seq), 0, 0)

    def rope_idx(i):
        return (jnp.where(i < n_ctx_tiles, tiles_per_seq, (i - n_ctx_tiles) % tiles_per_seq), 0)

    const = lambda i: (0, 0)
    row = lambda i: (i, 0)
    row_c = lambda i: (jnp.minimum(i, n_ctx_tiles - 1), 0)
    row_l = lambda i: (jnp.maximum(i - n_ctx_tiles, 0), 0)
    outs = pl.pallas_call(
        functools.partial(_premix_kernel, n_ctx_tiles),
        out_shape=(
            jax.ShapeDtypeStruct((n, ATTN_WIDTH), BF16),
            jax.ShapeDtypeStruct((n, KV_WIDTH), BF16),
            jax.ShapeDtypeStruct((n, 2 * KV_WIDTH), BF16),
            jax.ShapeDtypeStruct((n, KV_WIDTH), F32),
            jax.ShapeDtypeStruct((n, KV_WIDTH), F32),
            jax.ShapeDtypeStruct((n, LRU_WIDTH), F32),
            jax.ShapeDtypeStruct((n, LRU_WIDTH), F32),
            jax.ShapeDtypeStruct((n, 2 * CONV_WIDTH), F32),
        ),
        grid=(n // tm,),
        in_specs=[
            pl.BlockSpec((tm, D_MODEL), row_c),
            pl.BlockSpec((tm, D_MODEL), row_l),
            pl.BlockSpec((None, 1, 6 * D_MODEL), mod_idx),
            pl.BlockSpec((1, D_MODEL), const),
            pl.BlockSpec((D_MODEL, IN_WIDTH), const),
            pl.BlockSpec((1, ATTN_WIDTH), const),
            pl.BlockSpec((1, KV_WIDTH), const),
            pl.BlockSpec((ATTN_WIDTH, ATTN_WIDTH), const),
            pl.BlockSpec((tm, LANES), rope_idx),
            pl.BlockSpec((tm, LANES), rope_idx),
        ],
        out_specs=(
            pl.BlockSpec((tm, ATTN_WIDTH), row),
            pl.BlockSpec((tm, KV_WIDTH), row),
            pl.BlockSpec((tm, 2 * KV_WIDTH), row),
            pl.BlockSpec((tm, KV_WIDTH), row),
            pl.BlockSpec((tm, KV_WIDTH), row),
            pl.BlockSpec((tm, LRU_WIDTH), row),
            pl.BlockSpec((tm, LRU_WIDTH), row),
            pl.BlockSpec((tm, 2 * CONV_WIDTH), row),
        ),
        compiler_params=_params(("parallel",)),
        name="premix",
    )(x_c, x_l, mod3, P["norm1_g"][l][None, :], P["w_in_bf16"][l], P["q_norm_g_t"][l][None, :],
      P["k_norm_g_t"][l][None, :], P["head_blockdiag"], rope_cos, rope_sin)
    return outs


def _attn_kernel(q_ref, k_ref, v_ref, o_ref):
    outs = []
    for kh in range(N_KV_HEADS):
        k = k_ref[:, kh * HEAD_DIM:(kh + 1) * HEAD_DIM]
        v = v_ref[:, kh * 2 * HEAD_DIM:(kh + 1) * 2 * HEAD_DIM]
        for g in range(KV_GROUP):
            hd = kh * KV_GROUP + g
            q = q_ref[:, hd * HEAD_DIM:(hd + 1) * HEAD_DIM]
            s = lax.dot_general(q, k, (((1,), (1,)), ((), ())), preferred_element_type=F32)
            m = jnp.max(s, axis=-1, keepdims=True)
            p = jnp.exp2(s - m).astype(BF16)
            o = jnp.dot(p, v, preferred_element_type=F32)
            outs.append(o[:, :HEAD_DIM] / o[:, HEAD_DIM:HEAD_DIM + 1])
    o_ref[...] = jnp.concatenate(outs, axis=1)


def _attn_call(q3, k3, v3, tq, b, q_off, kv_off):
    s = q3.shape[1]
    t = k3.shape[1]
    return pl.pallas_call(
        _attn_kernel,
        out_shape=jax.ShapeDtypeStruct((b, s, ATTN_WIDTH), F32),
        grid=(b, s // tq),
        in_specs=[
            pl.BlockSpec((None, tq, ATTN_WIDTH), lambda i, j: (i + q_off, j, 0)),
            pl.BlockSpec((None, t, KV_WIDTH), lambda i, j: (i + kv_off, 0, 0)),
            pl.BlockSpec((None, t, 2 * KV_WIDTH), lambda i, j: (i + kv_off, 0, 0)),
        ],
        out_specs=pl.BlockSpec((None, tq, ATTN_WIDTH), lambda i, j: (i, j, 0)),
        compiler_params=_params(("parallel", "parallel")),
        name="attention",
    )(q3, k3, v3)


def _tap_sum(win, taps, bias, first, n_taps, rows):
    n = win.shape[0]
    acc = bias
    for phase in range(SUBLANES):
        starts = [first + j for j in range(n_taps) if (first + j) % SUBLANES == phase]
        if not starts:
            continue
        rolled = win if phase == 0 else pltpu.roll(win, n - phase, 0)
        for s0 in starts:
            base = s0 - phase
            acc = acc + taps[s0 - first:s0 - first + 1, :] * rolled[base:base + rows, :]
    return acc


def _chunk_scan(a, b, h, reverse):
    n, w = a.shape
    groups = n // SUBLANES
    a = a.reshape(groups, SUBLANES, w)
    b = b.reshape(groups, SUBLANES, w)
    sub = lax.broadcasted_iota(I32, a.shape, 1)
    d = 1
    while d < SUBLANES:
        if reverse:
            a_s = pltpu.roll(a, SUBLANES - d, 1)
            b_s = pltpu.roll(b, SUBLANES - d, 1)
            ok = sub < SUBLANES - d
        else:
            a_s = pltpu.roll(a, d, 1)
            b_s = pltpu.roll(b, d, 1)
            ok = sub >= d
        b = jnp.where(ok, a * b_s + b, b)
        a = jnp.where(ok, a * a_s, a)
        d *= 2
    outs = [None] * groups
    for g in (range(groups - 1, -1, -1) if reverse else range(groups)):
        hg = a[g] * h + b[g]
        outs[g] = hg
        h = hg[0:1, :] if reverse else hg[SUBLANES - 1:SUBLANES, :]
    return jnp.concatenate(outs, axis=0), h


def _gelu_tanh(x):
    return 0.5 * x * (1.0 + jnp.tanh(math.sqrt(2.0 / math.pi) * (x + 0.044715 * (x * x * x))))


def _lru_kernel(u_ref, gate_ref, cw_ref, cb_ref, wg_ref, bg_ref, lam_ref, h0_ref,
                y_ref, fin_ref, upad, xc_scr, fwd_scr):
    t_len = u_ref.shape[0]
    ch = SCAN_CHUNK
    n_chunks = t_len // ch
    w = LRU_WIDTH
    zeros_halo = jnp.zeros((LRU_HALO, w), F32)
    upad[0:LRU_HALO, :] = zeros_halo
    upad[t_len + LRU_HALO:t_len + 2 * LRU_HALO, :] = zeros_halo

    def fill(c, carry):
        r0 = pl.multiple_of(c * ch, ch)
        upad[pl.ds(r0 + LRU_HALO, ch), :] = u_ref[pl.ds(r0, ch), :]
        return carry

    lax.fori_loop(0, n_chunks, fill, 0)

    lam = lam_ref[...]
    nlam = -lam
    softplus = jnp.maximum(nlam, 0.0) + jnp.log(1.0 + jnp.exp(-jnp.abs(nlam)))
    decay = -RG_C * softplus
    cw = cw_ref[...]
    cb = cb_ref[...]

    def gates(xc, d):
        z = jnp.dot(xc.astype(BF16), wg_ref[:, 2 * d * w:2 * (d + 1) * w],
                    preferred_element_type=F32) + bg_ref[:, 2 * d * w:2 * (d + 1) * w]
        r = _sigmoid(z[:, :w])
        i = _sigmoid(z[:, w:])
        log_a = decay[d:d + 1, :] * r
        a = jnp.exp(log_a)
        b = jnp.sqrt(-jnp.tanh(log_a) * (a * a + 1.0)) * (i * xc)
        return a, b

    def fwd(c, h):
        r0 = pl.multiple_of(c * ch, ch)
        win = upad[pl.ds(r0, ch + 2 * LRU_HALO), :]
        xc = _tap_sum(win, cw, cb, LRU_HALO - (LRU_CONV - 1) // 2, LRU_CONV, ch)
        xc_scr[pl.ds(r0, ch), :] = xc
        a, b = gates(xc, 0)
        hs, h_out = _chunk_scan(a, b, h, reverse=False)
        fwd_scr[pl.ds(r0, ch), :] = hs
        return h_out

    h_f = lax.fori_loop(0, n_chunks, fwd, h0_ref[0:1, :])

    def bwd(ci, h):
        c = n_chunks - 1 - ci
        r0 = pl.multiple_of(c * ch, ch)
        xc = xc_scr[pl.ds(r0, ch), :]
        a, b = gates(xc, 1)
        hs, h_out = _chunk_scan(a, b, h, reverse=True)
        y_ref[pl.ds(r0, ch), :] = (fwd_scr[pl.ds(r0, ch), :] + hs) * _gelu_tanh(gate_ref[pl.ds(r0, ch), :])
        return h_out

    h_b = lax.fori_loop(0, n_chunks, bwd, h0_ref[1:2, :])
    fin_ref[0:1, :] = h_f
    fin_ref[1:2, :] = h_b


def _lru_call(u3, gate3, h0, l, P, b, seq_off):
    _, t, w = u3.shape
    const = lambda i: (0, 0)
    seq = lambda i: (i, 0, 0)
    seq_in = lambda i: (i + seq_off, 0, 0)
    return pl.pallas_call(
        _lru_kernel,
        out_shape=(jax.ShapeDtypeStruct((b, t, w), F32),
                   jax.ShapeDtypeStruct((b, 2, w), F32)),
        grid=(b,),
        in_specs=[
            pl.BlockSpec((None, t, w), seq_in),
            pl.BlockSpec((None, t, w), seq_in),
            pl.BlockSpec((LRU_CONV, w), const),
            pl.BlockSpec((1, w), const),
            pl.BlockSpec((w, 4 * w), const),
            pl.BlockSpec((1, 4 * w), const),
            pl.BlockSpec((2, w), const),
            pl.BlockSpec((None, 2, w), seq),
        ],
        out_specs=(pl.BlockSpec((None, t, w), seq),
                   pl.BlockSpec((None, 2, w), seq)),
        scratch_shapes=[pltpu.VMEM((t + 2 * LRU_HALO, w), F32),
                        pltpu.VMEM((t, w), F32),
                        pltpu.VMEM((t, w), F32)],
        compiler_params=_params(("parallel",)),
        name="rglru",
    )(u3, gate3, P["lru_conv_w"][l], P["lru_conv_b"][l][None, :], P["lru_gate_w"][l],
      P["lru_gate_b"][l][None, :], P["lru_lambda"][l], h0)


def _convmod_kernel(cv_ref, w_ref, b_ref, g_ref, beta_ref, o_ref, hpad):
    t_len = cv_ref.shape[0]
    ch = CONV_CHUNK
    n_chunks = t_len // ch
    w = CONV_WIDTH
    zeros_halo = jnp.zeros((CONV_HALO, w), F32)
    hpad[0:CONV_HALO, :] = zeros_halo
    hpad[t_len + CONV_HALO:t_len + 2 * CONV_HALO, :] = zeros_halo

    def glu(c, carry):
        r0 = pl.multiple_of(c * ch, ch)
        blk = cv_ref[pl.ds(r0, ch), :]
        hpad[pl.ds(r0 + CONV_HALO, ch), :] = blk[:, :w] * _sigmoid(blk[:, w:])
        return carry

    lax.fori_loop(0, n_chunks, glu, 0)

    taps = w_ref[...]
    bias = b_ref[...]
    gamma = g_ref[...]
    beta = beta_ref[...]

    def conv(c, carry):
        r0 = pl.multiple_of(c * ch, ch)
        win = hpad[pl.ds(r0, ch + 2 * CONV_HALO), :]
        acc = _tap_sum(win, taps, bias, CONV_HALO - CM_KERNEL // 2, CM_KERNEL, ch)
        mu = jnp.mean(acc, axis=-1, keepdims=True)
        cen = acc - mu
        var = jnp.mean(cen * cen, axis=-1, keepdims=True)
        y = cen * lax.rsqrt(var + EPS) * gamma + beta
        o_ref[pl.ds(r0, ch), :] = _silu(y)
        return carry

    lax.fori_loop(0, n_chunks, conv, 0)


def _convmod_call(cv3, l, P, b, seq_off):
    t = cv3.shape[1]
    w = CONV_WIDTH
    const = lambda i: (0, 0)
    seq = lambda i: (i, 0, 0)
    return pl.pallas_call(
        _convmod_kernel,
        out_shape=jax.ShapeDtypeStruct((b, t, w), F32),
        grid=(b,),
        in_specs=[
            pl.BlockSpec((None, t, 2 * w), lambda i: (i + seq_off, 0, 0)),
            pl.BlockSpec((CM_KERNEL, w), const),
            pl.BlockSpec((1, w), const),
            pl.BlockSpec((1, w), const),
            pl.BlockSpec((1, w), const),
        ],
        out_specs=pl.BlockSpec((None, t, w), seq),
        scratch_shapes=[pltpu.VMEM((t + 2 * CONV_HALO, w), F32)],
        compiler_params=_params(("parallel",)),
        name="convmod",
    )(cv3, P["cm_dw_w"][l], P["cm_dw_b"][l][None, :], P["cm_ln_g"][l][None, :], P["cm_ln_b"][l][None, :])


def _postmix_kernel(n_ctx_tiles, attn_c_ref, attn_l_ref, lru_c_ref, lru_l_ref, conv_c_ref, conv_l_ref,
                    xc_ref, xl_ref, mod_ref, og_ref, wo_ref, n2g_ref, wr_hi_ref, wr_lo_ref, br_ref, tri_ref,
                    x1_ref, h2_ref, ids_ref, gw_ref, rank_ref, cnt_ref, carry):
    is_ctx = pl.program_id(0) < n_ctx_tiles
    og = og_ref[...]
    a0, a1, a2 = ATTN_WIDTH, ATTN_WIDTH + LRU_WIDTH, D_MODEL
    y = jnp.concatenate([_rms(jnp.where(is_ctx, attn_c_ref[...], attn_l_ref[...]), og[:, :a0]),
                         _rms(jnp.where(is_ctx, lru_c_ref[...], lru_l_ref[...]), og[:, a0:a1]),
                         _rms(jnp.where(is_ctx, conv_c_ref[...], conv_l_ref[...]), og[:, a1:a2])], axis=1)
    y = jnp.dot(y.astype(BF16), wo_ref[...], preferred_element_type=F32)
    g1 = mod_ref[:, 2 * D_MODEL:3 * D_MODEL]
    sh2 = mod_ref[:, 3 * D_MODEL:4 * D_MODEL]
    sc2 = mod_ref[:, 4 * D_MODEL:5 * D_MODEL]
    x1 = jnp.where(is_ctx, xc_ref[...], xl_ref[...]) + g1 * y
    x1_ref[...] = x1
    h2 = _rms(x1, n2g_ref[...]) * (1.0 + sc2) + sh2
    _store_rows(h2_ref, _pack_bf16_pair(h2))
    hi = h2.astype(BF16)
    lo = (h2 - hi.astype(F32)).astype(BF16)
    w_hi = wr_hi_ref[...]
    logits = (jnp.dot(hi, w_hi, preferred_element_type=F32)
              + jnp.dot(lo, w_hi, preferred_element_type=F32)
              + jnp.dot(hi, wr_lo_ref[...], preferred_element_type=F32))
    _route_tile(logits, br_ref, tri_ref, ids_ref, gw_ref, rank_ref, cnt_ref, carry)


def _postmix_call(attn_c, attn_l, lru_c, lru_l, conv_c, conv_l, x_c, x_l, mod3, tri, l, P, n_ctx, lat_t):
    n = x_c.shape[0] + x_l.shape[0]
    tm = TOKEN_TILE
    n_ctx_tiles = n_ctx // tm
    tiles_per_seq = lat_t // tm

    def mod_idx(i):
        return (jnp.where(i < n_ctx_tiles, 0, 1 + (i - n_ctx_tiles) // tiles_per_seq), 0, 0)

    const = lambda i: (0, 0)
    row = lambda i: (i, 0)
    row_c = lambda i: (jnp.minimum(i, n_ctx_tiles - 1), 0)
    row_l = lambda i: (jnp.maximum(i - n_ctx_tiles, 0), 0)
    return pl.pallas_call(
        functools.partial(_postmix_kernel, n_ctx_tiles),
        out_shape=(jax.ShapeDtypeStruct((n, D_MODEL), F32),
                   jax.ShapeDtypeStruct((ROW_SUB * n, LANES), U32),
                   jax.ShapeDtypeStruct((n, TOP_K), I32),
                   jax.ShapeDtypeStruct((n, TOP_K), F32),
                   jax.ShapeDtypeStruct((n, TOP_K), I32),
                   jax.ShapeDtypeStruct((SUBLANES, N_EXPERTS), F32)),
        grid=(n // tm,),
        in_specs=[
            pl.BlockSpec((tm, ATTN_WIDTH), row_c),
            pl.BlockSpec((tm, ATTN_WIDTH), row_l),
            pl.BlockSpec((tm, LRU_WIDTH), row_c),
            pl.BlockSpec((tm, LRU_WIDTH), row_l),
            pl.BlockSpec((tm, CONV_WIDTH), row_c),
            pl.BlockSpec((tm, CONV_WIDTH), row_l),
            pl.BlockSpec((tm, D_MODEL), row_c),
            pl.BlockSpec((tm, D_MODEL), row_l),
            pl.BlockSpec((None, 1, 6 * D_MODEL), mod_idx),
            pl.BlockSpec((1, D_MODEL), const),
            pl.BlockSpec((D_MODEL, D_MODEL), const),
            pl.BlockSpec((1, D_MODEL), const),
            pl.BlockSpec((D_MODEL, N_EXPERTS), const),
            pl.BlockSpec((D_MODEL, N_EXPERTS), const),
            pl.BlockSpec((1, N_EXPERTS), const),
            pl.BlockSpec((tm, tm), const),
        ],
        out_specs=(pl.BlockSpec((tm, D_MODEL), row),
                   pl.BlockSpec((ROW_SUB * tm, LANES), row),
                   pl.BlockSpec((tm, TOP_K), row),
                   pl.BlockSpec((tm, TOP_K), row),
                   pl.BlockSpec((tm, TOP_K), row),
                   pl.BlockSpec((SUBLANES, N_EXPERTS), const)),
        scratch_shapes=[pltpu.VMEM((SUBLANES, N_EXPERTS), F32)],
        compiler_params=_params(("arbitrary",)),
        name="postmix",
    )(attn_c, attn_l, lru_c, lru_l, conv_c, conv_l, x_c, x_l, mod3, P["out_norm_g"][l][None, :], P["w_out_bf16"][l],
      P["norm2_g"][l][None, :], P["w_router_hi"][l], P["w_router_lo"][l], P["b_router"][l][None, :], tri)


def _route_tile(logits, br_ref, tri_ref, ids_ref, gw_ref, rank_ref, cnt_ref, carry):
    i = pl.program_id(0)

    @pl.when(i == 0)
    def _():
        carry[...] = jnp.zeros_like(carry)

    scores = _sigmoid(logits)
    sel = scores + br_ref[...]
    tm = scores.shape[0]
    lane = lax.broadcasted_iota(I32, (tm, N_EXPERTS), 1).astype(F32)
    slot_lane = lax.broadcasted_iota(I32, (tm, LANES), 1)
    ids_acc = jnp.zeros((tm, LANES), F32)
    gw_acc = jnp.zeros((tm, LANES), F32)
    hot = jnp.zeros((tm, N_EXPERTS), F32)
    picked = []
    for k in range(TOP_K):
        m = jnp.max(sel, axis=-1, keepdims=True)
        idx = jnp.min(jnp.where(sel == m, lane, float(N_EXPERTS)), axis=-1, keepdims=True)
        one = lane == idx
        g = jnp.sum(jnp.where(one, scores, 0.0), axis=-1, keepdims=True)
        sel = jnp.where(one, -jnp.inf, sel)
        hot = jnp.where(one, 1.0, hot)
        ids_acc = jnp.where(slot_lane == k, idx, ids_acc)
        gw_acc = jnp.where(slot_lane == k, g, gw_acc)
        picked.append(idx)
    denom = jnp.sum(gw_acc, axis=-1, keepdims=True)
    gw_acc = ROUTED_SCALE * gw_acc / denom

    before = jnp.dot(tri_ref[...], hot.astype(BF16), preferred_element_type=F32) + carry[0:1, :]
    rank_acc = jnp.zeros((tm, LANES), F32)
    for k in range(TOP_K):
        r = jnp.sum(jnp.where(lane == picked[k], before, 0.0), axis=-1, keepdims=True)
        rank_acc = jnp.where(slot_lane == k, r, rank_acc)
    carry[0:1, :] = carry[0:1, :] + jnp.sum(hot, axis=0, keepdims=True)
    cnt_ref[...] = carry[...]
    ids_ref[...] = ids_acc[:, :TOP_K].astype(I32)
    gw_ref[...] = gw_acc[:, :TOP_K]
    rank_ref[...] = rank_acc[:, :TOP_K].astype(I32)


def _slots_kernel(ids_ref, rank_ref, start_ref, slots_ref):
    ids = ids_ref[...].astype(F32)
    tm = ids.shape[0]
    lane = lax.broadcasted_iota(I32, (tm, N_EXPERTS), 1).astype(F32)
    slot_lane = lax.broadcasted_iota(I32, (tm, TOP_K), 1)
    start = start_ref[...]
    acc = jnp.zeros((tm, TOP_K), F32)
    for k in range(TOP_K):
        s = jnp.sum(jnp.where(lane == ids[:, k:k + 1], start, 0.0), axis=-1, keepdims=True)
        acc = jnp.where(slot_lane == k, s, acc)
    slots_ref[...] = (acc.astype(I32) + rank_ref[...]) * ROW_SUB


def _slots_call(ids, rank, group_start):
    n = ids.shape[0]
    tm = ROUTE_TILE
    row = lambda i: (i, 0)
    return pl.pallas_call(
        _slots_kernel,
        out_shape=jax.ShapeDtypeStruct((n, TOP_K), I32),
        grid=(n // tm,),
        in_specs=[pl.BlockSpec((tm, TOP_K), row), pl.BlockSpec((tm, TOP_K), row),
                  pl.BlockSpec((1, N_EXPERTS), lambda i: (0, 0))],
        out_specs=pl.BlockSpec((tm, TOP_K), row),
        compiler_params=_params(("parallel",)),
        name="slots",
    )(ids, rank, group_start)


def _wait_rows(src_ref, dst_ref, sem, n_waits):
    for _ in range(n_waits):
        pltpu.make_async_copy(src_ref, dst_ref, sem).wait()


def _dispatch_kernel(first_ref, slots_ref, h_ref, xs_ref, zeros, sem_rows, sem_zero):
    n_tok = h_ref.shape[0] // ROW_SUB
    bm = EXPERT_BLOCK * ROW_SUB
    n_blocks = xs_ref.shape[0] // bm
    nused = first_ref[N_EXPERTS]

    def zero_copy(b):
        return pltpu.make_async_copy(zeros, xs_ref.at[pl.ds(pl.multiple_of(b * bm, bm), bm), :], sem_zero)

    @pl.when(pl.program_id(0) == 0)
    def _():
        zeros[...] = jnp.zeros_like(zeros)

        def group(start):
            def body(e, c):
                @pl.when(first_ref[e + 1] > first_ref[e])
                def _():
                    cp = zero_copy(first_ref[e + 1] - 1)
                    cp.start() if start else cp.wait()
                return c
            lax.fori_loop(0, N_EXPERTS, body, 0)

        def tail(start):
            def body(b, c):
                cp = zero_copy(b)
                cp.start() if start else cp.wait()
                return c
            lax.fori_loop(nused, n_blocks, body, 0)

        group(True)
        tail(True)
        group(False)
        tail(False)

    def issue(t, c):
        src = h_ref.at[pl.ds(pl.multiple_of(t * ROW_SUB, ROW_SUB), ROW_SUB), :]
        for k in range(TOP_K):
            row0 = pl.multiple_of(slots_ref[t * TOP_K + k], ROW_SUB)
            pltpu.make_async_copy(src, xs_ref.at[pl.ds(row0, ROW_SUB), :], sem_rows).start(priority=k % 2)
        return c

    lax.fori_loop(0, n_tok, issue, 0, unroll=4)
    _wait_rows(h_ref, xs_ref.at[pl.ds(0, n_tok * ROW_SUB), :], sem_rows, TOP_K)


def _dispatch_call(first_blk, slots_flat, h2, n_slots):
    n = h2.shape[0] // ROW_SUB
    tm = DISPATCH_TILE
    grid_spec = pltpu.PrefetchScalarGridSpec(
        num_scalar_prefetch=1,
        grid=(n // tm,),
        in_specs=[
            pl.BlockSpec((tm * TOP_K,), lambda i, first: (i,), memory_space=pltpu.SMEM),
            pl.BlockSpec((ROW_SUB * tm, LANES), lambda i, first: (i, 0)),
        ],
        out_specs=pl.BlockSpec(memory_space=pl.ANY),
        scratch_shapes=[pltpu.VMEM((ROW_SUB * EXPERT_BLOCK, LANES), U32),
                        pltpu.SemaphoreType.DMA,
                        pltpu.SemaphoreType.DMA],
    )
    return pl.pallas_call(
        _dispatch_kernel,
        out_shape=jax.ShapeDtypeStruct((ROW_SUB * n_slots, LANES), U32),
        grid_spec=grid_spec,
        compiler_params=_params(("arbitrary",)),
        name="dispatch",
    )(first_blk, slots_flat, h2)


def _expert_kernel(first_ref, xs_ref, wg_ref, wu_ref, wd_ref, ys_ref,
                   xbuf, ybuf, wg_bf, wu_bf, wd_bf, sem_in, sem_out):
    e = pl.program_id(0)
    bm = EXPERT_BLOCK * ROW_SUB
    n_blocks = ys_ref.shape[0] // bm
    lo = first_ref[e]
    hi = first_ref[e + 1]
    nused = first_ref[N_EXPERTS]

    def rows(b):
        return pl.ds(pl.multiple_of(b * bm, bm), bm)

    def load(b, s):
        return pltpu.make_async_copy(xs_ref.at[rows(b), :], xbuf.at[s], sem_in.at[s])

    def store(b, s):
        return pltpu.make_async_copy(ybuf.at[s], ys_ref.at[rows(b), :], sem_out.at[s])

    ahead = EXPERT_IN_SLOTS - EXPERT_CHUNK

    @pl.when(e == 0)
    def _():
        for j in range(ahead):
            @pl.when(j < nused)
            def _():
                load(j, j).start()

    @pl.when(hi > lo)
    def _():
        wg_bf[...] = wg_ref[...].astype(BF16)
        wu_bf[...] = wu_ref[...].astype(BF16)
        wd_bf[...] = wd_ref[...].astype(BF16)

    def chunk(b, nblk):
        for d in range(nblk):
            nxt = b + ahead + d

            @pl.when(nxt < nused)
            def _():
                load(nxt, nxt % EXPERT_IN_SLOTS).start()

        halves = []
        for d in range(nblk):
            load(b + d, (b + d) % EXPERT_IN_SLOTS).wait()
            halves.append(_load_rows(xbuf.at[(b + d) % EXPERT_IN_SLOTS], EXPERT_BLOCK))
        xa = jnp.concatenate([h[0] for h in halves], axis=0).astype(BF16)
        xb = jnp.concatenate([h[1] for h in halves], axis=0).astype(BF16)
        g = (jnp.dot(xa, wg_bf[:PACKED, :], preferred_element_type=F32)
             + jnp.dot(xb, wg_bf[PACKED:, :], preferred_element_type=F32))
        u = (jnp.dot(xa, wu_bf[:PACKED, :], preferred_element_type=F32)
             + jnp.dot(xb, wu_bf[PACKED:, :], preferred_element_type=F32))
        h = (_silu(g) * u).astype(BF16)
        y = _pack_bf16_pair(jnp.dot(h, wd_bf[...], preferred_element_type=F32))
        for d in range(nblk):
            s = (b + d) % EXPERT_OUT_SLOTS

            @pl.when(b + d >= EXPERT_OUT_SLOTS)
            def _():
                store(b + d - EXPERT_OUT_SLOTS, s).wait()

            _store_rows(ybuf.at[s], y[d * EXPERT_BLOCK:(d + 1) * EXPERT_BLOCK, :])
            store(b + d, s).start()

    def full_chunk(p, c):
        chunk(lo + p * EXPERT_CHUNK, EXPERT_CHUNK)
        return c

    n_full = (hi - lo) // EXPERT_CHUNK
    lax.fori_loop(0, n_full, full_chunk, 0)
    for rest in range(1, EXPERT_CHUNK):
        @pl.when((hi - lo) % EXPERT_CHUNK == rest)
        def _():
            chunk(hi - rest, rest)

    @pl.when(e == pl.num_programs(0) - 1)
    def _():
        for back in range(EXPERT_OUT_SLOTS, 0, -1):
            @pl.when(nused >= back)
            def _():
                store(nused - back, (nused - back) % EXPERT_OUT_SLOTS).wait()

        ybuf[0] = jnp.zeros((bm, LANES), U32)

        def fill(b, c):
            store(b, 0).start()
            return c

        def drain(b, c):
            store(b, 0).wait()
            return c

        lax.fori_loop(nused, n_blocks, fill, 0)
        lax.fori_loop(nused, n_blocks, drain, 0)


def _expert_call(first_blk, xs, w_e_gate, w_e_up, w_e_down, l):
    n_slots = xs.shape[0]
    bm = EXPERT_BLOCK

    def w_idx(e, first_ref):
        return (l, e, 0, 0)

    grid_spec = pltpu.PrefetchScalarGridSpec(
        num_scalar_prefetch=1,
        grid=(N_EXPERTS,),
        in_specs=[
            pl.BlockSpec(memory_space=pl.ANY),
            pl.BlockSpec((None, None, D_MODEL, EXPERT_FF), w_idx),
            pl.BlockSpec((None, None, D_MODEL, EXPERT_FF), w_idx),
            pl.BlockSpec((None, None, EXPERT_FF, D_MODEL), w_idx),
        ],
        out_specs=pl.BlockSpec(memory_space=pl.ANY),
        scratch_shapes=[pltpu.VMEM((EXPERT_IN_SLOTS, ROW_SUB * bm, LANES), U32),
                        pltpu.VMEM((EXPERT_OUT_SLOTS, ROW_SUB * bm, LANES), U32),
                        pltpu.VMEM((D_MODEL, EXPERT_FF), BF16),
                        pltpu.VMEM((D_MODEL, EXPERT_FF), BF16),
                        pltpu.VMEM((EXPERT_FF, D_MODEL), BF16),
                        pltpu.SemaphoreType.DMA((EXPERT_IN_SLOTS,)),
                        pltpu.SemaphoreType.DMA((EXPERT_OUT_SLOTS,))],
    )
    return pl.pallas_call(
        _expert_kernel,
        out_shape=jax.ShapeDtypeStruct(xs.shape, U32),
        grid_spec=grid_spec,
        compiler_params=_params(("arbitrary",)),
        name="experts",
    )(first_blk, xs, w_e_gate, w_e_up, w_e_down)


def _combine_kernel(n_ctx_tiles, slots_ref, slots_next_ref, gw_ref, x1_ref, h2_ref, mod_ref, wsg_ref, wsu_ref,
                    wsd_ref, ys_ref, oc_ref, ol_ref, buf, sems, acc_scr):
    i = pl.program_id(0)
    n_tok = x1_ref.shape[0]
    half = i % 2
    grp = SUBLANES

    def issue(idx_ref, dst_half, t):
        dst_rows = pl.ds(pl.multiple_of(t * ROW_SUB, ROW_SUB), ROW_SUB)
        for k in range(TOP_K):
            row0 = pl.multiple_of(idx_ref[t * TOP_K + k], ROW_SUB)
            pltpu.make_async_copy(ys_ref.at[pl.ds(row0, ROW_SUB), :], buf.at[dst_half, k, dst_rows, :],
                                  sems.at[dst_half]).start(priority=k % 2)

    @pl.when(i == 0)
    def _():
        def body(t, c):
            issue(slots_ref, 0, t)
            return c
        lax.fori_loop(0, n_tok, body, 0)

    ha, hb = _load_rows(h2_ref, n_tok)
    ha = ha.astype(BF16)
    hb = hb.astype(BF16)
    gate = (jnp.dot(ha, wsg_ref[:PACKED, :], preferred_element_type=F32)
            + jnp.dot(hb, wsg_ref[PACKED:, :], preferred_element_type=F32))
    up = (jnp.dot(ha, wsu_ref[:PACKED, :], preferred_element_type=F32)
          + jnp.dot(hb, wsu_ref[PACKED:, :], preferred_element_type=F32))
    acc_scr[...] = jnp.dot((_silu(gate) * up).astype(BF16), wsd_ref[...], preferred_element_type=F32)

    for k in range(TOP_K):
        pltpu.make_async_copy(ys_ref.at[pl.ds(0, n_tok * ROW_SUB), :], buf.at[half, k], sems.at[half]).wait()
    g2 = mod_ref[:, 5 * D_MODEL:6 * D_MODEL]

    def reduce_group(src_half, r0):
        gw = gw_ref[pl.ds(r0, grp), :]
        acc = [acc_scr[pl.ds(r0, grp), j * LANES:(j + 1) * LANES] for j in range(2 * ROW_SUB)]
        for k in range(TOP_K):
            wk = gw[:, k:k + 1]
            src = buf.at[src_half, k]
            for j in range(ROW_SUB):
                hi, lo = _unpack_bf16_pair(src[pl.ds(r0 * ROW_SUB + j, grp, stride=ROW_SUB), :])
                acc[j] = acc[j] + wk * hi
                acc[ROW_SUB + j] = acc[ROW_SUB + j] + wk * lo
        acc_scr[pl.ds(r0, grp), :] = x1_ref[pl.ds(r0, grp), :] + g2 * jnp.concatenate(acc, axis=1)

    def run(src_half, prefetch):
        def body(g, c):
            r0 = pl.multiple_of(g * grp, grp)
            if prefetch:
                for dt in range(grp):
                    issue(slots_next_ref, 1 - src_half, r0 + dt)
            reduce_group(src_half, r0)
            return c
        lax.fori_loop(0, n_tok // grp, body, 0)

    has_next = i + 1 < pl.num_programs(0)
    for src_half in range(2):
        @pl.when(jnp.logical_and(has_next, half == src_half))
        def _():
            run(src_half, True)

        @pl.when(jnp.logical_and(jnp.logical_not(has_next), half == src_half))
        def _():
            run(src_half, False)

    @pl.when(i < n_ctx_tiles)
    def _():
        oc_ref[...] = acc_scr[...]

    @pl.when(i >= n_ctx_tiles)
    def _():
        ol_ref[...] = acc_scr[...]


def _combine_call(slots_flat, gw, x1, h2, mod3, ys, l, P, n_ctx, lat_t):
    n = x1.shape[0]
    tm = COMBINE_TILE
    n_ctx_tiles = n_ctx // tm
    tiles_per_seq = lat_t // tm

    def mod_idx(i):
        return (jnp.where(i < n_ctx_tiles, 0, 1 + (i - n_ctx_tiles) // tiles_per_seq), 0, 0)

    const = lambda i: (0, 0)
    row = lambda i: (i, 0)
    n_tiles = n // tm
    return pl.pallas_call(
        functools.partial(_combine_kernel, n_ctx_tiles),
        out_shape=(jax.ShapeDtypeStruct((n_ctx, D_MODEL), F32),
                   jax.ShapeDtypeStruct((n - n_ctx, D_MODEL), F32)),
        grid=(n_tiles,),
        in_specs=[
            pl.BlockSpec((tm * TOP_K,), lambda i: (i,), memory_space=pltpu.SMEM),
            pl.BlockSpec((tm * TOP_K,), lambda i: (jnp.minimum(i + 1, n_tiles - 1),), memory_space=pltpu.SMEM),
            pl.BlockSpec((tm, TOP_K), row),
            pl.BlockSpec((tm, D_MODEL), row),
            pl.BlockSpec((ROW_SUB * tm, LANES), row),
            pl.BlockSpec((None, 1, 6 * D_MODEL), mod_idx),
            pl.BlockSpec((D_MODEL, EXPERT_FF), const),
            pl.BlockSpec((D_MODEL, EXPERT_FF), const),
            pl.BlockSpec((EXPERT_FF, D_MODEL), const),
            pl.BlockSpec(memory_space=pl.ANY),
        ],
        out_specs=(pl.BlockSpec((tm, D_MODEL), lambda i: (jnp.minimum(i, n_ctx_tiles - 1), 0)),
                   pl.BlockSpec((tm, D_MODEL), lambda i: (jnp.maximum(i - n_ctx_tiles, 0), 0))),
        scratch_shapes=[pltpu.VMEM((2, TOP_K, ROW_SUB * tm, LANES), U32),
                        pltpu.SemaphoreType.DMA((2,)),
                        pltpu.VMEM((tm, D_MODEL), F32)],
        compiler_params=_params(("arbitrary",)),
        name="combine",
    )(slots_flat, slots_flat, gw, x1, h2, mod3, P["w_s_gate_bf16"][l], P["w_s_up_bf16"][l],
      P["w_s_down_bf16"][l], ys)


def _rope_tables(lat_t):
    rows = lat_t // GRID_W
    row = jnp.repeat(jnp.arange(rows, dtype=F32), GRID_W)
    col = jnp.tile(jnp.arange(GRID_W, dtype=F32), rows)
    n_freq = HEAD_DIM // 4
    inv = jnp.power(ROPE_BASE, -jnp.arange(n_freq, dtype=F32) / n_freq)
    ang = jnp.concatenate([row[:, None] * inv, col[:, None] * inv], axis=-1)
    cos = jnp.repeat(jnp.cos(ang), 2, axis=-1)
    sign = jnp.tile(jnp.array([-1.0, 1.0], F32), HEAD_DIM // 2)
    sin = jnp.repeat(jnp.sin(ang), 2, axis=-1) * sign
    cos = jnp.concatenate([cos, jnp.ones((TOKEN_TILE, HEAD_DIM), F32)], axis=0)
    sin = jnp.concatenate([sin, jnp.zeros((TOKEN_TILE, HEAD_DIM), F32)], axis=0)
    reps = LANES // HEAD_DIM
    return jnp.tile(cos, (1, reps)), jnp.tile(sin, (1, reps))


def _block_diag_gates(w_a, w_x):
    def dense(w):
        eye = jnp.eye(LRU_HEADS, dtype=w.dtype)
        return jnp.einsum("ncd,nm->ncmd", w, eye).reshape(LRU_WIDTH, LRU_WIDTH)
    return jnp.concatenate([dense(w_a[0]), dense(w_x[0]), dense(w_a[1]), dense(w_x[1])], axis=1)


def _prepare(P):
    Q = dict(P)
    Q["w_in_bf16"] = P["w_in"].astype(BF16)
    Q["w_out_bf16"] = P["w_out"].astype(BF16)
    Q["q_norm_g_t"] = jnp.tile(P["q_norm_g"], (1, N_HEADS))
    Q["k_norm_g_t"] = jnp.tile(P["k_norm_g"], (1, N_KV_HEADS))
    head = jnp.arange(ATTN_WIDTH) // HEAD_DIM
    Q["head_blockdiag"] = (head[:, None] == head[None, :]).astype(BF16)
    Q["lru_gate_w"] = jnp.stack([_block_diag_gates(P["lru_w_a"][l], P["lru_w_x"][l])
                                 for l in range(DEPTH)]).astype(BF16)
    Q["lru_gate_b"] = jnp.concatenate([P["lru_b_a"][:, 0], P["lru_b_x"][:, 0],
                                       P["lru_b_a"][:, 1], P["lru_b_x"][:, 1]], axis=-1)
    w_hi = P["w_router"].astype(BF16)
    Q["w_router_hi"] = w_hi
    Q["w_router_lo"] = (P["w_router"] - w_hi.astype(F32)).astype(BF16)
    Q["w_s_gate_bf16"] = P["w_s_gate"].astype(BF16)
    Q["w_s_up_bf16"] = P["w_s_up"].astype(BF16)
    Q["w_s_down_bf16"] = P["w_s_down"].astype(BF16)
    return Q


def _group_tables(counts):
    bm = EXPERT_BLOCK
    blocks = (counts.astype(I32) + bm - 1) // bm
    first_blk = jnp.concatenate([jnp.zeros((1,), I32), jnp.cumsum(blocks).astype(I32)])
    group_start = (first_blk[:N_EXPERTS] * bm).astype(F32)[None, :]
    return group_start, first_blk


def _trunk(x_prompt, x_sample, c, cache_k, cache_v, state_lru, c_ctx, P):
    n_seq_c, ctx_t, _ = x_prompt.shape
    n_seq_l, lat_t, _ = x_sample.shape
    past = cache_k.shape[2]
    n_ctx = n_seq_c * ctx_t
    n_lat = n_seq_l * lat_t
    n = n_ctx + n_lat
    assert n_ctx % lat_t == 0, "latent sequences must start on a whole-sequence boundary of the merged token axis"
    lat_off = n_ctx // lat_t
    P = _prepare(P)

    n_cond = -(-(1 + n_seq_l) // SUBLANES) * SUBLANES
    cond = jnp.concatenate([c_ctx[None, :], c, jnp.zeros((n_cond - 1 - n_seq_l, D_MODEL), F32)], axis=0)
    mods = _mod_call(cond, P["w_mod"], P["b_mod"])
    rope_cos, rope_sin = _rope_tables(lat_t)
    tri = (jnp.arange(TOKEN_TILE)[:, None] > jnp.arange(TOKEN_TILE)[None, :]).astype(BF16)
    n_blocks = n * TOP_K // EXPERT_BLOCK + N_EXPERTS
    n_slots = n_blocks * EXPERT_BLOCK

    x_c = x_prompt.reshape(n_ctx, D_MODEL)
    x_l = x_sample.reshape(n_lat, D_MODEL)
    ks, vs, ss = [], [], []
    for l in range(DEPTH):
        mod3 = mods[l].reshape(n_cond, 1, 6 * D_MODEL)
        q, kr, vb, kc, vc, u, gate, cv = _premix_call(x_c, x_l, mod3, l, P, rope_cos, rope_sin, n_ctx, lat_t)
        ks.append(kc[:n_ctx].reshape(n_seq_c, ctx_t, N_KV_HEADS, HEAD_DIM))
        vs.append(vc[:n_ctx].reshape(n_seq_c, ctx_t, N_KV_HEADS, HEAD_DIM))

        as_ctx = lambda a: a.reshape(n // ctx_t, ctx_t, a.shape[-1])
        as_lat = lambda a: a.reshape(n // lat_t, lat_t, a.shape[-1])

        attn_c = _attn_call(as_ctx(q), as_ctx(kr), as_ctx(vb), min(ctx_t, 256), n_seq_c, 0, 0)
        k_all = jnp.concatenate([kr[n_ctx:].reshape(n_seq_l, lat_t, KV_WIDTH),
                                 cache_k[:, l].reshape(n_seq_l, past, KV_WIDTH).astype(BF16)], axis=1)
        cv_ones = jnp.concatenate([cache_v[:, l], jnp.ones_like(cache_v[:, l])], axis=-1)
        v_all = jnp.concatenate([vb[n_ctx:].reshape(n_seq_l, lat_t, 2 * KV_WIDTH),
                                 cv_ones.reshape(n_seq_l, past, 2 * KV_WIDTH).astype(BF16)], axis=1)
        attn_l = _attn_call(as_lat(q), k_all, v_all, 256, n_seq_l, lat_off, 0)

        lru_c, fin_c = _lru_call(as_ctx(u), as_ctx(gate), jnp.zeros((n_seq_c, 2, LRU_WIDTH), F32), l, P,
                                 n_seq_c, 0)
        lru_l, _ = _lru_call(as_lat(u), as_lat(gate), state_lru[:, l], l, P, n_seq_l, lat_off)
        ss.append(fin_c)
        conv_c = _convmod_call(as_ctx(cv), l, P, n_seq_c, 0)
        conv_l = _convmod_call(as_lat(cv), l, P, n_seq_l, lat_off)

        x1, h2, ids, gw, rank, counts = _postmix_call(
            attn_c.reshape(n_ctx, ATTN_WIDTH), attn_l.reshape(n_lat, ATTN_WIDTH),
            lru_c.reshape(n_ctx, LRU_WIDTH), lru_l.reshape(n_lat, LRU_WIDTH),
            conv_c.reshape(n_ctx, CONV_WIDTH), conv_l.reshape(n_lat, CONV_WIDTH),
            x_c, x_l, mod3, tri, l, P, n_ctx, lat_t)
        group_start, first_blk = _group_tables(counts[0])
        slots = _slots_call(ids, rank, group_start).reshape(-1)
        xs = _dispatch_call(first_blk, slots, h2, n_slots)
        ys = _expert_call(first_blk, xs, P["w_e_gate"], P["w_e_up"], P["w_e_down"], l)
        x_c, x_l = _combine_call(slots, gw, x1, h2, mod3, ys, l, P, n_ctx, lat_t)

    y_prompt = x_c.reshape(n_seq_c, ctx_t, D_MODEL)
    y_sample = x_l.reshape(n_seq_l, lat_t, D_MODEL)
    return (y_prompt, y_sample, jnp.stack(ks, axis=1), jnp.stack(vs, axis=1), jnp.stack(ss, axis=1))


def kernel(x_prompt, x_sample, c, cache_k, cache_v, state_lru, c_ctx, w_mod, b_mod, norm1_g, w_in, q_norm_g, k_norm_g, lru_conv_w, lru_conv_b, lru_w_a, lru_b_a, lru_w_x, lru_b_x, lru_lambda, cm_dw_w, cm_dw_b, cm_ln_g, cm_ln_b, out_norm_g, w_out, norm2_g, w_router, b_router, w_e_gate, w_e_up, w_e_down, w_s_gate, w_s_up, w_s_down):
    P = {"w_mod": w_mod, "b_mod": b_mod, "norm1_g": norm1_g, "w_in": w_in, "q_norm_g": q_norm_g,
         "k_norm_g": k_norm_g, "lru_conv_w": lru_conv_w, "lru_conv_b": lru_conv_b, "lru_w_a": lru_w_a,
         "lru_b_a": lru_b_a, "lru_w_x": lru_w_x, "lru_b_x": lru_b_x, "lru_lambda": lru_lambda,
         "cm_dw_w": cm_dw_w, "cm_dw_b": cm_dw_b, "cm_ln_g": cm_ln_g, "cm_ln_b": cm_ln_b,
         "out_norm_g": out_norm_g, "w_out": w_out, "norm2_g": norm2_g, "w_router": w_router,
         "b_router": b_router, "w_e_gate": w_e_gate, "w_e_up": w_e_up, "w_e_down": w_e_down,
         "w_s_gate": w_s_gate, "w_s_up": w_s_up, "w_s_down": w_s_down}
    return _trunk(x_prompt, x_sample, c, cache_k, cache_v, state_lru, c_ctx, P)
```

```python
import functools
import math

import jax
import jax.numpy as jnp
from jax import lax
from jax.experimental import pallas as pl
from jax.experimental.pallas import tpu as pltpu

F32 = jnp.float32
BF16 = jnp.bfloat16
I32 = jnp.int32

D_MODEL = 1024
DEPTH = 2
GRID_W = 64
ATTN_WIDTH = 512
LRU_WIDTH = 256
CONV_WIDTH = 256
HEAD_DIM = 64
N_HEADS = 8
N_KV_HEADS = 2
KV_GROUP = N_HEADS // N_KV_HEADS
KV_WIDTH = N_KV_HEADS * HEAD_DIM
ROPE_BASE = 10000.0
LRU_HEADS = 4
LRU_HEAD_DIM = LRU_WIDTH // LRU_HEADS
LRU_CONV = 4
RG_C = 8.0
CM_KERNEL = 31
N_EXPERTS = 256
TOP_K = 8
EXPERT_FF = 256
ROUTED_SCALE = 2.5
EPS = 1e-6
IN_WIDTH = ATTN_WIDTH + 2 * KV_WIDTH + 2 * LRU_WIDTH + 2 * CONV_WIDTH

SUBLANES = 8
LANES = 128
VMEM_LIMIT = 56 * 1024 * 1024

TOKEN_TILE = 512
ROUTE_TILE = 512
SCAN_CHUNK = 256
CONV_CHUNK = 128
CONV_HALO = 16
LRU_HALO = 8
EXPERT_BLOCK = 256
EXPERT_CHUNK = 2
EXPERT_IN_SLOTS = 6
EXPERT_OUT_SLOTS = 4
COMBINE_TILE = 256
DISPATCH_TILE = 512


def _params(sem):
    return pltpu.CompilerParams(dimension_semantics=sem, vmem_limit_bytes=VMEM_LIMIT)


def _sigmoid(x):
    return 1.0 / (1.0 + jnp.exp(-x))


def _silu(x):
    return x * _sigmoid(x)


def _bdot(a, b):
    return jnp.dot(a.astype(BF16), b.astype(BF16), preferred_element_type=F32)


def _split_dot(a, b_bf16):
    hi = a.astype(BF16)
    lo = (a - hi.astype(F32)).astype(BF16)
    return (jnp.dot(hi, b_bf16, preferred_element_type=F32)
            + jnp.dot(lo, b_bf16, preferred_element_type=F32))


U32 = jnp.uint32
PACKED = D_MODEL // 2


def _pack_bf16_pair(x):
    hi = lax.bitcast_convert_type(x[:, :PACKED].astype(BF16).astype(F32), U32)
    lo = lax.bitcast_convert_type(x[:, PACKED:].astype(BF16).astype(F32), U32)
    return (hi & jnp.uint32(0xFFFF0000)) | (lo >> 16)


def _unpack_bf16_pair(w):
    hi = lax.bitcast_convert_type(w & jnp.uint32(0xFFFF0000), F32)
    lo = lax.bitcast_convert_type(w << 16, F32)
    return hi, lo


ROW_SUB = PACKED // LANES


def _store_rows(ref, w):
    r = w.shape[0]
    for j in range(ROW_SUB):
        ref[pl.ds(j, r, stride=ROW_SUB), :] = w[:, j * LANES:(j + 1) * LANES]


def _load_rows(ref, r):
    his, los = [], []
    for j in range(ROW_SUB):
        hi, lo = _unpack_bf16_pair(ref[pl.ds(j, r, stride=ROW_SUB), :])
        his.append(hi)
        los.append(lo)
    return jnp.concatenate(his, axis=1), jnp.concatenate(los, axis=1)


def _rms(x, g):
    return x * lax.rsqrt(jnp.mean(x * x, axis=-1, keepdims=True) + EPS) * g


MOD_COLS = 1536


def _mod_kernel(c_ref, w_ref, b_ref, o_ref):
    a = _silu(c_ref[...])
    o_ref[...] = jnp.dot(a, w_ref[...], preferred_element_type=F32,
                         precision=lax.Precision.HIGHEST) + b_ref[...]


def _mod_call(cond, w_mod, b_mod):
    n_cond = cond.shape[0]
    width = 6 * D_MODEL
    return pl.pallas_call(
        _mod_kernel,
        out_shape=jax.ShapeDtypeStruct((DEPTH, n_cond, width), F32),
        grid=(DEPTH, width // MOD_COLS),
        in_specs=[
            pl.BlockSpec((n_cond, D_MODEL), lambda l, j: (0, 0)),
            pl.BlockSpec((None, D_MODEL, MOD_COLS), lambda l, j: (l, 0, j)),
            pl.BlockSpec((None, 1, MOD_COLS), lambda l, j: (l, 0, j)),
        ],
        out_specs=pl.BlockSpec((None, n_cond, MOD_COLS), lambda l, j: (l, 0, j)),
        compiler_params=_params(("arbitrary", "arbitrary")),
        name="mod",
    )(cond, w_mod, b_mod.reshape(DEPTH, 1, width))


def _swap_pairs(x):
    n = x.shape[-1]
    lane = lax.broadcasted_iota(I32, x.shape, 1)
    nxt = pltpu.roll(x, n - 1, 1)
    prv = pltpu.roll(x, 1, 1)
    return jnp.where(lane % 2 == 0, nxt, prv)


def _premix_kernel(n_ctx_tiles, xc_ref, xl_ref, mod_ref, n1g_ref, w_ref, qg_ref, kg_ref, bd_ref, cos_ref, sin_ref,
                   q_ref, kr_ref, vb_ref, kc_ref, vc_ref, u_ref, gate_ref, cv_ref):
    x = jnp.where(pl.program_id(0) < n_ctx_tiles, xc_ref[...], xl_ref[...])
    sh1 = mod_ref[:, 0:D_MODEL]
    sc1 = mod_ref[:, D_MODEL:2 * D_MODEL]
    h = _rms(x, n1g_ref[...]) * (1.0 + sc1) + sh1
    z = jnp.dot(h.astype(BF16), w_ref[...], preferred_element_type=F32)
    o = 0
    q = z[:, o:o + ATTN_WIDTH]; o += ATTN_WIDTH
    k = z[:, o:o + KV_WIDTH]; o += KV_WIDTH
    v = z[:, o:o + KV_WIDTH]; o += KV_WIDTH
    u_ref[...] = z[:, o:o + LRU_WIDTH]; o += LRU_WIDTH
    gate_ref[...] = z[:, o:o + LRU_WIDTH]; o += LRU_WIDTH
    cv_ref[...] = z[:, o:o + 2 * CONV_WIDTH]

    bd = bd_ref[...]
    inv_hd = 1.0 / HEAD_DIM
    q_ms = _split_dot(q * q, bd) * inv_hd
    k_ms = _split_dot(k * k, bd[:KV_WIDTH, :KV_WIDTH]) * inv_hd
    qn = q * lax.rsqrt(q_ms + EPS) * qg_ref[...]
    kn = k * lax.rsqrt(k_ms + EPS) * kg_ref[...]
    kc_ref[...] = kn
    vc_ref[...] = v
    ones = jnp.ones((v.shape[0], HEAD_DIM), F32)
    vb_ref[...] = jnp.concatenate([v[:, :HEAD_DIM], ones, v[:, HEAD_DIM:], ones], axis=1).astype(BF16)

    cos = cos_ref[...]
    sin = sin_ref[...]
    reps = ATTN_WIDTH // cos.shape[-1]
    cos_q = jnp.concatenate([cos] * reps, axis=1)
    sin_q = jnp.concatenate([sin] * reps, axis=1)
    qr = qn * cos_q + _swap_pairs(qn) * sin_q
    kr = kn * cos + _swap_pairs(kn) * sin
    q_ref[...] = (qr * (HEAD_DIM ** -0.5 * math.log2(math.e))).astype(BF16)
    kr_ref[...] = kr.astype(BF16)


def _premix_call(x_c, x_l, mod3, l, P, rope_cos, rope_sin, n_ctx, lat_t):
    n = x_c.shape[0] + x_l.shape[0]
    tm = TOKEN_TILE
    n_ctx_tiles = n_ctx // tm
    tiles_per_seq = lat_t // tm

    def mod_idx(i):
        return (jnp.where(i < n_ctx_tiles, 0, 1 + (i - n_ctx_tiles) // tiles_per_seq), 0, 0)

    def rope_idx(i):
        return (jnp.where(i < n_ctx_tiles, tiles_per_seq, (i - n_ctx_tiles) % tiles_per_seq), 0)

    const = lambda i: (0, 0)
    row = lambda i: (i, 0)
    row_c = lambda i: (jnp.minimum(i, n_ctx_tiles - 1), 0)
    row_l = lambda i: (jnp.maximum(i - n_ctx_tiles, 0), 0)
    outs = pl.pallas_call(
        functools.partial(_premix_kernel, n_ctx_tiles),
        out_shape=(
            jax.ShapeDtypeStruct((n, ATTN_WIDTH), BF16),
            jax.ShapeDtypeStruct((n, KV_WIDTH), BF16),
            jax.ShapeDtypeStruct((n, 2 * KV_WIDTH), BF16),
            jax.ShapeDtypeStruct((n, KV_WIDTH), F32),
            jax.ShapeDtypeStruct((n, KV_WIDTH), F32),
            jax.ShapeDtypeStruct((n, LRU_WIDTH), F32),
            jax.ShapeDtypeStruct((n, LRU_WIDTH), F32),
            jax.ShapeDtypeStruct((n, 2 * CONV_WIDTH), F32),
        ),
        grid=(n // tm,),
        in_specs=[
            pl.BlockSpec((tm, D_MODEL), row_c),
            pl.BlockSpec((tm, D_MODEL), row_l),
            pl.BlockSpec((None, 1, 6 * D_MODEL), mod_idx),
            pl.BlockSpec((1, D_MODEL), const),
            pl.BlockSpec((D_MODEL, IN_WIDTH), const),
            pl.BlockSpec((1, ATTN_WIDTH), const),
            pl.BlockSpec((1, KV_WIDTH), const),
            pl.BlockSpec((ATTN_WIDTH, ATTN_WIDTH), const),
            pl.BlockSpec((tm, LANES), rope_idx),
            pl.BlockSpec((tm, LANES), rope_idx),
        ],
        out_specs=(
            pl.BlockSpec((tm, ATTN_WIDTH), row),
            pl.BlockSpec((tm, KV_WIDTH), row),
            pl.BlockSpec((tm, 2 * KV_WIDTH), row),
            pl.BlockSpec((tm, KV_WIDTH), row),
            pl.BlockSpec((tm, KV_WIDTH), row),
            pl.BlockSpec((tm, LRU_WIDTH), row),
            pl.BlockSpec((tm, LRU_WIDTH), row),
            pl.BlockSpec((tm, 2 * CONV_WIDTH), row),
        ),
        compiler_params=_params(("parallel",)),
        name="premix",
    )(x_c, x_l, mod3, P["norm1_g"][l][None, :], P["w_in_bf16"][l], P["q_norm_g_t"][l][None, :],
      P["k_norm_g_t"][l][None, :], P["head_blockdiag"], rope_cos, rope_sin)
    return outs


def _attn_kernel(q_ref, k_ref, v_ref, o_ref):
    outs = []
    for kh in range(N_KV_HEADS):
        k = k_ref[:, kh * HEAD_DIM:(kh + 1) * HEAD_DIM]
        v = v_ref[:, kh * 2 * HEAD_DIM:(kh + 1) * 2 * HEAD_DIM]
        for g in range(KV_GROUP):
            hd = kh * KV_GROUP + g
            q = q_ref[:, hd * HEAD_DIM:(hd + 1) * HEAD_DIM]
            s = lax.dot_general(q, k, (((1,), (1,)), ((), ())), preferred_element_type=F32)
            m = jnp.max(s, axis=-1, keepdims=True)
            p = jnp.exp2(s - m).astype(BF16)
            o = jnp.dot(p, v, preferred_element_type=F32)
            outs.append(o[:, :HEAD_DIM] / o[:, HEAD_DIM:HEAD_DIM + 1])
    o_ref[...] = jnp.concatenate(outs, axis=1)


def _attn_call(q3, k3, v3, tq, b, q_off, kv_off):
    s = q3.shape[1]
    t = k3.shape[1]
    return pl.pallas_call(
        _attn_kernel,
        out_shape=jax.ShapeDtypeStruct((b, s, ATTN_WIDTH), F32),
        grid=(b, s // tq),
        in_specs=[
            pl.BlockSpec((None, tq, ATTN_WIDTH), lambda i, j: (i + q_off, j, 0)),
            pl.BlockSpec((None, t, KV_WIDTH), lambda i, j: (i + kv_off, 0, 0)),
            pl.BlockSpec((None, t, 2 * KV_WIDTH), lambda i, j: (i + kv_off, 0, 0)),
        ],
        out_specs=pl.BlockSpec((None, tq, ATTN_WIDTH), lambda i, j: (i, j, 0)),
        compiler_params=_params(("parallel", "parallel")),
        name="attention",
    )(q3, k3, v3)


def _tap_sum(win, taps, bias, first, n_taps, rows):
    n = win.shape[0]
    acc = bias
    for phase in range(SUBLANES):
        starts = [first + j for j in range(n_taps) if (first + j) % SUBLANES == phase]
        if not starts:
            continue
        rolled = win if phase == 0 else pltpu.roll(win, n - phase, 0)
        for s0 in starts:
            base = s0 - phase
            acc = acc + taps[s0 - first:s0 - first + 1, :] * rolled[base:base + rows, :]
    return acc


def _chunk_scan(a, b, h, reverse):
    n, w = a.shape
    groups = n // SUBLANES
    a = a.reshape(groups, SUBLANES, w)
    b = b.reshape(groups, SUBLANES, w)
    sub = lax.broadcasted_iota(I32, a.shape, 1)
    d = 1
    while d < SUBLANES:
        if reverse:
            a_s = pltpu.roll(a, SUBLANES - d, 1)
            b_s = pltpu.roll(b, SUBLANES - d, 1)
            ok = sub < SUBLANES - d
        else:
            a_s = pltpu.roll(a, d, 1)
            b_s = pltpu.roll(b, d, 1)
            ok = sub >= d
        b = jnp.where(ok, a * b_s + b, b)
        a = jnp.where(ok, a * a_s, a)
        d *= 2
    outs = [None] * groups
    for g in (range(groups - 1, -1, -1) if reverse else range(groups)):
        hg = a[g] * h + b[g]
        outs[g] = hg
        h = hg[0:1, :] if reverse else hg[SUBLANES - 1:SUBLANES, :]
    return jnp.concatenate(outs, axis=0), h


def _gelu_tanh(x):
    return 0.5 * x * (1.0 + jnp.tanh(math.sqrt(2.0 / math.pi) * (x + 0.044715 * (x * x * x))))


def _lru_kernel(u_ref, gate_ref, cw_ref, cb_ref, wg_ref, bg_ref, lam_ref, h0_ref,
                y_ref, fin_ref, upad, xc_scr, fwd_scr):
    t_len = u_ref.shape[0]
    ch = SCAN_CHUNK
    n_chunks = t_len // ch
    w = LRU_WIDTH
    zeros_halo = jnp.zeros((LRU_HALO, w), F32)
    upad[0:LRU_HALO, :] = zeros_halo
    upad[t_len + LRU_HALO:t_len + 2 * LRU_HALO, :] = zeros_halo

    def fill(c, carry):
        r0 = pl.multiple_of(c * ch, ch)
        upad[pl.ds(r0 + LRU_HALO, ch), :] = u_ref[pl.ds(r0, ch), :]
        return carry

    lax.fori_loop(0, n_chunks, fill, 0)

    lam = lam_ref[...]
    nlam = -lam
    softplus = jnp.maximum(nlam, 0.0) + jnp.log(1.0 + jnp.exp(-jnp.abs(nlam)))
    decay = -RG_C * softplus
    cw = cw_ref[...]
    cb = cb_ref[...]

    def gates(xc, d):
        z = jnp.dot(xc.astype(BF16), wg_ref[:, 2 * d * w:2 * (d + 1) * w],
                    preferred_element_type=F32) + bg_ref[:, 2 * d * w:2 * (d + 1) * w]
        r = _sigmoid(z[:, :w])
        i = _sigmoid(z[:, w:])
        log_a = decay[d:d + 1, :] * r
        a = jnp.exp(log_a)
        b = jnp.sqrt(-jnp.tanh(log_a) * (a * a + 1.0)) * (i * xc)
        return a, b

    def fwd(c, h):
        r0 = pl.multiple_of(c * ch, ch)
        win = upad[pl.ds(r0, ch + 2 * LRU_HALO), :]
        xc = _tap_sum(win, cw, cb, LRU_HALO - (LRU_CONV - 1) // 2, LRU_CONV, ch)
        xc_scr[pl.ds(r0, ch), :] = xc
        a, b = gates(xc, 0)
        hs, h_out = _chunk_scan(a, b, h, reverse=False)
        fwd_scr[pl.ds(r0, ch), :] = hs
        return h_out

    h_f = lax.fori_loop(0, n_chunks, fwd, h0_ref[0:1, :])

    def bwd(ci, h):
        c = n_chunks - 1 - ci
        r0 = pl.multiple_of(c * ch, ch)
        xc = xc_scr[pl.ds(r0, ch), :]
        a, b = gates(xc, 1)
        hs, h_out = _chunk_scan(a, b, h, reverse=True)
        y_ref[pl.ds(r0, ch), :] = (fwd_scr[pl.ds(r0, ch), :] + hs) * _gelu_tanh(gate_ref[pl.ds(r0, ch), :])
        return h_out

    h_b = lax.fori_loop(0, n_chunks, bwd, h0_ref[1:2, :])
    fin_ref[0:1, :] = h_f
    fin_ref[1:2, :] = h_b


def _lru_call(u3, gate3, h0, l, P, b, seq_off):
    _, t, w = u3.shape
    const = lambda i: (0, 0)
    seq = lambda i: (i, 0, 0)
    seq_in = lambda i: (i + seq_off, 0, 0)
    return pl.pallas_call(
        _lru_kernel,
        out_shape=(jax.ShapeDtypeStruct((b, t, w), F32),
                   jax.ShapeDtypeStruct((b, 2, w), F32)),
        grid=(b,),
        in_specs=[
            pl.BlockSpec((None, t, w), seq_in),
            pl.BlockSpec((None, t, w), seq_in),
            pl.BlockSpec((LRU_CONV, w), const),
            pl.BlockSpec((1, w), const),
            pl.BlockSpec((w, 4 * w), const),
            pl.BlockSpec((1, 4 * w), const),
            pl.BlockSpec((2, w), const),
            pl.BlockSpec((None, 2, w), seq),
        ],
        out_specs=(pl.BlockSpec((None, t, w), seq),
                   pl.BlockSpec((None, 2, w), seq)),
        scratch_shapes=[pltpu.VMEM((t + 2 * LRU_HALO, w), F32),
                        pltpu.VMEM((t, w), F32),
                        pltpu.VMEM((t, w), F32)],
        compiler_params=_params(("parallel",)),
        name="rglru",
    )(u3, gate3, P["lru_conv_w"][l], P["lru_conv_b"][l][None, :], P["lru_gate_w"][l],
      P["lru_gate_b"][l][None, :], P["lru_lambda"][l], h0)


def _convmod_kernel(cv_ref, w_ref, b_ref, g_ref, beta_ref, o_ref, hpad):
    t_len = cv_ref.shape[0]
    ch = CONV_CHUNK
    n_chunks = t_len // ch
    w = CONV_WIDTH
    zeros_halo = jnp.zeros((CONV_HALO, w), F32)
    hpad[0:CONV_HALO, :] = zeros_halo
    hpad[t_len + CONV_HALO:t_len + 2 * CONV_HALO, :] = zeros_halo

    def glu(c, carry):
        r0 = pl.multiple_of(c * ch, ch)
        blk = cv_ref[pl.ds(r0, ch), :]
        hpad[pl.ds(r0 + CONV_HALO, ch), :] = blk[:, :w] * _sigmoid(blk[:, w:])
        return carry

    lax.fori_loop(0, n_chunks, glu, 0)

    taps = w_ref[...]
    bias = b_ref[...]
    gamma = g_ref[...]
    beta = beta_ref[...]

    def conv(c, carry):
        r0 = pl.multiple_of(c * ch, ch)
        win = hpad[pl.ds(r0, ch + 2 * CONV_HALO), :]
        acc = _tap_sum(win, taps, bias, CONV_HALO - CM_KERNEL // 2, CM_KERNEL, ch)
        mu = jnp.mean(acc, axis=-1, keepdims=True)
        cen = acc - mu
        var = jnp.mean(cen * cen, axis=-1, keepdims=True)
        y = cen * lax.rsqrt(var + EPS) * gamma + beta
        o_ref[pl.ds(r0, ch), :] = _silu(y)
        return carry

    lax.fori_loop(0, n_chunks, conv, 0)


def _convmod_call(cv3, l, P, b, seq_off):
    t = cv3.shape[1]
    w = CONV_WIDTH
    const = lambda i: (0, 0)
    seq = lambda i: (i, 0, 0)
    return pl.pallas_call(
        _convmod_kernel,
        out_shape=jax.ShapeDtypeStruct((b, t, w), F32),
        grid=(b,),
        in_specs=[
            pl.BlockSpec((None, t, 2 * w), lambda i: (i + seq_off, 0, 0)),
            pl.BlockSpec((CM_KERNEL, w), const),
            pl.BlockSpec((1, w), const),
            pl.BlockSpec((1, w), const),
            pl.BlockSpec((1, w), const),
        ],
        out_specs=pl.BlockSpec((None, t, w), seq),
        scratch_shapes=[pltpu.VMEM((t + 2 * CONV_HALO, w), F32)],
        compiler_params=_params(("parallel",)),
        name="convmod",
    )(cv3, P["cm_dw_w"][l], P["cm_dw_b"][l][None, :], P["cm_ln_g"][l][None, :], P["cm_ln_b"][l][None, :])


def _postmix_kernel(n_ctx_tiles, attn_c_ref, attn_l_ref, lru_c_ref, lru_l_ref, conv_c_ref, conv_l_ref,
                    xc_ref, xl_ref, mod_ref, og_ref, wo_ref, n2g_ref, wr_hi_ref, wr_lo_ref, br_ref, tri_ref,
                    x1_ref, h2_ref, ids_ref, gw_ref, rank_ref, cnt_ref, carry):
    is_ctx = pl.program_id(0) < n_ctx_tiles
    og = og_ref[...]
    a0, a1, a2 = ATTN_WIDTH, ATTN_WIDTH + LRU_WIDTH, D_MODEL
    y = jnp.concatenate([_rms(jnp.where(is_ctx, attn_c_ref[...], attn_l_ref[...]), og[:, :a0]),
                         _rms(jnp.where(is_ctx, lru_c_ref[...], lru_l_ref[...]), og[:, a0:a1]),
                         _rms(jnp.where(is_ctx, conv_c_ref[...], conv_l_ref[...]), og[:, a1:a2])], axis=1)
    y = jnp.dot(y.astype(BF16), wo_ref[...], preferred_element_type=F32)
    g1 = mod_ref[:, 2 * D_MODEL:3 * D_MODEL]
    sh2 = mod_ref[:, 3 * D_MODEL:4 * D_MODEL]
    sc2 = mod_ref[:, 4 * D_MODEL:5 * D_MODEL]
    x1 = jnp.where(is_ctx, xc_ref[...], xl_ref[...]) + g1 * y
    x1_ref[...] = x1
    h2 = _rms(x1, n2g_ref[...]) * (1.0 + sc2) + sh2
    _store_rows(h2_ref, _pack_bf16_pair(h2))
    hi = h2.astype(BF16)
    lo = (h2 - hi.astype(F32)).astype(BF16)
    w_hi = wr_hi_ref[...]
    logits = (jnp.dot(hi, w_hi, preferred_element_type=F32)
              + jnp.dot(lo, w_hi, preferred_element_type=F32)
              + jnp.dot(hi, wr_lo_ref[...], preferred_element_type=F32))
    _route_tile(logits, br_ref, tri_ref, ids_ref, gw_ref, rank_ref, cnt_ref, carry)


def _postmix_call(attn_c, attn_l, lru_c, lru_l, conv_c, conv_l, x_c, x_l, mod3, tri, l, P, n_ctx, lat_t):
    n = x_c.shape[0] + x_l.shape[0]
    tm = TOKEN_TILE
    n_ctx_tiles = n_ctx // tm
    tiles_per_seq = lat_t // tm

    def mod_idx(i):
        return (jnp.where(i < n_ctx_tiles, 0, 1 + (i - n_ctx_tiles) // tiles_per_seq), 0, 0)

    const = lambda i: (0, 0)
    row = lambda i: (i, 0)
    row_c = lambda i: (jnp.minimum(i, n_ctx_tiles - 1), 0)
    row_l = lambda i: (jnp.maximum(i - n_ctx_tiles, 0), 0)
    return pl.pallas_call(
        functools.partial(_postmix_kernel, n_ctx_tiles),
        out_shape=(jax.ShapeDtypeStruct((n, D_MODEL), F32),
                   jax.ShapeDtypeStruct((ROW_SUB * n, LANES), U32),
                   jax.ShapeDtypeStruct((n, TOP_K), I32),
                   jax.ShapeDtypeStruct((n, TOP_K), F32),
                   jax.ShapeDtypeStruct((n, TOP_K), I32),
                   jax.ShapeDtypeStruct((SUBLANES, N_EXPERTS), F32)),
        grid=(n // tm,),
        in_specs=[
            pl.BlockSpec((tm, ATTN_WIDTH), row_c),
            pl.BlockSpec((tm, ATTN_WIDTH), row_l),
            pl.BlockSpec((tm, LRU_WIDTH), row_c),
            pl.BlockSpec((tm, LRU_WIDTH), row_l),
            pl.BlockSpec((tm, CONV_WIDTH), row_c),
            pl.BlockSpec((tm, CONV_WIDTH), row_l),
            pl.BlockSpec((tm, D_MODEL), row_c),
            pl.BlockSpec((tm, D_MODEL), row_l),
            pl.BlockSpec((None, 1, 6 * D_MODEL), mod_idx),
            pl.BlockSpec((1, D_MODEL), const),
            pl.BlockSpec((D_MODEL, D_MODEL), const),
            pl.BlockSpec((1, D_MODEL), const),
            pl.BlockSpec((D_MODEL, N_EXPERTS), const),
            pl.BlockSpec((D_MODEL, N_EXPERTS), const),
            pl.BlockSpec((1, N_EXPERTS), const),
            pl.BlockSpec((tm, tm), const),
        ],
        out_specs=(pl.BlockSpec((tm, D_MODEL), row),
                   pl.BlockSpec((ROW_SUB * tm, LANES), row),
                   pl.BlockSpec((tm, TOP_K), row),
                   pl.BlockSpec((tm, TOP_K), row),
                   pl.BlockSpec((tm, TOP_K), row),
                   pl.BlockSpec((SUBLANES, N_EXPERTS), const)),
        scratch_shapes=[pltpu.VMEM((SUBLANES, N_EXPERTS), F32)],
        compiler_params=_params(("arbitrary",)),
        name="postmix",
    )(attn_c, attn_l, lru_c, lru_l, conv_c, conv_l, x_c, x_l, mod3, P["out_norm_g"][l][None, :], P["w_out_bf16"][l],
      P["norm2_g"][l][None, :], P["w_router_hi"][l], P["w_router_lo"][l], P["b_router"][l][None, :], tri)


def _route_tile(logits, br_ref, tri_ref, ids_ref, gw_ref, rank_ref, cnt_ref, carry):
    i = pl.program_id(0)

    @pl.when(i == 0)
    def _():
        carry[...] = jnp.zeros_like(carry)

    scores = _sigmoid(logits)
    sel = scores + br_ref[...]
    tm = scores.shape[0]
    lane = lax.broadcasted_iota(I32, (tm, N_EXPERTS), 1).astype(F32)
    slot_lane = lax.broadcasted_iota(I32, (tm, LANES), 1)
    ids_acc = jnp.zeros((tm, LANES), F32)
    gw_acc = jnp.zeros((tm, LANES), F32)
    hot = jnp.zeros((tm, N_EXPERTS), F32)
    picked = []
    for k in range(TOP_K):
        m = jnp.max(sel, axis=-1, keepdims=True)
        idx = jnp.min(jnp.where(sel == m, lane, float(N_EXPERTS)), axis=-1, keepdims=True)
        one = lane == idx
        g = jnp.sum(jnp.where(one, scores, 0.0), axis=-1, keepdims=True)
        sel = jnp.where(one, -jnp.inf, sel)
        hot = jnp.where(one, 1.0, hot)
        ids_acc = jnp.where(slot_lane == k, idx, ids_acc)
        gw_acc = jnp.where(slot_lane == k, g, gw_acc)
        picked.append(idx)
    denom = jnp.sum(gw_acc, axis=-1, keepdims=True)
    gw_acc = ROUTED_SCALE * gw_acc / denom

    before = jnp.dot(tri_ref[...], hot.astype(BF16), preferred_element_type=F32) + carry[0:1, :]
    rank_acc = jnp.zeros((tm, LANES), F32)
    for k in range(TOP_K):
        r = jnp.sum(jnp.where(lane == picked[k], before, 0.0), axis=-1, keepdims=True)
        rank_acc = jnp.where(slot_lane == k, r, rank_acc)
    carry[0:1, :] = carry[0:1, :] + jnp.sum(hot, axis=0, keepdims=True)
    cnt_ref[...] = carry[...]
    ids_ref[...] = ids_acc[:, :TOP_K].astype(I32)
    gw_ref[...] = gw_acc[:, :TOP_K]
    rank_ref[...] = rank_acc[:, :TOP_K].astype(I32)


def _slots_kernel(ids_ref, rank_ref, start_ref, slots_ref):
    ids = ids_ref[...].astype(F32)
    tm = ids.shape[0]
    lane = lax.broadcasted_iota(I32, (tm, N_EXPERTS), 1).astype(F32)
    slot_lane = lax.broadcasted_iota(I32, (tm, TOP_K), 1)
    start = start_ref[...]
    acc = jnp.zeros((tm, TOP_K), F32)
    for k in range(TOP_K):
        s = jnp.sum(jnp.where(lane == ids[:, k:k + 1], start, 0.0), axis=-1, keepdims=True)
        acc = jnp.where(slot_lane == k, s, acc)
    slots_ref[...] = (acc.astype(I32) + rank_ref[...]) * ROW_SUB


def _slots_call(ids, rank, group_start):
    n = ids.shape[0]
    tm = ROUTE_TILE
    row = lambda i: (i, 0)
    return pl.pallas_call(
        _slots_kernel,
        out_shape=jax.ShapeDtypeStruct((n, TOP_K), I32),
        grid=(n // tm,),
        in_specs=[pl.BlockSpec((tm, TOP_K), row), pl.BlockSpec((tm, TOP_K), row),
                  pl.BlockSpec((1, N_EXPERTS), lambda i: (0, 0))],
        out_specs=pl.BlockSpec((tm, TOP_K), row),
        compiler_params=_params(("parallel",)),
        name="slots",
    )(ids, rank, group_start)


def _wait_rows(src_ref, dst_ref, sem, n_waits):
    for _ in range(n_waits):
        pltpu.make_async_copy(src_ref, dst_ref, sem).wait()


def _dispatch_kernel(first_ref, slots_ref, h_ref, xs_ref, zeros, sem_rows, sem_zero):
    n_tok = h_ref.shape[0] // ROW_SUB
    bm = EXPERT_BLOCK * ROW_SUB
    n_blocks = xs_ref.shape[0] // bm
    nused = first_ref[N_EXPERTS]

    def zero_copy(b):
        return pltpu.make_async_copy(zeros, xs_ref.at[pl.ds(pl.multiple_of(b * bm, bm), bm), :], sem_zero)

    @pl.when(pl.program_id(0) == 0)
    def _():
        zeros[...] = jnp.zeros_like(zeros)

        def group(start):
            def body(e, c):
                @pl.when(first_ref[e + 1] > first_ref[e])
                def _():
                    cp = zero_copy(first_ref[e + 1] - 1)
                    cp.start() if start else cp.wait()
                return c
            lax.fori_loop(0, N_EXPERTS, body, 0)

        def tail(start):
            def body(b, c):
                cp = zero_copy(b)
                cp.start() if start else cp.wait()
                return c
            lax.fori_loop(nused, n_blocks, body, 0)

        group(True)
        tail(True)
        group(False)
        tail(False)

    def issue(t, c):
        src = h_ref.at[pl.ds(pl.multiple_of(t * ROW_SUB, ROW_SUB), ROW_SUB), :]
        for k in range(TOP_K):
            row0 = pl.multiple_of(slots_ref[t * TOP_K + k], ROW_SUB)
            pltpu.make_async_copy(src, xs_ref.at[pl.ds(row0, ROW_SUB), :], sem_rows).start(priority=k % 2)
        return c

    lax.fori_loop(0, n_tok, issue, 0, unroll=4)
    _wait_rows(h_ref, xs_ref.at[pl.ds(0, n_tok * ROW_SUB), :], sem_rows, TOP_K)


def _dispatch_call(first_blk, slots_flat, h2, n_slots):
    n = h2.shape[0] // ROW_SUB
    tm = DISPATCH_TILE
    grid_spec = pltpu.PrefetchScalarGridSpec(
        num_scalar_prefetch=1,
        grid=(n // tm,),
        in_specs=[
            pl.BlockSpec((tm * TOP_K,), lambda i, first: (i,), memory_space=pltpu.SMEM),
            pl.BlockSpec((ROW_SUB * tm, LANES), lambda i, first: (i, 0)),
        ],
        out_specs=pl.BlockSpec(memory_space=pl.ANY),
        scratch_shapes=[pltpu.VMEM((ROW_SUB * EXPERT_BLOCK, LANES), U32),
                        pltpu.SemaphoreType.DMA,
                        pltpu.SemaphoreType.DMA],
    )
    return pl.pallas_call(
        _dispatch_kernel,
        out_shape=jax.ShapeDtypeStruct((ROW_SUB * n_slots, LANES), U32),
        grid_spec=grid_spec,
        compiler_params=_params(("arbitrary",)),
        name="dispatch",
    )(first_blk, slots_flat, h2)


def _expert_kernel(first_ref, xs_ref, wg_ref, wu_ref, wd_ref, ys_ref,
                   xbuf, ybuf, sem_in, sem_out):
    e = pl.program_id(0)
    bm = EXPERT_BLOCK * ROW_SUB
    n_blocks = ys_ref.shape[0] // bm
    lo = first_ref[e]
    hi = first_ref[e + 1]
    nused = first_ref[N_EXPERTS]

    def rows(b):
        return pl.ds(pl.multiple_of(b * bm, bm), bm)

    def load(b, s):
        return pltpu.make_async_copy(xs_ref.at[rows(b), :], xbuf.at[s], sem_in.at[s])

    def store(b, s):
        return pltpu.make_async_copy(ybuf.at[s], ys_ref.at[rows(b), :], sem_out.at[s])

    ahead = EXPERT_IN_SLOTS - EXPERT_CHUNK

    @pl.when(e == 0)
    def _():
        for j in range(ahead):
            @pl.when(j < nused)
            def _():
                load(j, j).start()

    def chunk(b, nblk):
        for d in range(nblk):
            nxt = b + ahead + d

            @pl.when(nxt < nused)
            def _():
                load(nxt, nxt % EXPERT_IN_SLOTS).start()

        halves = []
        for d in range(nblk):
            load(b + d, (b + d) % EXPERT_IN_SLOTS).wait()
            halves.append(_load_rows(xbuf.at[(b + d) % EXPERT_IN_SLOTS], EXPERT_BLOCK))
        xa = jnp.concatenate([h[0] for h in halves], axis=0).astype(BF16)
        xb = jnp.concatenate([h[1] for h in halves], axis=0).astype(BF16)
        g = (jnp.dot(xa, wg_ref[:PACKED, :].astype(BF16), preferred_element_type=F32)
             + jnp.dot(xb, wg_ref[PACKED:, :].astype(BF16), preferred_element_type=F32))
        u = (jnp.dot(xa, wu_ref[:PACKED, :].astype(BF16), preferred_element_type=F32)
             + jnp.dot(xb, wu_ref[PACKED:, :].astype(BF16), preferred_element_type=F32))
        h = (_silu(g) * u).astype(BF16)
        y = _pack_bf16_pair(jnp.dot(h, wd_ref[...].astype(BF16), preferred_element_type=F32))
        for d in range(nblk):
            s = (b + d) % EXPERT_OUT_SLOTS

            @pl.when(b + d >= EXPERT_OUT_SLOTS)
            def _():
                store(b + d - EXPERT_OUT_SLOTS, s).wait()

            _store_rows(ybuf.at[s], y[d * EXPERT_BLOCK:(d + 1) * EXPERT_BLOCK, :])
            store(b + d, s).start()

    def full_chunk(p, c):
        chunk(lo + p * EXPERT_CHUNK, EXPERT_CHUNK)
        return c

    n_full = (hi - lo) // EXPERT_CHUNK
    lax.fori_loop(0, n_full, full_chunk, 0)
    for rest in range(1, EXPERT_CHUNK):
        @pl.when((hi - lo) % EXPERT_CHUNK == rest)
        def _():
            chunk(hi - rest, rest)

    @pl.when(e == pl.num_programs(0) - 1)
    def _():
        for back in range(EXPERT_OUT_SLOTS, 0, -1):
            @pl.when(nused >= back)
            def _():
                store(nused - back, (nused - back) % EXPERT_OUT_SLOTS).wait()

        ybuf[0] = jnp.zeros((bm, LANES), U32)

        def fill(b, c):
            store(b, 0).start()
            return c

        def drain(b, c):
            store(b, 0).wait()
            return c

        lax.fori_loop(nused, n_blocks, fill, 0)
        lax.fori_loop(nused, n_blocks, drain, 0)


def _expert_call(first_blk, xs, w_e_gate, w_e_up, w_e_down, l):
    n_slots = xs.shape[0]
    bm = EXPERT_BLOCK

    def w_idx(e, first_ref):
        return (l, e, 0, 0)

    grid_spec = pltpu.PrefetchScalarGridSpec(
        num_scalar_prefetch=1,
        grid=(N_EXPERTS,),
        in_specs=[
            pl.BlockSpec(memory_space=pl.ANY),
            pl.BlockSpec((None, None, D_MODEL, EXPERT_FF), w_idx),
            pl.BlockSpec((None, None, D_MODEL, EXPERT_FF), w_idx),
            pl.BlockSpec((None, None, EXPERT_FF, D_MODEL), w_idx),
        ],
        out_specs=pl.BlockSpec(memory_space=pl.ANY),
        scratch_shapes=[pltpu.VMEM((EXPERT_IN_SLOTS, ROW_SUB * bm, LANES), U32),
                        pltpu.VMEM((EXPERT_OUT_SLOTS, ROW_SUB * bm, LANES), U32),
                        pltpu.SemaphoreType.DMA((EXPERT_IN_SLOTS,)),
                        pltpu.SemaphoreType.DMA((EXPERT_OUT_SLOTS,))],
    )
    return pl.pallas_call(
        _expert_kernel,
        out_shape=jax.ShapeDtypeStruct(xs.shape, U32),
        grid_spec=grid_spec,
        compiler_params=_params(("arbitrary",)),
        name="experts",
    )(first_blk, xs, w_e_gate, w_e_up, w_e_down)


def _combine_kernel(n_ctx_tiles, slots_ref, slots_next_ref, gw_ref, x1_ref, h2_ref, mod_ref, wsg_ref, wsu_ref,
                    wsd_ref, ys_ref, oc_ref, ol_ref, buf, sems, acc_scr):
    i = pl.program_id(0)
    n_tok = x1_ref.shape[0]
    half = i % 2
    grp = SUBLANES

    def issue(idx_ref, dst_half, t):
        dst_rows = pl.ds(pl.multiple_of(t * ROW_SUB, ROW_SUB), ROW_SUB)
        for k in range(TOP_K):
            row0 = pl.multiple_of(idx_ref[t * TOP_K + k], ROW_SUB)
            pltpu.make_async_copy(ys_ref.at[pl.ds(row0, ROW_SUB), :], buf.at[dst_half, k, dst_rows, :],
                                  sems.at[dst_half]).start(priority=k % 2)

    @pl.when(i == 0)
    def _():
        def body(t, c):
            issue(slots_ref, 0, t)
            return c
        lax.fori_loop(0, n_tok, body, 0)

    ha, hb = _load_rows(h2_ref, n_tok)
    ha = ha.astype(BF16)
    hb = hb.astype(BF16)
    gate = (jnp.dot(ha, wsg_ref[:PACKED, :], preferred_element_type=F32)
            + jnp.dot(hb, wsg_ref[PACKED:, :], preferred_element_type=F32))
    up = (jnp.dot(ha, wsu_ref[:PACKED, :], preferred_element_type=F32)
          + jnp.dot(hb, wsu_ref[PACKED:, :], preferred_element_type=F32))
    acc_scr[...] = jnp.dot((_silu(gate) * up).astype(BF16), wsd_ref[...], preferred_element_type=F32)

    for k in range(TOP_K):
        pltpu.make_async_copy(ys_ref.at[pl.ds(0, n_tok * ROW_SUB), :], buf.at[half, k], sems.at[half]).wait()
    g2 = mod_ref[:, 5 * D_MODEL:6 * D_MODEL]

    def reduce_group(src_half, r0):
        gw = gw_ref[pl.ds(r0, grp), :]
        acc = [acc_scr[pl.ds(r0, grp), j * LANES:(j + 1) * LANES] for j in range(2 * ROW_SUB)]
        for k in range(TOP_K):
            wk = gw[:, k:k + 1]
            src = buf.at[src_half, k]
            for j in range(ROW_SUB):
                hi, lo = _unpack_bf16_pair(src[pl.ds(r0 * ROW_SUB + j, grp, stride=ROW_SUB), :])
                acc[j] = acc[j] + wk * hi
                acc[ROW_SUB + j] = acc[ROW_SUB + j] + wk * lo
        acc_scr[pl.ds(r0, grp), :] = x1_ref[pl.ds(r0, grp), :] + g2 * jnp.concatenate(acc, axis=1)

    def run(src_half, prefetch):
        def body(g, c):
            r0 = pl.multiple_of(g * grp, grp)
            if prefetch:
                for dt in range(grp):
                    issue(slots_next_ref, 1 - src_half, r0 + dt)
            reduce_group(src_half, r0)
            return c
        lax.fori_loop(0, n_tok // grp, body, 0)

    has_next = i + 1 < pl.num_programs(0)
    for src_half in range(2):
        @pl.when(jnp.logical_and(has_next, half == src_half))
        def _():
            run(src_half, True)

        @pl.when(jnp.logical_and(jnp.logical_not(has_next), half == src_half))
        def _():
            run(src_half, False)

    @pl.when(i < n_ctx_tiles)
    def _():
        oc_ref[...] = acc_scr[...]

    @pl.when(i >= n_ctx_tiles)
    def _():
        ol_ref[...] = acc_scr[...]


def _combine_call(slots_flat, gw, x1, h2, mod3, ys, l, P, n_ctx, lat_t):
    n = x1.shape[0]
    tm = COMBINE_TILE
    n_ctx_tiles = n_ctx // tm
    tiles_per_seq = lat_t // tm

    def mod_idx(i):
        return (jnp.where(i < n_ctx_tiles, 0, 1 + (i - n_ctx_tiles) // tiles_per_seq), 0, 0)

    const = lambda i: (0, 0)
    row = lambda i: (i, 0)
    n_tiles = n // tm
    return pl.pallas_call(
        functools.partial(_combine_kernel, n_ctx_tiles),
        out_shape=(jax.ShapeDtypeStruct((n_ctx, D_MODEL), F32),
                   jax.ShapeDtypeStruct((n - n_ctx, D_MODEL), F32)),
        grid=(n_tiles,),
        in_specs=[
            pl.BlockSpec((tm * TOP_K,), lambda i: (i,), memory_space=pltpu.SMEM),
            pl.BlockSpec((tm * TOP_K,), lambda i: (jnp.minimum(i + 1, n_tiles - 1),), memory_space=pltpu.SMEM),
            pl.BlockSpec((tm, TOP_K), row),
            pl.BlockSpec((tm, D_MODEL), row),
            pl.BlockSpec((ROW_SUB * tm, LANES), row),
            pl.BlockSpec((None, 1, 6 * D_MODEL), mod_idx),
            pl.BlockSpec((D_MODEL, EXPERT_FF), const),
            pl.BlockSpec((D_MODEL, EXPERT_FF), const),
            pl.BlockSpec((EXPERT_FF, D_MODEL), const),
            pl.BlockSpec(memory_space=pl.ANY),
        ],
        out_specs=(pl.BlockSpec((tm, D_MODEL), lambda i: (jnp.minimum(i, n_ctx_tiles - 1), 0)),
                   pl.BlockSpec((tm, D_MODEL), lambda i: (jnp.maximum(i - n_ctx_tiles, 0), 0))),
        scratch_shapes=[pltpu.VMEM((2, TOP_K, ROW_SUB * tm, LANES), U32),
                        pltpu.SemaphoreType.DMA((2,)),
                        pltpu.VMEM((tm, D_MODEL), F32)],
        compiler_params=_params(("arbitrary",)),
        name="combine",
    )(slots_flat, slots_flat, gw, x1, h2, mod3, P["w_s_gate_bf16"][l], P["w_s_up_bf16"][l],
      P["w_s_down_bf16"][l], ys)


def _rope_tables(lat_t):
    rows = lat_t // GRID_W
    row = jnp.repeat(jnp.arange(rows, dtype=F32), GRID_W)
    col = jnp.tile(jnp.arange(GRID_W, dtype=F32), rows)
    n_freq = HEAD_DIM // 4
    inv = jnp.power(ROPE_BASE, -jnp.arange(n_freq, dtype=F32) / n_freq)
    ang = jnp.concatenate([row[:, None] * inv, col[:, None] * inv], axis=-1)
    cos = jnp.repeat(jnp.cos(ang), 2, axis=-1)
    sign = jnp.tile(jnp.array([-1.0, 1.0], F32), HEAD_DIM // 2)
    sin = jnp.repeat(jnp.sin(ang), 2, axis=-1) * sign
    cos = jnp.concatenate([cos, jnp.ones((TOKEN_TILE, HEAD_DIM), F32)], axis=0)
    sin = jnp.concatenate([sin, jnp.zeros((TOKEN_TILE, HEAD_DIM), F32)], axis=0)
    reps = LANES // HEAD_DIM
    return jnp.tile(cos, (1, reps)), jnp.tile(sin, (1, reps))


def _block_diag_gates(w_a, w_x):
    def dense(w):
        eye = jnp.eye(LRU_HEADS, dtype=w.dtype)
        return jnp.einsum("ncd,nm->ncmd", w, eye).reshape(LRU_WIDTH, LRU_WIDTH)
    return jnp.concatenate([dense(w_a[0]), dense(w_x[0]), dense(w_a[1]), dense(w_x[1])], axis=1)


def _prepare(P):
    Q = dict(P)
    Q["w_in_bf16"] = P["w_in"].astype(BF16)
    Q["w_out_bf16"] = P["w_out"].astype(BF16)
    Q["q_norm_g_t"] = jnp.tile(P["q_norm_g"], (1, N_HEADS))
    Q["k_norm_g_t"] = jnp.tile(P["k_norm_g"], (1, N_KV_HEADS))
    head = jnp.arange(ATTN_WIDTH) // HEAD_DIM
    Q["head_blockdiag"] = (head[:, None] == head[None, :]).astype(BF16)
    Q["lru_gate_w"] = jnp.stack([_block_diag_gates(P["lru_w_a"][l], P["lru_w_x"][l])
                                 for l in range(DEPTH)]).astype(BF16)
    Q["lru_gate_b"] = jnp.concatenate([P["lru_b_a"][:, 0], P["lru_b_x"][:, 0],
                                       P["lru_b_a"][:, 1], P["lru_b_x"][:, 1]], axis=-1)
    w_hi = P["w_router"].astype(BF16)
    Q["w_router_hi"] = w_hi
    Q["w_router_lo"] = (P["w_router"] - w_hi.astype(F32)).astype(BF16)
    Q["w_s_gate_bf16"] = P["w_s_gate"].astype(BF16)
    Q["w_s_up_bf16"] = P["w_s_up"].astype(BF16)
    Q["w_s_down_bf16"] = P["w_s_down"].astype(BF16)
    return Q


def _group_tables(counts):
    bm = EXPERT_BLOCK
    blocks = (counts.astype(I32) + bm - 1) // bm
    first_blk = jnp.concatenate([jnp.zeros((1,), I32), jnp.cumsum(blocks).astype(I32)])
    group_start = (first_blk[:N_EXPERTS] * bm).astype(F32)[None, :]
    return group_start, first_blk


def _trunk(x_prompt, x_sample, c, cache_k, cache_v, state_lru, c_ctx, P):
    n_seq_c, ctx_t, _ = x_prompt.shape
    n_seq_l, lat_t, _ = x_sample.shape
    past = cache_k.shape[2]
    n_ctx = n_seq_c * ctx_t
    n_lat = n_seq_l * lat_t
    n = n_ctx + n_lat
    assert n_ctx % lat_t == 0, "latent sequences must start on a whole-sequence boundary of the merged token axis"
    lat_off = n_ctx // lat_t
    P = _prepare(P)

    n_cond = -(-(1 + n_seq_l) // SUBLANES) * SUBLANES
    cond = jnp.concatenate([c_ctx[None, :], c, jnp.zeros((n_cond - 1 - n_seq_l, D_MODEL), F32)], axis=0)
    mods = _mod_call(cond, P["w_mod"], P["b_mod"])
    rope_cos, rope_sin = _rope_tables(lat_t)
    tri = (jnp.arange(TOKEN_TILE)[:, None] > jnp.arange(TOKEN_TILE)[None, :]).astype(BF16)
    n_blocks = n * TOP_K // EXPERT_BLOCK + N_EXPERTS
    n_slots = n_blocks * EXPERT_BLOCK

    x_c = x_prompt.reshape(n_ctx, D_MODEL)
    x_l = x_sample.reshape(n_lat, D_MODEL)
    ks, vs, ss = [], [], []
    for l in range(DEPTH):
        mod3 = mods[l].reshape(n_cond, 1, 6 * D_MODEL)
        q, kr, vb, kc, vc, u, gate, cv = _premix_call(x_c, x_l, mod3, l, P, rope_cos, rope_sin, n_ctx, lat_t)
        ks.append(kc[:n_ctx].reshape(n_seq_c, ctx_t, N_KV_HEADS, HEAD_DIM))
        vs.append(vc[:n_ctx].reshape(n_seq_c, ctx_t, N_KV_HEADS, HEAD_DIM))

        as_ctx = lambda a: a.reshape(n // ctx_t, ctx_t, a.shape[-1])
        as_lat = lambda a: a.reshape(n // lat_t, lat_t, a.shape[-1])

        attn_c = _attn_call(as_ctx(q), as_ctx(kr), as_ctx(vb), min(ctx_t, 256), n_seq_c, 0, 0)
        k_all = jnp.concatenate([kr[n_ctx:].reshape(n_seq_l, lat_t, KV_WIDTH),
                                 cache_k[:, l].reshape(n_seq_l, past, KV_WIDTH).astype(BF16)], axis=1)
        cv_ones = jnp.concatenate([cache_v[:, l], jnp.ones_like(cache_v[:, l])], axis=-1)
        v_all = jnp.concatenate([vb[n_ctx:].reshape(n_seq_l, lat_t, 2 * KV_WIDTH),
                                 cv_ones.reshape(n_seq_l, past, 2 * KV_WIDTH).astype(BF16)], axis=1)
        attn_l = _attn_call(as_lat(q), k_all, v_all, 256, n_seq_l, lat_off, 0)

        lru_c, fin_c = _lru_call(as_ctx(u), as_ctx(gate), jnp.zeros((n_seq_c, 2, LRU_WIDTH), F32), l, P,
                                 n_seq_c, 0)
        lru_l, _ = _lru_call(as_lat(u), as_lat(gate), state_lru[:, l], l, P, n_seq_l, lat_off)
        ss.append(fin_c)
        conv_c = _convmod_call(as_ctx(cv), l, P, n_seq_c, 0)
        conv_l = _convmod_call(as_lat(cv), l, P, n_seq_l, lat_off)

        x1, h2, ids, gw, rank, counts = _postmix_call(
            attn_c.reshape(n_ctx, ATTN_WIDTH), attn_l.reshape(n_lat, ATTN_WIDTH),
            lru_c.reshape(n_ctx, LRU_WIDTH), lru_l.reshape(n_lat, LRU_WIDTH),
            conv_c.reshape(n_ctx, CONV_WIDTH), conv_l.reshape(n_lat, CONV_WIDTH),
            x_c, x_l, mod3, tri, l, P, n_ctx, lat_t)
        group_start, first_blk = _group_tables(counts[0])
        slots = _slots_call(ids, rank, group_start).reshape(-1)
        xs = _dispatch_call(first_blk, slots, h2, n_slots)
        ys = _expert_call(first_blk, xs, P["w_e_gate"], P["w_e_up"], P["w_e_down"], l)
        x_c, x_l = _combine_call(slots, gw, x1, h2, mod3, ys, l, P, n_ctx, lat_t)

    y_prompt = x_c.reshape(n_seq_c, ctx_t, D_MODEL)
    y_sample = x_l.reshape(n_seq_l, lat_t, D_MODEL)
    return (y_prompt, y_sample, jnp.stack(ks, axis=1), jnp.stack(vs, axis=1), jnp.stack(ss, axis=1))


def kernel(x_prompt, x_sample, c, cache_k, cache_v, state_lru, c_ctx, w_mod, b_mod, norm1_g, w_in, q_norm_g, k_norm_g, lru_conv_w, lru_conv_b, lru_w_a, lru_b_a, lru_w_x, lru_b_x, lru_lambda, cm_dw_w, cm_dw_b, cm_ln_g, cm_ln_b, out_norm_g, w_out, norm2_g, w_router, b_router, w_e_gate, w_e_up, w_e_down, w_s_gate, w_s_up, w_s_down):
    P = {"w_mod": w_mod, "b_mod": b_mod, "norm1_g": norm1_g, "w_in": w_in, "q_norm_g": q_norm_g,
         "k_norm_g": k_norm_g, "lru_conv_w": lru_conv_w, "lru_conv_b": lru_conv_b, "lru_w_a": lru_w_a,
         "lru_b_a": lru_b_a, "lru_w_x": lru_w_x, "lru_b_x": lru_b_x, "lru_lambda": lru_lambda,
         "cm_dw_w": cm_dw_w, "cm_dw_b": cm_dw_b, "cm_ln_g": cm_ln_g, "cm_ln_b": cm_ln_b,
         "out_norm_g": out_norm_g, "w_out": w_out, "norm2_g": norm2_g, "w_router": w_router,
         "b_router": b_router, "w_e_gate": w_e_gate, "w_e_up": w_e_up, "w_e_down": w_e_down,
         "w_s_gate": w_s_gate, "w_s_up": w_s_up, "w_s_down": w_s_down}
    return _trunk(x_prompt, x_sample, c, cache_k, cache_v, state_lru, c_ctx, P)
```
